```python
import jax, jax.numpy as jnp
from jax import lax
import numpy as np

D_MODEL = 1024
BATCH = 16
SEQ = 256
DEPTH = 4
DEC_BATCH = 8
DEC_SEQ = 1024
PAST_LEN = 512

GRID_W = 64
MLA_HEADS = 4
QK_NOPE = 128
QK_ROPE = 64
V_HEAD = 128
Q_RANK = 256
KV_RANK = 128
MLA_WIDTH = MLA_HEADS * V_HEAD
CONV_GROUPS = 4
CONV_WIDTH = 256
CONV_K = 3
CMLP_GROUPS = 4
CMLP_WIDTH = 256
CHUNK = 128
MIX_WIDTH = MLA_WIDTH + CONV_WIDTH + CMLP_WIDTH
IN_WIDTH = Q_RANK + KV_RANK + QK_ROPE + 3 * CONV_WIDTH + 2 * CMLP_WIDTH
N_EXPERTS = 16
EC_FACTOR = 2
EXPERT_FF = D_MODEL // 2
ROPE_BASE = 10000.0
EPS = 1e-6
Q_BLOCK = 128

kernel_name = "hybrid_mla_conv_gmlp_ec_diffusion_step"


def rmsnorm(x, g):
    xf = x.astype(jnp.float32)
    y = xf * lax.rsqrt(jnp.mean(xf * xf, axis=-1, keepdims=True) + EPS)
    return (y * g.astype(jnp.float32)).astype(x.dtype)


def axial_rope_tables(n_tokens, dtype):
    rows = n_tokens // GRID_W
    row = jnp.repeat(jnp.arange(rows), GRID_W).astype(jnp.float32)
    col = jnp.tile(jnp.arange(GRID_W), rows).astype(jnp.float32)
    n_freq = QK_ROPE // 4
    inv = ROPE_BASE ** (-jnp.arange(n_freq, dtype=jnp.float32) / n_freq)
    ang_r = row[:, None] * inv
    ang_c = col[:, None] * inv
    ang = jnp.concatenate([ang_r, ang_r, ang_c, ang_c], axis=-1)
    return jnp.cos(ang).astype(dtype), jnp.sin(ang).astype(dtype)


def apply_axial_rope(x, cos, sin):
    x1, x2, x3, x4 = jnp.split(x, 4, axis=-1)
    rot = jnp.concatenate([-x2, x1, -x4, x3], axis=-1)
    return x * cos + rot * sin


def block_attention(q, k, v):
    b, lq, h, dk = q.shape
    nb = lq // Q_BLOCK
    qb = q.reshape(b, nb, Q_BLOCK, h, dk).transpose(1, 0, 2, 3, 4)
    scale = dk ** -0.5

    def one_block(qi):
        s = jnp.einsum('bqhd,bkhd->bhqk', qi, k).astype(jnp.float32) * scale
        p = jax.nn.softmax(s, axis=-1).astype(v.dtype)
        return jnp.einsum('bhqk,bkhd->bqhd', p, v)

    o = lax.map(one_block, qb)
    return o.transpose(1, 0, 2, 3, 4).reshape(b, lq, h, v.shape[-1])


def token_mixers(h, rope, ctx_ckv, ctx_krope, p):
    b, n, _ = h.shape
    proj = h @ p['w_in']
    offs = [Q_RANK, Q_RANK + KV_RANK, Q_RANK + KV_RANK + QK_ROPE,
            Q_RANK + KV_RANK + QK_ROPE + 3 * CONV_WIDTH]
    c_q, c_kv, k_pe, conv_in, cmlp_in = jnp.split(proj, offs, axis=-1)

    q = (rmsnorm(c_q, p['q_norm_g']) @ p['w_uq']).reshape(b, n, MLA_HEADS, QK_NOPE + QK_ROPE)
    q_nope, q_pe = q[..., :QK_NOPE], q[..., QK_NOPE:]
    c_kv = rmsnorm(c_kv, p['kv_norm_g'])
    if rope is None:
        all_ckv, all_kpe = c_kv, k_pe
    else:
        cos, sin = rope
        q_pe = apply_axial_rope(q_pe, cos[:, None, :], sin[:, None, :])
        k_pe_rot = apply_axial_rope(k_pe, cos, sin)
        all_ckv = jnp.concatenate([ctx_ckv, c_kv], axis=1)
        all_kpe = jnp.concatenate([ctx_krope, k_pe_rot], axis=1)
    lk = all_ckv.shape[1]
    kv = (all_ckv @ p['w_ukv']).reshape(b, lk, MLA_HEADS, QK_NOPE + V_HEAD)
    k_nope, v = kv[..., :QK_NOPE], kv[..., QK_NOPE:]
    k = jnp.concatenate(
        [k_nope, jnp.broadcast_to(all_kpe[:, :, None, :], (b, lk, MLA_HEADS, QK_ROPE))], axis=-1)
    qf = jnp.concatenate([q_nope, q_pe], axis=-1)
    attn_out = block_attention(qf, k, v).reshape(b, n, MLA_WIDTH)

    g_b, g_c, xc = jnp.split(conv_in, 3, axis=-1)
    z = jnp.pad(g_c * xc, ((0, 0), (1, 1), (0, 0)))
    cw = p['conv_w']
    conv = z[:, :-2] * cw[0] + z[:, 1:-1] * cw[1] + z[:, 2:] * cw[2]
    conv_out = g_b * conv

    u, vv = jnp.split(cmlp_in, 2, axis=-1)
    nc = n // CHUNK
    vg = vv.reshape(b, nc, CHUNK, CMLP_GROUPS, CMLP_WIDTH // CMLP_GROUPS)
    mixed = (jnp.einsum('gpq,bcqgd->bcpgd', p['spatial_w'], vg)
             + p['spatial_b'].T[:, :, None])
    cmlp_out = u * mixed.reshape(b, n, CMLP_WIDTH)

    out = jnp.concatenate([attn_out, conv_out, cmlp_out], axis=-1) @ p['w_out']
    return out, c_kv, k_pe


def expert_choice_ffn(h, p):
    b, n, d = h.shape
    cap = (EC_FACTOR * n) // N_EXPERTS
    logits = jnp.einsum('bnd,de->bne', h, p['w_router']).astype(jnp.float32)
    aff = jax.nn.softmax(logits, axis=-1)
    gates, idx = lax.top_k(aff.transpose(0, 2, 1), cap)
    xs = jax.vmap(lambda hb, ib: hb[ib])(h, idx)
    hid = (jax.nn.silu(jnp.einsum('becd,edf->becf', xs, p['w_gate']))
           * jnp.einsum('becd,edf->becf', xs, p['w_up']))
    ye = jnp.einsum('becf,efd->becd', hid, p['w_down']) * gates[..., None].astype(h.dtype)
    flat = (idx + (jnp.arange(b) * n)[:, None, None]).reshape(-1)
    out = jnp.zeros((b * n, d), h.dtype).at[flat].add(ye.reshape(-1, d))
    return out.reshape(b, n, d)


def layer_step(x, mod, rope, ctx_ckv, ctx_krope, p):
    sh1, sc1, g1, sh2, sc2, g2 = jnp.split(mod, 6, axis=-1)
    h = rmsnorm(x, p['norm1_g']) * (1 + sc1) + sh1
    mix, ckv, kpe = token_mixers(h, rope, ctx_ckv, ctx_krope, p)
    x = x + g1 * mix
    h = rmsnorm(x, p['norm2_g']) * (1 + sc2) + sh2
    x = x + g2 * expert_choice_ffn(h, p)
    return x, ckv, kpe


def setup_inputs(seed: int = 0) -> dict:
    key = jax.random.key(seed)
    ks = jax.random.split(key, 26)
    f32 = jnp.float32

    def nrm(k, shape, scale):
        return jax.random.normal(k, shape, f32) * scale

    return {
        "x_prompt": nrm(ks[0], (BATCH, SEQ, D_MODEL), 1.0),
        "x_sample": nrm(ks[1], (DEC_BATCH, DEC_SEQ, D_MODEL), 1.0),
        "cache_ckv": nrm(ks[2], (DEC_BATCH, DEPTH, PAST_LEN, KV_RANK), 1.0),
        "cache_krope": nrm(ks[3], (DEC_BATCH, DEPTH, PAST_LEN, QK_ROPE), 1.0),
        "c": nrm(ks[4], (DEC_BATCH, D_MODEL), 1.0),
        "c_ctx": nrm(ks[5], (D_MODEL,), 1.0),
        "w_ada": nrm(ks[6], (DEPTH, D_MODEL, 6 * D_MODEL), 0.5 * D_MODEL ** -0.5),
        "b_ada": nrm(ks[7], (DEPTH, 6 * D_MODEL), 0.02),
        "norm1_g": 1.0 + nrm(ks[8], (DEPTH, D_MODEL), 0.05),
        "norm2_g": 1.0 + nrm(ks[9], (DEPTH, D_MODEL), 0.05),
        "w_in": nrm(ks[10], (DEPTH, D_MODEL, IN_WIDTH), D_MODEL ** -0.5),
        "q_norm_g": 1.0 + nrm(ks[11], (DEPTH, Q_RANK), 0.05),
        "kv_norm_g": 1.0 + nrm(ks[12], (DEPTH, KV_RANK), 0.05),
        "w_uq": nrm(ks[13], (DEPTH, Q_RANK, MLA_HEADS * (QK_NOPE + QK_ROPE)), Q_RANK ** -0.5),
        "w_ukv": nrm(ks[14], (DEPTH, KV_RANK, MLA_HEADS * (QK_NOPE + V_HEAD)), KV_RANK ** -0.5),
        "conv_w": nrm(ks[15], (DEPTH, CONV_K, CONV_WIDTH), CONV_K ** -0.5),
        "spatial_w": nrm(ks[16], (DEPTH, CMLP_GROUPS, CHUNK, CHUNK), CHUNK ** -0.5),
        "spatial_b": 1.0 + nrm(ks[17], (DEPTH, CMLP_GROUPS, CHUNK), 0.1),
        "w_out": nrm(ks[18], (DEPTH, MIX_WIDTH, D_MODEL), MIX_WIDTH ** -0.5),
        "w_router": nrm(ks[19], (DEPTH, D_MODEL, N_EXPERTS), D_MODEL ** -0.5),
        "w_gate": nrm(ks[20], (DEPTH, N_EXPERTS, D_MODEL, EXPERT_FF), D_MODEL ** -0.5),
        "w_up": nrm(ks[21], (DEPTH, N_EXPERTS, D_MODEL, EXPERT_FF), D_MODEL ** -0.5),
        "w_down": nrm(ks[22], (DEPTH, N_EXPERTS, EXPERT_FF, D_MODEL), EXPERT_FF ** -0.5),
        "final_norm_g": 1.0 + nrm(ks[23], (D_MODEL,), 0.05),
    }


def reference(x_prompt, x_sample, cache_ckv, cache_krope, c, c_ctx, w_ada, b_ada,
              norm1_g, norm2_g, w_in, q_norm_g, kv_norm_g, w_uq, w_ukv, conv_w,
              spatial_w, spatial_b, w_out, w_router, w_gate, w_up, w_down, final_norm_g):
    rope = axial_rope_tables(x_sample.shape[1], x_sample.dtype)
    xp, xs = x_prompt, x_sample
    ckv_list, kpe_list = [], []
    for l in range(DEPTH):
        p = dict(norm1_g=norm1_g[l], norm2_g=norm2_g[l], w_in=w_in[l], q_norm_g=q_norm_g[l],
                 kv_norm_g=kv_norm_g[l], w_uq=w_uq[l], w_ukv=w_ukv[l], conv_w=conv_w[l],
                 spatial_w=spatial_w[l], spatial_b=spatial_b[l], w_out=w_out[l],
                 w_router=w_router[l], w_gate=w_gate[l], w_up=w_up[l], w_down=w_down[l])
        mod_ctx = (jax.nn.silu(c_ctx) @ w_ada[l] + b_ada[l])[None, None, :]
        xp, ckv, kpe = layer_step(xp, mod_ctx, None, None, None, p)
        ckv_list.append(ckv)
        kpe_list.append(kpe)
        mod_lat = (jax.nn.silu(c) @ w_ada[l] + b_ada[l])[:, None, :]
        xs, _, _ = layer_step(xs, mod_lat, rope, cache_ckv[:, l], cache_krope[:, l], p)
    y_prompt = rmsnorm(xp, final_norm_g)
    y_sample = rmsnorm(xs, final_norm_g)
    new_ckv = jnp.stack(ckv_list, axis=1)
    new_krope = jnp.stack(kpe_list, axis=1)
    return (y_prompt, y_sample, new_ckv, new_krope)
```

```python
import functools

import jax
import jax.numpy as jnp
from jax import lax
from jax.experimental import pallas as pl
from jax.experimental.pallas import tpu as pltpu

F32 = jnp.float32
BF16 = jnp.bfloat16
I32 = jnp.int32

D = 1024
BATCH = 16
SEQ = 256
DEPTH = 4
DEC_BATCH = 8
DEC_SEQ = 1024
PAST_LEN = 512
GRID_W = 64
HEADS = 4
QK_NOPE = 128
QK_ROPE = 64
V_HEAD = 128
Q_RANK = 256
KV_RANK = 128
CONV_W = 256
CMLP_W = 256
CMLP_G = 4
CHUNK = 128
N_EXP = 16
EC_FACTOR = 2
FF = D // 2
ROPE_BASE = 10000.0
EPS = 1e-6

ROW = 1024
N_ROWS = (BATCH * SEQ + DEC_BATCH * DEC_SEQ) // ROW
P_ROWS = BATCH * SEQ // ROW
P_PER_ROW = ROW // SEQ
CAP = EC_FACTOR * ROW // N_EXP
CAP_P = EC_FACTOR * SEQ // N_EXP
CTX_MOD_ROW = DEC_BATCH
MOD_ROWS = 16

C_Q, C_KV, C_KPE, C_KPR, C_CONV, C_CMLP, C_END = 0, 256, 384, 512, 640, 1408, 1920
TB = 256
TQ = 256
VMEM_LIMIT = 58 * 1024 * 1024

NT = (((1,), (1,)), ((), ()))
TN = (((0,), (0,)), ((), ()))


def _rms(x, g):
    return x * lax.rsqrt(jnp.mean(x * x, axis=-1, keepdims=True) + EPS) * g


def _silu(x):
    return x / (1.0 + jnp.exp(-x))


def _dot(a, b):
    return jnp.dot(a, b, preferred_element_type=F32)


def _ada_kernel(c_ref, w_ref, b_ref, o_ref):
    s = _silu(c_ref[...]).astype(BF16)
    o_ref[...] = _dot(s, w_ref[...].astype(BF16)) + b_ref[...]


def _ada_call(cond, w_ada, b_ada):
    nc = 6
    return pl.pallas_call(
        _ada_kernel,
        out_shape=jax.ShapeDtypeStruct((DEPTH, MOD_ROWS, 6 * D), F32),
        grid=(DEPTH, nc),
        in_specs=[
            pl.BlockSpec((MOD_ROWS, D), lambda l, j: (0, 0)),
            pl.BlockSpec((None, D, D), lambda l, j: (l, 0, j)),
            pl.BlockSpec((None, 1, D), lambda l, j: (l, 0, j)),
        ],
        out_specs=pl.BlockSpec((None, MOD_ROWS, D), lambda l, j: (l, 0, j)),
        compiler_params=pltpu.CompilerParams(
            dimension_semantics=("arbitrary", "arbitrary"), vmem_limit_bytes=VMEM_LIMIT),
        name="ada_mod",
    )(cond, w_ada, b_ada.reshape(DEPTH, 1, 6 * D))


def _make_mixer_kernel(n, n_ctx, rope, emit_ctx, n_alias):
    lk = n_ctx + n
    nblk = n // TB
    scale = float(QK_NOPE + QK_ROPE) ** -0.5

    def kern(*refs):
        it = iter(refs)
        x_ref, mod_ref, g1n_ref, g2n_ref, win_ref, qg_ref, kvg_ref, wuq_ref, wukv_ref = (
            next(it) for _ in range(9))
        cw_ref, sw_ref, sb_ref, wout_ref, wr_ref = (next(it) for _ in range(5))
        if rope:
            cos_ref, sin_ref = next(it), next(it)
        if n_ctx:
            cckv_ref, ckr_ref = next(it), next(it)
        for _ in range(n_alias):
            next(it)
        xmid_ref, h2_ref, aff_ref = next(it), next(it), next(it)
        if emit_ctx:
            ckv_out, kpe_out = next(it), next(it)
        qf_s, kf_s, v_s, mix_s, gb_s, z_s = (next(it) for _ in range(6))

        mod = mod_ref[...]
        sh1, sc1, g1 = mod[:, 0:D], mod[:, D:2 * D], mod[:, 2 * D:3 * D]
        sh2, sc2, g2 = mod[:, 3 * D:4 * D], mod[:, 4 * D:5 * D], mod[:, 5 * D:6 * D]

        def put_kv(row0, rows, ckv_n, kpe_pad):
            kv = _dot(ckv_n.astype(BF16), wukv_ref[...])
            kpb = kpe_pad.astype(BF16)
            for h in range(HEADS):
                c0 = h * (QK_NOPE + V_HEAD)
                kf_s[h, pl.ds(row0, rows), 0:QK_NOPE] = kv[:, c0:c0 + QK_NOPE].astype(BF16)
                kf_s[h, pl.ds(row0, rows), QK_NOPE:2 * QK_NOPE] = kpb
                v_s[h, pl.ds(row0, rows), :] = kv[:, c0 + QK_NOPE:c0 + QK_NOPE + V_HEAD].astype(BF16)

        if n_ctx:
            kr = ckr_ref[...]
            put_kv(0, n_ctx, cckv_ref[...], jnp.concatenate([kr, jnp.zeros_like(kr)], axis=1))

        grp_shift = (CMLP_W // CMLP_G).bit_length() - 1
        lane_grp = lax.shift_right_logical(lax.broadcasted_iota(I32, (CHUNK, CMLP_W), 1), grp_shift)

        def phase1(i, carry):
            r0 = pl.multiple_of(i * TB, TB)
            x = x_ref[pl.ds(r0, TB), :]
            hb = (_rms(x, g1n_ref[...]) * (1.0 + sc1) + sh1).astype(BF16)
            pa = _dot(hb, win_ref[:, C_Q:C_CONV])
            cq, ckv = pa[:, C_Q:C_KV], pa[:, C_KV:C_KPE]
            kp, kpr = pa[:, C_KPE:C_KPR], pa[:, C_KPR:C_CONV]
            if rope:
                cos, sin = cos_ref[pl.ds(r0, TB), :], sin_ref[pl.ds(r0, TB), :]
            qa = _dot(_rms(cq, qg_ref[...]).astype(BF16), wuq_ref[...])
            for h in range(HEADS):
                qn = qa[:, h * 128:(h + 1) * 128]
                qp = qa[:, 512 + h * 128:512 + (h + 1) * 128]
                if rope:
                    qp = qp * cos + qa[:, 1024 + h * 128:1024 + (h + 1) * 128] * sin
                qf_s[h, pl.ds(r0, TB), 0:128] = qn.astype(BF16)
                qf_s[h, pl.ds(r0, TB), 128:256] = qp.astype(BF16)
            ckv_n = _rms(ckv, kvg_ref[...])
            if emit_ctx:
                ckv_out[pl.ds(r0, TB), :] = ckv_n
                kpe_out[pl.ds(r0, TB), :] = kp[:, 0:QK_ROPE]
            kpe = kp * cos + kpr * sin if rope else kp
            put_kv(pl.multiple_of(n_ctx + r0, TB), TB, ckv_n, kpe)
            cv = _dot(hb, win_ref[:, C_CONV:C_CMLP])
            gb_s[pl.ds(r0, TB), :] = cv[:, 0:CONV_W]
            z_s[pl.ds(r0, TB), :] = cv[:, CONV_W:2 * CONV_W] * cv[:, 2 * CONV_W:3 * CONV_W]
            cm = _dot(hb, win_ref[:, C_CMLP:C_END])
            u, vvb = cm[:, 0:CMLP_W], cm[:, CMLP_W:2 * CMLP_W].astype(BF16)
            for c in range(TB // CHUNK):
                r = _dot(sw_ref[...], vvb[c * CHUNK:(c + 1) * CHUNK, :])
                mixed = jnp.where(
                    lane_grp == 0, r[0:CHUNK],
                    jnp.where(lane_grp == 1, r[CHUNK:2 * CHUNK],
                              jnp.where(lane_grp == 2, r[2 * CHUNK:3 * CHUNK], r[3 * CHUNK:4 * CHUNK])))
                out = u[c * CHUNK:(c + 1) * CHUNK, :] * (mixed + sb_ref[...])
                mix_s[pl.ds(pl.multiple_of(r0 + c * CHUNK, CHUNK), CHUNK), 768:1024] = out.astype(BF16)
            return carry

        lax.fori_loop(0, nblk, phase1, 0)

        z = z_s[...]
        row = lax.broadcasted_iota(I32, (n, CONV_W), 0)
        zm = jnp.where(row == 0, 0.0, pltpu.roll(z, 1, 0))
        zp = jnp.where(row == n - 1, 0.0, pltpu.roll(z, n - 1, 0))
        cw = cw_ref[...]
        conv = zm * cw[0:1, :] + z * cw[1:2, :] + zp * cw[2:3, :]
        mix_s[:, 512:768] = (gb_s[...] * conv).astype(BF16)

        def attn(i, carry):
            r0 = pl.multiple_of(i * TQ, TQ)
            for h in range(HEADS):
                s = lax.dot_general(qf_s[h, pl.ds(r0, TQ), :], kf_s[h], NT,
                                    preferred_element_type=F32) * scale
                p = jnp.exp(s - jnp.max(s, axis=-1, keepdims=True))
                o = _dot(p.astype(BF16), v_s[h]) / jnp.sum(p, axis=-1, keepdims=True)
                mix_s[pl.ds(r0, TQ), h * 128:(h + 1) * 128] = o.astype(BF16)
            return carry

        lax.fori_loop(0, n // TQ, attn, 0)

        wr = wr_ref[...]
        wr_hi = wr.astype(BF16)
        wr_lo = (wr - wr_hi.astype(F32)).astype(BF16)

        def phase2(i, carry):
            r0 = pl.multiple_of(i * TB, TB)
            mo = _dot(mix_s[pl.ds(r0, TB), :], wout_ref[...])
            xm = x_ref[pl.ds(r0, TB), :] + g1 * mo
            xmid_ref[pl.ds(r0, TB), :] = xm
            h2 = _rms(xm, g2n_ref[...]) * (1.0 + sc2) + sh2
            h2_hi = h2.astype(BF16)
            h2_ref[pl.ds(r0, TB), :] = h2_hi
            h2_lo = (h2 - h2_hi.astype(F32)).astype(BF16)
            lg = (lax.dot_general(wr_hi, h2_hi, NT, preferred_element_type=F32)
                  + lax.dot_general(wr_lo, h2_hi, NT, preferred_element_type=F32)
                  + lax.dot_general(wr_hi, h2_lo, NT, preferred_element_type=F32))
            e = jnp.exp(lg - jnp.max(lg, axis=0, keepdims=True))
            aff_ref[:, pl.ds(r0, TB)] = e / jnp.sum(e, axis=0, keepdims=True)
            return carry

        lax.fori_loop(0, nblk, phase2, 0)

    return kern


def _const_spec(shape):
    nd = len(shape)
    return pl.BlockSpec(shape, lambda b: (0,) * nd, pipeline_mode=pl.Buffered(1))


def _layer_spec(l, shape):
    nd = len(shape)
    return pl.BlockSpec((None,) + shape, lambda b: (l,) + (0,) * nd, pipeline_mode=pl.Buffered(1))


def _mixer_call(l, x, x_row0, mods, wts, rope_tabs, caches, aliased, *, n, n_req, first_row, mod_row):
    rope = rope_tabs is not None
    n_ctx = PAST_LEN if caches is not None else 0
    emit_ctx = caches is None
    lk = n_ctx + n
    kern = _make_mixer_kernel(n, n_ctx, rope, emit_ctx, len(aliased))

    in_specs = [
        pl.BlockSpec((None, n, D), lambda b: (x_row0 + b, 0, 0)),
        pl.BlockSpec((None, None, 1, 6 * D), lambda b: (l, mod_row(b), 0, 0)),
        _layer_spec(l, (1, D)), _layer_spec(l, (1, D)),
        _layer_spec(l, (D, C_END)),
        _layer_spec(l, (1, Q_RANK)), _layer_spec(l, (1, KV_RANK)),
        _layer_spec(l, (Q_RANK, 3 * HEADS * 128)),
        _layer_spec(l, (KV_RANK, HEADS * (QK_NOPE + V_HEAD))),
        _layer_spec(l, (3, CONV_W)),
        _layer_spec(l, (CMLP_G * CHUNK, CHUNK)),
        _layer_spec(l, (CHUNK, CMLP_W)),
        _layer_spec(l, (D, D)),
        _layer_spec(l, (N_EXP, D)),
    ]
    args = [x, mods, wts["norm1_g"], wts["norm2_g"], wts["w_in"], wts["q_norm_g"], wts["kv_norm_g"],
            wts["w_uq"], wts["w_ukv"], wts["conv_w"], wts["spatial_w"], wts["spatial_b"],
            wts["w_out"], wts["w_router_t"]]
    if rope:
        in_specs += [_const_spec((n, 128)), _const_spec((n, 128))]
        args += list(rope_tabs)
    if n_ctx:
        in_specs += [pl.BlockSpec((None, None, PAST_LEN, KV_RANK), lambda b: (b, l, 0, 0)),
                     pl.BlockSpec((None, None, PAST_LEN, QK_ROPE), lambda b: (b, l, 0, 0))]
        args += list(caches)
    in_specs += [pl.BlockSpec(memory_space=pl.ANY)] * len(aliased)
    args += list(aliased)

    rows_total = N_ROWS * ROW // n
    out_shape = [jax.ShapeDtypeStruct((rows_total, n, D), F32),
                 jax.ShapeDtypeStruct((rows_total, n, D), BF16),
                 jax.ShapeDtypeStruct((n_req * n // ROW, N_EXP, ROW), F32)]
    per_row = ROW // n
    out_specs = [pl.BlockSpec((None, n, D), lambda b: (first_row + b, 0, 0)),
                 pl.BlockSpec((None, n, D), lambda b: (first_row + b, 0, 0)),
                 pl.BlockSpec((None, N_EXP, n), lambda b: (b // per_row, 0, b % per_row))]
    if emit_ctx:
        out_shape += [jax.ShapeDtypeStruct((n_req, n, KV_RANK), F32),
                      jax.ShapeDtypeStruct((n_req, n, QK_ROPE), F32)]
        out_specs += [pl.BlockSpec((None, n, KV_RANK), lambda b: (b, 0, 0)),
                      pl.BlockSpec((None, n, QK_ROPE), lambda b: (b, 0, 0))]
    n_in = len(args)
    io_alias = {n_in - len(aliased) + k: k for k in range(len(aliased))}

    return pl.pallas_call(
        kern,
        out_shape=out_shape,
        grid=(n_req,),
        in_specs=in_specs,
        out_specs=out_specs,
        scratch_shapes=[
            pltpu.VMEM((HEADS, n, 256), BF16),
            pltpu.VMEM((HEADS, lk, 256), BF16),
            pltpu.VMEM((HEADS, lk, V_HEAD), BF16),
            pltpu.VMEM((n, D), BF16),
            pltpu.VMEM((n, CONV_W), F32),
            pltpu.VMEM((n, CONV_W), F32),
        ],
        input_output_aliases=io_alias,
        compiler_params=pltpu.CompilerParams(
            dimension_semantics=("arbitrary",), vmem_limit_bytes=VMEM_LIMIT),
        name="mixer_lat" if rope else "mixer_ctx",
    )(*args)


def _select_slots(aff, cap, tri):
    key = pltpu.bitcast(aff, I32)
    capf = float(cap)

    def bit_step(i, t):
        cand = t | jnp.left_shift(jnp.int32(1), 30 - i)
        cnt = jnp.sum(jnp.where(key >= cand, 1.0, 0.0), axis=1, keepdims=True)
        return jnp.where(cnt >= capf, cand, t)

    thr = lax.fori_loop(0, 31, bit_step, jnp.zeros((aff.shape[0], 1), I32))
    gt = key > thr
    eq = key == thr
    n_gt = jnp.sum(jnp.where(gt, 1.0, 0.0), axis=1, keepdims=True)
    eq_rank = _dot(jnp.where(eq, 1.0, 0.0).astype(BF16), tri)
    sel = jnp.where(gt, 1.0, jnp.where(eq, jnp.where(eq_rank < capf - n_gt, 1.0, 0.0), 0.0))
    pos = _dot(sel.astype(BF16), tri)
    return jnp.where(sel > 0.5, pos.astype(I32), -1)


def _route_kernel(ap_ref, as_ref, slot_ref, gate_ref, tri_s):
    r = lax.broadcasted_iota(I32, (ROW, ROW), 0)
    c = lax.broadcasted_iota(I32, (ROW, ROW), 1)
    tri_s[...] = jnp.where(r < c, 1.0, 0.0).astype(BF16)
    np_ = P_ROWS * N_EXP
    for s in range(P_PER_ROW):
        a = ap_ref[:, s * SEQ:(s + 1) * SEQ]
        sl = _select_slots(a, CAP_P, tri_s[0:SEQ, 0:SEQ])
        slot_ref[0:np_, s * SEQ:(s + 1) * SEQ] = jnp.where(sl >= 0, sl + s * CAP_P, -1)
    gate_ref[0:np_, :] = ap_ref[...]
    a = as_ref[...]
    slot_ref[np_:, :] = _select_slots(a, CAP, tri_s[...])
    gate_ref[np_:, :] = a


def _route_call(aff_p, aff_s):
    rows = N_ROWS * N_EXP
    return pl.pallas_call(
        _route_kernel,
        out_shape=[jax.ShapeDtypeStruct((rows, ROW), I32), jax.ShapeDtypeStruct((rows, ROW), F32)],
        scratch_shapes=[pltpu.VMEM((ROW, ROW), BF16)],
        compiler_params=pltpu.CompilerParams(vmem_limit_bytes=VMEM_LIMIT),
        name="route",
    )(aff_p.reshape(P_ROWS * N_EXP, ROW), aff_s.reshape(DEC_BATCH * N_EXP, ROW))


def _expert_kernel(slot_ref, gate_ref, h2_ref, wg_ref, wu_ref, wd_ref, ye_ref, wg_s, wu_s, wd_s):
    @pl.when(pl.program_id(1) == 0)
    def _():
        wg_s[...] = wg_ref[...].astype(BF16)
        wu_s[...] = wu_ref[...].astype(BF16)
        wd_s[...] = wd_ref[...].astype(BF16)

    hit = slot_ref[...] == lax.broadcasted_iota(I32, (CAP, ROW), 0)
    gate = jnp.sum(jnp.where(hit, gate_ref[...], 0.0), axis=1, keepdims=True)
    xs = _dot(jnp.where(hit, 1.0, 0.0).astype(BF16), h2_ref[...]).astype(BF16)
    hid = _silu(_dot(xs, wg_s[...])) * _dot(xs, wu_s[...])
    ye_ref[...] = (_dot(hid.astype(BF16), wd_s[...]) * gate).astype(BF16)


def _expert_call(l, slot, gate, h2, w_gate, w_up, w_down):
    return pl.pallas_call(
        _expert_kernel,
        out_shape=jax.ShapeDtypeStruct((N_EXP, N_ROWS, CAP, D), BF16),
        grid=(N_EXP, N_ROWS),
        in_specs=[
            pl.BlockSpec((None, None, 1, ROW), lambda e, r: (r, e, 0, 0)),
            pl.BlockSpec((None, None, 1, ROW), lambda e, r: (r, e, 0, 0)),
            pl.BlockSpec((None, ROW, D), lambda e, r: (r, 0, 0)),
            pl.BlockSpec((None, None, D, FF), lambda e, r: (l, e, 0, 0)),
            pl.BlockSpec((None, None, D, FF), lambda e, r: (l, e, 0, 0)),
            pl.BlockSpec((None, None, FF, D), lambda e, r: (l, e, 0, 0)),
        ],
        out_specs=pl.BlockSpec((None, None, CAP, D), lambda e, r: (e, r, 0, 0)),
        scratch_shapes=[pltpu.VMEM((D, FF), BF16), pltpu.VMEM((D, FF), BF16), pltpu.VMEM((FF, D), BF16)],
        compiler_params=pltpu.CompilerParams(
            dimension_semantics=("arbitrary", "arbitrary"), vmem_limit_bytes=VMEM_LIMIT),
        name="experts",
    )(slot.reshape(N_ROWS, N_EXP, 1, ROW), gate.reshape(N_ROWS, N_EXP, 1, ROW), h2, w_gate, w_up, w_down)


def _make_combine_kernel(final):
    tc = 256

    def kern(*refs):
        if final:
            slot_ref, ye_ref, xmid_ref, g2_ref, fg_ref, o_ref = refs
        else:
            slot_ref, ye_ref, xmid_ref, g2_ref, o_ref = refs
        ye = ye_ref[...].reshape(N_EXP * CAP, D)
        g2 = g2_ref[...]
        j = lax.broadcasted_iota(I32, (CAP, tc), 0)
        for t in range(ROW // tc):
            onehot = jnp.concatenate(
                [jnp.where(slot_ref[e, :, t * tc:(t + 1) * tc] == j, 1.0, 0.0).astype(BF16)
                 for e in range(N_EXP)], axis=0)
            moe = lax.dot_general(onehot, ye, TN, preferred_element_type=F32)
            xn = xmid_ref[t * tc:(t + 1) * tc, :] + g2 * moe
            if final:
                xn = _rms(xn, fg_ref[...])
            o_ref[t * tc:(t + 1) * tc, :] = xn

    return kern


def _combine_call(l, slot, ye, xmid, mods, final_g, row0, n_rows, mod_row):
    final = final_g is not None
    in_specs = [
        pl.BlockSpec((None, N_EXP, 1, ROW), lambda r: (row0 + r, 0, 0, 0)),
        pl.BlockSpec((N_EXP, None, CAP, D), lambda r: (0, row0 + r, 0, 0)),
        pl.BlockSpec((None, ROW, D), lambda r: (row0 + r, 0, 0)),
        pl.BlockSpec((None, None, 1, D), lambda r: (l, mod_row(r), 0, 5)),
    ]
    args = [slot.reshape(N_ROWS, N_EXP, 1, ROW), ye, xmid, mods]
    if final:
        in_specs.append(pl.BlockSpec((1, D), lambda r: (0, 0)))
        args.append(final_g)
    return pl.pallas_call(
        _make_combine_kernel(final),
        out_shape=jax.ShapeDtypeStruct((n_rows, ROW, D), F32),
        grid=(n_rows,),
        in_specs=in_specs,
        out_specs=pl.BlockSpec((None, ROW, D), lambda r: (r, 0, 0)),
        compiler_params=pltpu.CompilerParams(
            dimension_semantics=("arbitrary",), vmem_limit_bytes=VMEM_LIMIT),
        name="combine",
    )(*args)


def _rope_rot_cols(w):
    q = QK_ROPE // 4
    return jnp.concatenate([-w[..., q:2 * q], w[..., 0:q], -w[..., 3 * q:4 * q], w[..., 2 * q:3 * q]], axis=-1)


def _pad_lanes(w, width=128):
    return jnp.pad(w, [(0, 0)] * (w.ndim - 1) + [(0, width - w.shape[-1])])


def _prep_weights(norm1_g, norm2_g, w_in, q_norm_g, kv_norm_g, w_uq, w_ukv, conv_w, spatial_w,
                  spatial_b, w_out, w_router):
    o_kv, o_pe, o_conv = Q_RANK, Q_RANK + KV_RANK, Q_RANK + KV_RANK + QK_ROPE
    w_kpe = w_in[..., o_pe:o_conv]
    w_in_p = jnp.concatenate(
        [w_in[..., :o_pe], _pad_lanes(w_kpe), _pad_lanes(_rope_rot_cols(w_kpe)), w_in[..., o_conv:]],
        axis=-1).astype(BF16)
    uq = w_uq.reshape(DEPTH, Q_RANK, HEADS, QK_NOPE + QK_ROPE)
    uq_pe = uq[..., QK_NOPE:]
    w_uq_p = jnp.concatenate(
        [uq[..., :QK_NOPE].reshape(DEPTH, Q_RANK, HEADS * QK_NOPE),
         _pad_lanes(uq_pe).reshape(DEPTH, Q_RANK, HEADS * 128),
         _pad_lanes(_rope_rot_cols(uq_pe)).reshape(DEPTH, Q_RANK, HEADS * 128)], axis=-1).astype(BF16)
    sb = jnp.repeat(jnp.swapaxes(spatial_b, 1, 2), CMLP_W // CMLP_G, axis=-1)
    return dict(
        norm1_g=norm1_g.reshape(DEPTH, 1, D), norm2_g=norm2_g.reshape(DEPTH, 1, D),
        w_in=w_in_p, q_norm_g=q_norm_g.reshape(DEPTH, 1, Q_RANK),
        kv_norm_g=kv_norm_g.reshape(DEPTH, 1, KV_RANK), w_uq=w_uq_p, w_ukv=w_ukv.astype(BF16),
        conv_w=conv_w, spatial_w=spatial_w.reshape(DEPTH, CMLP_G * CHUNK, CHUNK).astype(BF16),
        spatial_b=sb, w_out=w_out.astype(BF16), w_router_t=jnp.swapaxes(w_router, 1, 2))


def _rope_tables(n):
    rows = n // GRID_W
    row = jnp.repeat(jnp.arange(rows), GRID_W).astype(F32)
    col = jnp.tile(jnp.arange(GRID_W), rows).astype(F32)
    n_freq = QK_ROPE // 4
    inv = ROPE_BASE ** (-jnp.arange(n_freq, dtype=F32) / n_freq)
    ang_r, ang_c = row[:, None] * inv, col[:, None] * inv
    ang = jnp.concatenate([ang_r, ang_r, ang_c, ang_c], axis=-1)
    return _pad_lanes(jnp.cos(ang)), _pad_lanes(jnp.sin(ang))


def kernel(x_prompt, x_sample, cache_ckv, cache_krope, c, c_ctx, w_ada, b_ada, norm1_g, norm2_g, w_in,
           q_norm_g, kv_norm_g, w_uq, w_ukv, conv_w, spatial_w, spatial_b, w_out, w_router, w_gate,
           w_up, w_down, final_norm_g):
    wts = _prep_weights(norm1_g, norm2_g, w_in, q_norm_g, kv_norm_g, w_uq, w_ukv, conv_w, spatial_w,
                        spatial_b, w_out, w_router)
    rope_tabs = _rope_tables(DEC_SEQ)
    cond = jnp.concatenate(
        [c, c_ctx[None, :], jnp.zeros((MOD_ROWS - DEC_BATCH - 1, D), F32)], axis=0)
    mods = _ada_call(cond, w_ada, b_ada).reshape(DEPTH, MOD_ROWS, 1, 6 * D)
    final_g = final_norm_g.reshape(1, D)

    ctx_mod = lambda b: CTX_MOD_ROW
    lat_mod = lambda b: b
    row_mod = lambda r: jnp.where(r < P_ROWS, CTX_MOD_ROW, r - P_ROWS)

    xp, xp_row0 = x_prompt, 0
    xs, xs_row0 = x_sample, 0
    ckv_list, kpe_list = [], []
    for l in range(DEPTH):
        xmid, h2, aff_p, ckv, kpe = _mixer_call(
            l, xp, xp_row0, mods, wts, None, None, (), n=SEQ, n_req=BATCH, first_row=0, mod_row=ctx_mod)
        ckv_list.append(ckv)
        kpe_list.append(kpe)
        xmid, h2, aff_s = _mixer_call(
            l, xs, xs_row0, mods, wts, rope_tabs, (cache_ckv, cache_krope),
            (xmid.reshape(N_ROWS, ROW, D), h2.reshape(N_ROWS, ROW, D)),
            n=DEC_SEQ, n_req=DEC_BATCH, first_row=P_ROWS, mod_row=lat_mod)
        slot, gate = _route_call(aff_p, aff_s)
        ye = _expert_call(l, slot, gate, h2, w_gate, w_up, w_down)
        if l < DEPTH - 1:
            x = _combine_call(l, slot, ye, xmid, mods, None, 0, N_ROWS, row_mod)
            xp, xp_row0 = x.reshape(N_ROWS * ROW // SEQ, SEQ, D), 0
            xs, xs_row0 = x, P_ROWS
        else:
            y_prompt = _combine_call(l, slot, ye, xmid, mods, final_g, 0, P_ROWS, ctx_mod)
            y_sample = _combine_call(l, slot, ye, xmid, mods, final_g, P_ROWS, DEC_BATCH,
                                     lambda r: r)
    return (y_prompt.reshape(BATCH, SEQ, D), y_sample,
            jnp.stack(ckv_list, axis=1), jnp.stack(kpe_list, axis=1))
```

```python
import functools

import jax
import jax.numpy as jnp
from jax import lax
from jax.experimental import pallas as pl
from jax.experimental.pallas import tpu as pltpu

F32 = jnp.float32
BF16 = jnp.bfloat16
I32 = jnp.int32

D = 1024
BATCH = 16
SEQ = 256
DEPTH = 4
DEC_BATCH = 8
DEC_SEQ = 1024
PAST_LEN = 512
GRID_W = 64
HEADS = 4
QK_NOPE = 128
QK_ROPE = 64
V_HEAD = 128
Q_RANK = 256
KV_RANK = 128
CONV_W = 256
CMLP_W = 256
CMLP_G = 4
CHUNK = 128
N_EXP = 16
EC_FACTOR = 2
FF = D // 2
ROPE_BASE = 10000.0
EPS = 1e-6

ROW = 1024
N_ROWS = (BATCH * SEQ + DEC_BATCH * DEC_SEQ) // ROW
P_ROWS = BATCH * SEQ // ROW
P_PER_ROW = ROW // SEQ
CAP = EC_FACTOR * ROW // N_EXP
CAP_P = EC_FACTOR * SEQ // N_EXP
CTX_MOD_ROW = DEC_BATCH
MOD_ROWS = 16

C_Q, C_KV, C_KPE, C_KPR, C_CONV, C_CMLP, C_END = 0, 256, 384, 512, 640, 1408, 1920
TB = 256
TQ = 256
VMEM_LIMIT = 58 * 1024 * 1024

NT = (((1,), (1,)), ((), ()))
TN = (((0,), (0,)), ((), ()))


def _rms(x, g):
    return x * lax.rsqrt(jnp.mean(x * x, axis=-1, keepdims=True) + EPS) * g


def _silu(x):
    return x / (1.0 + jnp.exp(-x))


def _dot(a, b):
    return jnp.dot(a, b, preferred_element_type=F32)


def _ada_kernel(c_ref, w_ref, b_ref, o_ref):
    s = _silu(c_ref[...]).astype(BF16)
    o_ref[...] = _dot(s, w_ref[...].astype(BF16)) + b_ref[...]


def _ada_call(cond, w_ada, b_ada):
    nc = 6
    return pl.pallas_call(
        _ada_kernel,
        out_shape=jax.ShapeDtypeStruct((DEPTH, MOD_ROWS, 6 * D), F32),
        grid=(DEPTH, nc),
        in_specs=[
            pl.BlockSpec((MOD_ROWS, D), lambda l, j: (0, 0)),
            pl.BlockSpec((None, D, D), lambda l, j: (l, 0, j)),
            pl.BlockSpec((None, 1, D), lambda l, j: (l, 0, j)),
        ],
        out_specs=pl.BlockSpec((None, MOD_ROWS, D), lambda l, j: (l, 0, j)),
        compiler_params=pltpu.CompilerParams(
            dimension_semantics=("arbitrary", "arbitrary"), vmem_limit_bytes=VMEM_LIMIT),
        name="ada_mod",
    )(cond, w_ada, b_ada.reshape(DEPTH, 1, 6 * D))


def _make_mixer_kernel(n, n_ctx, rope, emit_ctx, n_alias):
    lk = n_ctx + n
    nblk = n // TB
    scale = float(QK_NOPE + QK_ROPE) ** -0.5

    def kern(*refs):
        it = iter(refs)
        x_ref, mod_ref, g1n_ref, g2n_ref, win_ref, qg_ref, kvg_ref, wuq_ref, wukv_ref = (
            next(it) for _ in range(9))
        cw_ref, sw_ref, sb_ref, wout_ref, wr_ref = (next(it) for _ in range(5))
        if rope:
            cos_ref, sin_ref = next(it), next(it)
        if n_ctx:
            cckv_ref, ckr_ref = next(it), next(it)
        for _ in range(n_alias):
            next(it)
        xmid_ref, h2_ref, aff_ref = next(it), next(it), next(it)
        if emit_ctx:
            ckv_out, kpe_out = next(it), next(it)
        qf_s, kf_s, v_s, mix_s, gb_s, z_s = (next(it) for _ in range(6))

        mod = mod_ref[...]
        sh1, sc1, g1 = mod[:, 0:D], mod[:, D:2 * D], mod[:, 2 * D:3 * D]
        sh2, sc2, g2 = mod[:, 3 * D:4 * D], mod[:, 4 * D:5 * D], mod[:, 5 * D:6 * D]

        def put_kv(row0, rows, ckv_n, kpe_pad):
            kv = _dot(ckv_n.astype(BF16), wukv_ref[...])
            kpb = kpe_pad.astype(BF16)
            for h in range(HEADS):
                c0 = h * (QK_NOPE + V_HEAD)
                kf_s[h, pl.ds(row0, rows), 0:QK_NOPE] = kv[:, c0:c0 + QK_NOPE].astype(BF16)
                kf_s[h, pl.ds(row0, rows), QK_NOPE:2 * QK_NOPE] = kpb
                v_s[h, pl.ds(row0, rows), :] = kv[:, c0 + QK_NOPE:c0 + QK_NOPE + V_HEAD].astype(BF16)

        if n_ctx:
            kr = ckr_ref[...]
            put_kv(0, n_ctx, cckv_ref[...], jnp.concatenate([kr, jnp.zeros_like(kr)], axis=1))

        grp_shift = (CMLP_W // CMLP_G).bit_length() - 1
        lane_grp = lax.shift_right_logical(lax.broadcasted_iota(I32, (CHUNK, CMLP_W), 1), grp_shift)

        def phase1(i, carry):
            r0 = pl.multiple_of(i * TB, TB)
            x = x_ref[pl.ds(r0, TB), :]
            hb = (_rms(x, g1n_ref[...]) * (1.0 + sc1) + sh1).astype(BF16)
            pa = _dot(hb, win_ref[:, C_Q:C_CONV])
            cq, ckv = pa[:, C_Q:C_KV], pa[:, C_KV:C_KPE]
            kp, kpr = pa[:, C_KPE:C_KPR], pa[:, C_KPR:C_CONV]
            if rope:
                cos, sin = cos_ref[pl.ds(r0, TB), :], sin_ref[pl.ds(r0, TB), :]
            qa = _dot(_rms(cq, qg_ref[...]).astype(BF16), wuq_ref[...])
            for h in range(HEADS):
                qn = qa[:, h * 128:(h + 1) * 128]
                qp = qa[:, 512 + h * 128:512 + (h + 1) * 128]
                if rope:
                    qp = qp * cos + qa[:, 1024 + h * 128:1024 + (h + 1) * 128] * sin
                qf_s[h, pl.ds(r0, TB), 0:128] = qn.astype(BF16)
                qf_s[h, pl.ds(r0, TB), 128:256] = qp.astype(BF16)
            ckv_n = _rms(ckv, kvg_ref[...])
            if emit_ctx:
                ckv_out[pl.ds(r0, TB), :] = ckv_n
                kpe_out[pl.ds(r0, TB), :] = kp[:, 0:QK_ROPE]
            kpe = kp * cos + kpr * sin if rope else kp
            put_kv(pl.multiple_of(n_ctx + r0, TB), TB, ckv_n, kpe)
            cv = _dot(hb, win_ref[:, C_CONV:C_CMLP])
            gb_s[pl.ds(r0, TB), :] = cv[:, 0:CONV_W]
            z_s[pl.ds(r0, TB), :] = cv[:, CONV_W:2 * CONV_W] * cv[:, 2 * CONV_W:3 * CONV_W]
            cm = _dot(hb, win_ref[:, C_CMLP:C_END])
            u, vvb = cm[:, 0:CMLP_W], cm[:, CMLP_W:2 * CMLP_W].astype(BF16)
            for c in range(TB // CHUNK):
                r = _dot(sw_ref[...], vvb[c * CHUNK:(c + 1) * CHUNK, :])
                mixed = jnp.where(
                    lane_grp == 0, r[0:CHUNK],
                    jnp.where(lane_grp == 1, r[CHUNK:2 * CHUNK],
                              jnp.where(lane_grp == 2, r[2 * CHUNK:3 * CHUNK], r[3 * CHUNK:4 * CHUNK])))
                out = u[c * CHUNK:(c + 1) * CHUNK, :] * (mixed + sb_ref[...])
                mix_s[pl.ds(pl.multiple_of(r0 + c * CHUNK, CHUNK), CHUNK), 768:1024] = out.astype(BF16)
            return carry

        lax.fori_loop(0, nblk, phase1, 0)

        z = z_s[...]
        row = lax.broadcasted_iota(I32, (n, CONV_W), 0)
        zm = jnp.where(row == 0, 0.0, pltpu.roll(z, 1, 0))
        zp = jnp.where(row == n - 1, 0.0, pltpu.roll(z, n - 1, 0))
        cw = cw_ref[...]
        conv = zm * cw[0:1, :] + z * cw[1:2, :] + zp * cw[2:3, :]
        mix_s[:, 512:768] = (gb_s[...] * conv).astype(BF16)

        def attn(i, carry):
            r0 = pl.multiple_of(i * TQ, TQ)
            for h in range(HEADS):
                s = lax.dot_general(qf_s[h, pl.ds(r0, TQ), :], kf_s[h], NT,
                                    preferred_element_type=F32) * scale
                p = jnp.exp(s - jnp.max(s, axis=-1, keepdims=True))
                o = _dot(p.astype(BF16), v_s[h]) / jnp.sum(p, axis=-1, keepdims=True)
                mix_s[pl.ds(r0, TQ), h * 128:(h + 1) * 128] = o.astype(BF16)
            return carry

        lax.fori_loop(0, n // TQ, attn, 0)

        wr = wr_ref[...]
        wr_hi = wr.astype(BF16)
        wr_lo = (wr - wr_hi.astype(F32)).astype(BF16)

        def phase2(i, carry):
            r0 = pl.multiple_of(i * TB, TB)
            mo = _dot(mix_s[pl.ds(r0, TB), :], wout_ref[...])
            xm = x_ref[pl.ds(r0, TB), :] + g1 * mo
            xmid_ref[pl.ds(r0, TB), :] = xm
            h2 = _rms(xm, g2n_ref[...]) * (1.0 + sc2) + sh2
            h2_hi = h2.astype(BF16)
            h2_ref[pl.ds(r0, TB), :] = h2_hi
            h2_lo = (h2 - h2_hi.astype(F32)).astype(BF16)
            lg = (lax.dot_general(wr_hi, h2_hi, NT, preferred_element_type=F32)
                  + lax.dot_general(wr_lo, h2_hi, NT, preferred_element_type=F32)
                  + lax.dot_general(wr_hi, h2_lo, NT, preferred_element_type=F32))
            e = jnp.exp(lg - jnp.max(lg, axis=0, keepdims=True))
            aff_ref[:, pl.ds(r0, TB)] = e / jnp.sum(e, axis=0, keepdims=True)
            return carry

        lax.fori_loop(0, nblk, phase2, 0)

    return kern


def _const_spec(shape):
    nd = len(shape)
    return pl.BlockSpec(shape, lambda b: (0,) * nd, pipeline_mode=pl.Buffered(1))


def _layer_spec(l, shape):
    nd = len(shape)
    return pl.BlockSpec((None,) + shape, lambda b: (l,) + (0,) * nd, pipeline_mode=pl.Buffered(1))


def _mixer_call(l, x, x_row0, mods, wts, rope_tabs, caches, aliased, *, n, n_req, first_row, mod_row):
    rope = rope_tabs is not None
    n_ctx = PAST_LEN if caches is not None else 0
    emit_ctx = caches is None
    lk = n_ctx + n
    kern = _make_mixer_kernel(n, n_ctx, rope, emit_ctx, len(aliased))

    in_specs = [
        pl.BlockSpec((None, n, D), lambda b: (x_row0 + b, 0, 0)),
        pl.BlockSpec((None, None, 1, 6 * D), lambda b: (l, mod_row(b), 0, 0)),
        _layer_spec(l, (1, D)), _layer_spec(l, (1, D)),
        _layer_spec(l, (D, C_END)),
        _layer_spec(l, (1, Q_RANK)), _layer_spec(l, (1, KV_RANK)),
        _layer_spec(l, (Q_RANK, 3 * HEADS * 128)),
        _layer_spec(l, (KV_RANK, HEADS * (QK_NOPE + V_HEAD))),
        _layer_spec(l, (3, CONV_W)),
        _layer_spec(l, (CMLP_G * CHUNK, CHUNK)),
        _layer_spec(l, (CHUNK, CMLP_W)),
        _layer_spec(l, (D, D)),
        _layer_spec(l, (N_EXP, D)),
    ]
    args = [x, mods, wts["norm1_g"], wts["norm2_g"], wts["w_in"], wts["q_norm_g"], wts["kv_norm_g"],
            wts["w_uq"], wts["w_ukv"], wts["conv_w"], wts["spatial_w"], wts["spatial_b"],
            wts["w_out"], wts["w_router_t"]]
    if rope:
        in_specs += [_const_spec((n, 128)), _const_spec((n, 128))]
        args += list(rope_tabs)
    if n_ctx:
        in_specs += [pl.BlockSpec((None, None, PAST_LEN, KV_RANK), lambda b: (b, l, 0, 0)),
                     pl.BlockSpec((None, None, PAST_LEN, QK_ROPE), lambda b: (b, l, 0, 0))]
        args += list(caches)
    in_specs += [pl.BlockSpec(memory_space=pl.ANY)] * len(aliased)
    args += list(aliased)

    rows_total = N_ROWS * ROW // n
    out_shape = [jax.ShapeDtypeStruct((rows_total, n, D), F32),
                 jax.ShapeDtypeStruct((rows_total, n, D), BF16),
                 jax.ShapeDtypeStruct((n_req * n // ROW, N_EXP, ROW), F32)]
    per_row = ROW // n
    out_specs = [pl.BlockSpec((None, n, D), lambda b: (first_row + b, 0, 0)),
                 pl.BlockSpec((None, n, D), lambda b: (first_row + b, 0, 0)),
                 pl.BlockSpec((None, N_EXP, n), lambda b: (b // per_row, 0, b % per_row))]
    if emit_ctx:
        out_shape += [jax.ShapeDtypeStruct((n_req, n, KV_RANK), F32),
                      jax.ShapeDtypeStruct((n_req, n, QK_ROPE), F32)]
        out_specs += [pl.BlockSpec((None, n, KV_RANK), lambda b: (b, 0, 0)),
                      pl.BlockSpec((None, n, QK_ROPE), lambda b: (b, 0, 0))]
    n_in = len(args)
    io_alias = {n_in - len(aliased) + k: k for k in range(len(aliased))}

    return pl.pallas_call(
        kern,
        out_shape=out_shape,
        grid=(n_req,),
        in_specs=in_specs,
        out_specs=out_specs,
        scratch_shapes=[
            pltpu.VMEM((HEADS, n, 256), BF16),
            pltpu.VMEM((HEADS, lk, 256), BF16),
            pltpu.VMEM((HEADS, lk, V_HEAD), BF16),
            pltpu.VMEM((n, D), BF16),
            pltpu.VMEM((n, CONV_W), F32),
            pltpu.VMEM((n, CONV_W), F32),
        ],
        input_output_aliases=io_alias,
        compiler_params=pltpu.CompilerParams(
            dimension_semantics=("arbitrary",), vmem_limit_bytes=VMEM_LIMIT),
        name="mixer_lat" if rope else "mixer_ctx",
    )(*args)


def _select_slots(aff, cap, tri):
    key = pltpu.bitcast(aff, I32)
    capf = float(cap)

    def bit_step(i, t):
        cand = t | jnp.left_shift(jnp.int32(1), 30 - i)
        cnt = jnp.sum(jnp.where(key >= cand, 1.0, 0.0), axis=1, keepdims=True)
        return jnp.where(cnt >= capf, cand, t)

    thr = lax.fori_loop(0, 31, bit_step, jnp.zeros((aff.shape[0], 1), I32))
    gt = key > thr
    eq = key == thr
    n_gt = jnp.sum(jnp.where(gt, 1.0, 0.0), axis=1, keepdims=True)
    eq_rank = _dot(jnp.where(eq, 1.0, 0.0).astype(BF16), tri)
    sel = jnp.where(gt, 1.0, jnp.where(eq, jnp.where(eq_rank < capf - n_gt, 1.0, 0.0), 0.0))
    pos = _dot(sel.astype(BF16), tri)
    return jnp.where(sel > 0.5, pos.astype(I32), -1)


def _route_kernel(ap_ref, as_ref, slot_ref, gate_ref, tri_s):
    r = lax.broadcasted_iota(I32, (ROW, ROW), 0)
    c = lax.broadcasted_iota(I32, (ROW, ROW), 1)
    tri_s[...] = jnp.where(r < c, 1.0, 0.0).astype(BF16)
    np_ = P_ROWS * N_EXP
    for s in range(P_PER_ROW):
        a = ap_ref[:, s * SEQ:(s + 1) * SEQ]
        sl = _select_slots(a, CAP_P, tri_s[0:SEQ, 0:SEQ])
        slot_ref[0:np_, s * SEQ:(s + 1) * SEQ] = jnp.where(sl >= 0, sl + s * CAP_P, -1)
    gate_ref[0:np_, :] = ap_ref[...]
    a = as_ref[...]
    slot_ref[np_:, :] = _select_slots(a, CAP, tri_s[...])
    gate_ref[np_:, :] = a


def _route_call(aff_p, aff_s):
    rows = N_ROWS * N_EXP
    return pl.pallas_call(
        _route_kernel,
        out_shape=[jax.ShapeDtypeStruct((rows, ROW), I32), jax.ShapeDtypeStruct((rows, ROW), F32)],
        scratch_shapes=[pltpu.VMEM((ROW, ROW), BF16)],
        compiler_params=pltpu.CompilerParams(vmem_limit_bytes=VMEM_LIMIT),
        name="route",
    )(aff_p.reshape(P_ROWS * N_EXP, ROW), aff_s.reshape(DEC_BATCH * N_EXP, ROW))


GATHER_NC = 256


def _gather_kernel(slot_ref, h2_ref, xs_ref, hot_s):
    j = lax.broadcasted_iota(I32, (CAP, ROW), 0)
    for e in range(N_EXP):
        hot_s[e * CAP:(e + 1) * CAP, :] = jnp.where(slot_ref[e] == j, 1.0, 0.0).astype(BF16)
    for c in range(D // GATHER_NC):
        xs = _dot(hot_s[...], h2_ref[:, c * GATHER_NC:(c + 1) * GATHER_NC]).astype(BF16)
        for e in range(N_EXP):
            xs_ref[e, :, c * GATHER_NC:(c + 1) * GATHER_NC] = xs[e * CAP:(e + 1) * CAP, :]


def _gather_call(slot, h2):
    return pl.pallas_call(
        _gather_kernel,
        out_shape=jax.ShapeDtypeStruct((N_EXP, N_ROWS, CAP, D), BF16),
        grid=(N_ROWS,),
        in_specs=[
            pl.BlockSpec((None, N_EXP, 1, ROW), lambda r: (r, 0, 0, 0)),
            pl.BlockSpec((None, ROW, D), lambda r: (r, 0, 0)),
        ],
        out_specs=pl.BlockSpec((N_EXP, None, CAP, D), lambda r: (0, r, 0, 0)),
        scratch_shapes=[pltpu.VMEM((N_EXP * CAP, ROW), BF16)],
        compiler_params=pltpu.CompilerParams(
            dimension_semantics=("arbitrary",), vmem_limit_bytes=VMEM_LIMIT),
        name="gather",
    )(slot.reshape(N_ROWS, N_EXP, 1, ROW), h2)


EXP_ROWS = 4


def _expert_kernel(slot_ref, gate_ref, xs_ref, wg_ref, wu_ref, wd_ref, ye_ref, wg_s, wu_s, wd_s):
    wg_s[...] = wg_ref[...].astype(BF16)
    wu_s[...] = wu_ref[...].astype(BF16)
    wd_s[...] = wd_ref[...].astype(BF16)
    j = lax.broadcasted_iota(I32, (CAP, ROW), 0)

    def group(g, carry):
        r0 = g * EXP_ROWS
        xs = xs_ref[pl.ds(r0, EXP_ROWS)].reshape(EXP_ROWS * CAP, D)
        hid = _silu(_dot(xs, wg_s[...])) * _dot(xs, wu_s[...])
        ye = _dot(hid.astype(BF16), wd_s[...])
        for k in range(EXP_ROWS):
            hit = slot_ref[r0 + k] == j
            gate = jnp.sum(jnp.where(hit, gate_ref[r0 + k], 0.0), axis=1, keepdims=True)
            ye_ref[r0 + k] = (ye[k * CAP:(k + 1) * CAP, :] * gate).astype(BF16)
        return carry

    lax.fori_loop(0, N_ROWS // EXP_ROWS, group, 0)


def _expert_call(l, slot, gate, xs, w_gate, w_up, w_down):
    return pl.pallas_call(
        _expert_kernel,
        out_shape=jax.ShapeDtypeStruct((N_EXP, N_ROWS, CAP, D), BF16),
        grid=(N_EXP,),
        in_specs=[
            pl.BlockSpec((N_ROWS, None, 1, ROW), lambda e: (0, e, 0, 0)),
            pl.BlockSpec((N_ROWS, None, 1, ROW), lambda e: (0, e, 0, 0)),
            pl.BlockSpec((None, N_ROWS, CAP, D), lambda e: (e, 0, 0, 0)),
            pl.BlockSpec((None, None, D, FF), lambda e: (l, e, 0, 0)),
            pl.BlockSpec((None, None, D, FF), lambda e: (l, e, 0, 0)),
            pl.BlockSpec((None, None, FF, D), lambda e: (l, e, 0, 0)),
        ],
        out_specs=pl.BlockSpec((None, N_ROWS, CAP, D), lambda e: (e, 0, 0, 0)),
        scratch_shapes=[pltpu.VMEM((D, FF), BF16), pltpu.VMEM((D, FF), BF16), pltpu.VMEM((FF, D), BF16)],
        compiler_params=pltpu.CompilerParams(
            dimension_semantics=("arbitrary",), vmem_limit_bytes=VMEM_LIMIT),
        name="experts",
    )(slot.reshape(N_ROWS, N_EXP, 1, ROW), gate.reshape(N_ROWS, N_EXP, 1, ROW), xs, w_gate, w_up, w_down)


def _make_combine_kernel(final):
    tc = 256

    def kern(*refs):
        if final:
            slot_ref, ye_ref, xmid_ref, g2_ref, fg_ref, o_ref = refs
        else:
            slot_ref, ye_ref, xmid_ref, g2_ref, o_ref = refs
        ye = ye_ref[...].reshape(N_EXP * CAP, D)
        g2 = g2_ref[...]
        j = lax.broadcasted_iota(I32, (CAP, tc), 0)
        for t in range(ROW // tc):
            onehot = jnp.concatenate(
                [jnp.where(slot_ref[e, :, t * tc:(t + 1) * tc] == j, 1.0, 0.0).astype(BF16)
                 for e in range(N_EXP)], axis=0)
            moe = lax.dot_general(onehot, ye, TN, preferred_element_type=F32)
            xn = xmid_ref[t * tc:(t + 1) * tc, :] + g2 * moe
            if final:
                xn = _rms(xn, fg_ref[...])
            o_ref[t * tc:(t + 1) * tc, :] = xn

    return kern


def _combine_call(l, slot, ye, xmid, mods, final_g, row0, n_rows, mod_row):
    final = final_g is not None
    in_specs = [
        pl.BlockSpec((None, N_EXP, 1, ROW), lambda r: (row0 + r, 0, 0, 0)),
        pl.BlockSpec((N_EXP, None, CAP, D), lambda r: (0, row0 + r, 0, 0)),
        pl.BlockSpec((None, ROW, D), lambda r: (row0 + r, 0, 0)),
        pl.BlockSpec((None, None, 1, D), lambda r: (l, mod_row(r), 0, 5)),
    ]
    args = [slot.reshape(N_ROWS, N_EXP, 1, ROW), ye, xmid, mods]
    if final:
        in_specs.append(pl.BlockSpec((1, D), lambda r: (0, 0)))
        args.append(final_g)
    return pl.pallas_call(
        _make_combine_kernel(final),
        out_shape=jax.ShapeDtypeStruct((n_rows, ROW, D), F32),
        grid=(n_rows,),
        in_specs=in_specs,
        out_specs=pl.BlockSpec((None, ROW, D), lambda r: (r, 0, 0)),
        compiler_params=pltpu.CompilerParams(
            dimension_semantics=("arbitrary",), vmem_limit_bytes=VMEM_LIMIT),
        name="combine",
    )(*args)


def _rope_rot_cols(w):
    q = QK_ROPE // 4
    return jnp.concatenate([-w[..., q:2 * q], w[..., 0:q], -w[..., 3 * q:4 * q], w[..., 2 * q:3 * q]], axis=-1)


def _pad_lanes(w, width=128):
    return jnp.pad(w, [(0, 0)] * (w.ndim - 1) + [(0, width - w.shape[-1])])


def _prep_weights(norm1_g, norm2_g, w_in, q_norm_g, kv_norm_g, w_uq, w_ukv, conv_w, spatial_w,
                  spatial_b, w_out, w_router):
    o_kv, o_pe, o_conv = Q_RANK, Q_RANK + KV_RANK, Q_RANK + KV_RANK + QK_ROPE
    w_kpe = w_in[..., o_pe:o_conv]
    w_in_p = jnp.concatenate(
        [w_in[..., :o_pe], _pad_lanes(w_kpe), _pad_lanes(_rope_rot_cols(w_kpe)), w_in[..., o_conv:]],
        axis=-1).astype(BF16)
    uq = w_uq.reshape(DEPTH, Q_RANK, HEADS, QK_NOPE + QK_ROPE)
    uq_pe = uq[..., QK_NOPE:]
    w_uq_p = jnp.concatenate(
        [uq[..., :QK_NOPE].reshape(DEPTH, Q_RANK, HEADS * QK_NOPE),
         _pad_lanes(uq_pe).reshape(DEPTH, Q_RANK, HEADS * 128),
         _pad_lanes(_rope_rot_cols(uq_pe)).reshape(DEPTH, Q_RANK, HEADS * 128)], axis=-1).astype(BF16)
    sb = jnp.repeat(jnp.swapaxes(spatial_b, 1, 2), CMLP_W // CMLP_G, axis=-1)
    return dict(
        norm1_g=norm1_g.reshape(DEPTH, 1, D), norm2_g=norm2_g.reshape(DEPTH, 1, D),
        w_in=w_in_p, q_norm_g=q_norm_g.reshape(DEPTH, 1, Q_RANK),
        kv_norm_g=kv_norm_g.reshape(DEPTH, 1, KV_RANK), w_uq=w_uq_p, w_ukv=w_ukv.astype(BF16),
        conv_w=conv_w, spatial_w=spatial_w.reshape(DEPTH, CMLP_G * CHUNK, CHUNK).astype(BF16),
        spatial_b=sb, w_out=w_out.astype(BF16), w_router_t=jnp.swapaxes(w_router, 1, 2))


def _rope_tables(n):
    rows = n // GRID_W
    row = jnp.repeat(jnp.arange(rows), GRID_W).astype(F32)
    col = jnp.tile(jnp.arange(GRID_W), rows).astype(F32)
    n_freq = QK_ROPE // 4
    inv = ROPE_BASE ** (-jnp.arange(n_freq, dtype=F32) / n_freq)
    ang_r, ang_c = row[:, None] * inv, col[:, None] * inv
    ang = jnp.concatenate([ang_r, ang_r, ang_c, ang_c], axis=-1)
    return _pad_lanes(jnp.cos(ang)), _pad_lanes(jnp.sin(ang))


def kernel(x_prompt, x_sample, cache_ckv, cache_krope, c, c_ctx, w_ada, b_ada, norm1_g, norm2_g, w_in,
           q_norm_g, kv_norm_g, w_uq, w_ukv, conv_w, spatial_w, spatial_b, w_out, w_router, w_gate,
           w_up, w_down, final_norm_g):
    wts = _prep_weights(norm1_g, norm2_g, w_in, q_norm_g, kv_norm_g, w_uq, w_ukv, conv_w, spatial_w,
                        spatial_b, w_out, w_router)
    rope_tabs = _rope_tables(DEC_SEQ)
    cond = jnp.concatenate(
        [c, c_ctx[None, :], jnp.zeros((MOD_ROWS - DEC_BATCH - 1, D), F32)], axis=0)
    mods = _ada_call(cond, w_ada, b_ada).reshape(DEPTH, MOD_ROWS, 1, 6 * D)
    final_g = final_norm_g.reshape(1, D)

    ctx_mod = lambda b: CTX_MOD_ROW
    lat_mod = lambda b: b
    row_mod = lambda r: jnp.where(r < P_ROWS, CTX_MOD_ROW, r - P_ROWS)

    xp, xp_row0 = x_prompt, 0
    xs, xs_row0 = x_sample, 0
    ckv_list, kpe_list = [], []
    for l in range(DEPTH):
        xmid, h2, aff_p, ckv, kpe = _mixer_call(
            l, xp, xp_row0, mods, wts, None, None, (), n=SEQ, n_req=BATCH, first_row=0, mod_row=ctx_mod)
        ckv_list.append(ckv)
        kpe_list.append(kpe)
        xmid, h2, aff_s = _mixer_call(
            l, xs, xs_row0, mods, wts, rope_tabs, (cache_ckv, cache_krope),
            (xmid.reshape(N_ROWS, ROW, D), h2.reshape(N_ROWS, ROW, D)),
            n=DEC_SEQ, n_req=DEC_BATCH, first_row=P_ROWS, mod_row=lat_mod)
        slot, gate = _route_call(aff_p, aff_s)
        ye = _expert_call(l, slot, gate, _gather_call(slot, h2), w_gate, w_up, w_down)
        if l < DEPTH - 1:
            x = _combine_call(l, slot, ye, xmid, mods, None, 0, N_ROWS, row_mod)
            xp, xp_row0 = x.reshape(N_ROWS * ROW // SEQ, SEQ, D), 0
            xs, xs_row0 = x, P_ROWS
        else:
            y_prompt = _combine_call(l, slot, ye, xmid, mods, final_g, 0, P_ROWS, ctx_mod)
            y_sample = _combine_call(l, slot, ye, xmid, mods, final_g, P_ROWS, DEC_BATCH,
                                     lambda r: r)
    return (y_prompt.reshape(BATCH, SEQ, D), y_sample,
            jnp.stack(ckv_list, axis=1), jnp.stack(kpe_list, axis=1))
```

```python
import jax
import jax.numpy as jnp
from jax import lax
from jax.experimental import pallas as pl
from jax.experimental.pallas import tpu as pltpu

F32 = jnp.float32
BF16 = jnp.bfloat16
I32 = jnp.int32

D = 1024
BATCH = 16
SEQ = 256
DEPTH = 4
DEC_BATCH = 8
DEC_SEQ = 1024
PAST_LEN = 512
GRID_W = 64
HEADS = 4
QK_NOPE = 128
QK_ROPE = 64
V_HEAD = 128
Q_RANK = 256
KV_RANK = 128
CONV_W = 256
CMLP_W = 256
CMLP_G = 4
CHUNK = 128
N_EXP = 16
EC_FACTOR = 2
FF = D // 2
ROPE_BASE = 10000.0
EPS = 1e-6

LANES = 128
ROW = 1024
N_ROWS = (BATCH * SEQ + DEC_BATCH * DEC_SEQ) // ROW
P_ROWS = BATCH * SEQ // ROW
P_PER_ROW = ROW // SEQ
CAP = EC_FACTOR * ROW // N_EXP
CAP_P = EC_FACTOR * SEQ // N_EXP
CTX_MOD_ROW = DEC_BATCH
MOD_ROWS = 16
QK_PAD = QK_NOPE + LANES

C_Q, C_KV, C_KPE, C_KPR, C_CONV, C_CMLP, C_END = 0, 256, 384, 512, 640, 1408, 1920
U_NOPE, U_PE, U_ROT, U_END = 0, HEADS * QK_NOPE, HEADS * (QK_NOPE + LANES), HEADS * (QK_NOPE + 2 * LANES)
TB = 256
TQ = 256
VMEM_LIMIT = 58 * 1024 * 1024

NT = (((1,), (1,)), ((), ()))
TN = (((0,), (0,)), ((), ()))


def _rms(x, g):
    return x * lax.rsqrt(jnp.mean(x * x, axis=-1, keepdims=True) + EPS) * g


def _silu(x):
    return x / (1.0 + jnp.exp(-x))


def _dot(a, b):
    return jnp.dot(a, b, preferred_element_type=F32)


def _dot_nt(a, b):
    return lax.dot_general(a, b, NT, preferred_element_type=F32)


def _ada_kernel(c_ref, w_ref, b_ref, o_ref):
    s = _silu(c_ref[...]).astype(BF16)
    o_ref[...] = _dot(s, w_ref[...].astype(BF16)) + b_ref[...]


def _ada_call(cond, w_ada, b_ada):
    nc = 6
    return pl.pallas_call(
        _ada_kernel,
        out_shape=jax.ShapeDtypeStruct((DEPTH, MOD_ROWS, 6 * D), F32),
        grid=(DEPTH, nc),
        in_specs=[
            pl.BlockSpec((MOD_ROWS, D), lambda l, j: (0, 0)),
            pl.BlockSpec((None, D, D), lambda l, j: (l, 0, j)),
            pl.BlockSpec((None, 1, D), lambda l, j: (l, 0, j)),
        ],
        out_specs=pl.BlockSpec((None, MOD_ROWS, D), lambda l, j: (l, 0, j)),
        compiler_params=pltpu.CompilerParams(
            dimension_semantics=("arbitrary", "arbitrary"), vmem_limit_bytes=VMEM_LIMIT),
        name="ada_mod",
    )(cond, w_ada, b_ada.reshape(DEPTH, 1, 6 * D))


def _mixer_kernel(xp_ref, xs_ref, mod_ref, g1n_ref, g2n_ref, win_ref, qg_ref, kvg_ref, wuq_ref, wukv_ref,
                  cw_ref, sw_ref, sb_ref, wout_ref, wr_ref, cos_ref, sin_ref, cckv_ref, ckr_ref,
                  xmid_ref, h2_ref, aff_ref, ckv_out, kpe_out,
                  qf_s, kf_s, v_s, mix_s, gb_s, z_s):
    scale = float(QK_NOPE + QK_ROPE) ** -0.5
    nblk = ROW // TB

    def run(x_ref, ctx):
        n_past = 0 if ctx else PAST_LEN
        seg = SEQ if ctx else ROW
        mod = mod_ref[...]
        sh1, sc1, g1 = mod[:, 0:D], mod[:, D:2 * D], mod[:, 2 * D:3 * D]
        sh2, sc2, g2 = mod[:, 3 * D:4 * D], mod[:, 4 * D:5 * D], mod[:, 5 * D:6 * D]

        def put_kv(row0, rows, ckv_n, kpe_pad):
            kv = _dot(ckv_n.astype(BF16), wukv_ref[...])
            kpb = kpe_pad.astype(BF16)
            for h in range(HEADS):
                c0 = h * (QK_NOPE + V_HEAD)
                kf_s[h, pl.ds(row0, rows), 0:QK_NOPE] = kv[:, c0:c0 + QK_NOPE].astype(BF16)
                kf_s[h, pl.ds(row0, rows), QK_NOPE:QK_PAD] = kpb
                v_s[h, pl.ds(row0, rows), :] = kv[:, c0 + QK_NOPE:c0 + QK_NOPE + V_HEAD].astype(BF16)

        if not ctx:
            kr = ckr_ref[...]
            put_kv(0, PAST_LEN, cckv_ref[...], jnp.concatenate([kr, jnp.zeros_like(kr)], axis=1))

        grp_shift = (CMLP_W // CMLP_G).bit_length() - 1
        lane_grp = lax.shift_right_logical(lax.broadcasted_iota(I32, (CHUNK, CMLP_W), 1), grp_shift)

        def phase1(i, carry):
            r0 = pl.multiple_of(i * TB, TB)
            x = x_ref[pl.ds(r0, TB), :]
            hb = (_rms(x, g1n_ref[...]) * (1.0 + sc1) + sh1).astype(BF16)
            pa = _dot(hb, win_ref[:, C_Q:C_CONV])
            cq, ckv = pa[:, C_Q:C_KV], pa[:, C_KV:C_KPE]
            kp, kpr = pa[:, C_KPE:C_KPR], pa[:, C_KPR:C_CONV]
            if not ctx:
                cos, sin = cos_ref[pl.ds(r0, TB), :], sin_ref[pl.ds(r0, TB), :]
            qa = _dot(_rms(cq, qg_ref[...]).astype(BF16), wuq_ref[...])
            for h in range(HEADS):
                qn = qa[:, U_NOPE + h * QK_NOPE:U_NOPE + (h + 1) * QK_NOPE]
                qp = qa[:, U_PE + h * LANES:U_PE + (h + 1) * LANES]
                if not ctx:
                    qp = qp * cos + qa[:, U_ROT + h * LANES:U_ROT + (h + 1) * LANES] * sin
                qf_s[h, pl.ds(r0, TB), 0:QK_NOPE] = qn.astype(BF16)
                qf_s[h, pl.ds(r0, TB), QK_NOPE:QK_PAD] = qp.astype(BF16)
            ckv_n = _rms(ckv, kvg_ref[...])
            if ctx:
                ckv_out[pl.ds(r0, TB), :] = ckv_n
                kpe_out[pl.ds(r0, TB), :] = kp[:, 0:QK_ROPE]
            kpe = kp if ctx else kp * cos + kpr * sin
            put_kv(pl.multiple_of(n_past + r0, TB), TB, ckv_n, kpe)
            cv = _dot(hb, win_ref[:, C_CONV:C_CMLP])
            gb_s[pl.ds(r0, TB), :] = cv[:, 0:CONV_W]
            z_s[pl.ds(r0, TB), :] = cv[:, CONV_W:2 * CONV_W] * cv[:, 2 * CONV_W:3 * CONV_W]
            cm = _dot(hb, win_ref[:, C_CMLP:C_END])
            u, vvb = cm[:, 0:CMLP_W], cm[:, CMLP_W:2 * CMLP_W].astype(BF16)
            for c in range(TB // CHUNK):
                r = _dot(sw_ref[...], vvb[c * CHUNK:(c + 1) * CHUNK, :])
                mixed = jnp.where(
                    lane_grp == 0, r[0:CHUNK],
                    jnp.where(lane_grp == 1, r[CHUNK:2 * CHUNK],
                              jnp.where(lane_grp == 2, r[2 * CHUNK:3 * CHUNK], r[3 * CHUNK:4 * CHUNK])))
                out = u[c * CHUNK:(c + 1) * CHUNK, :] * (mixed + sb_ref[...])
                mix_s[pl.ds(pl.multiple_of(r0 + c * CHUNK, CHUNK), CHUNK),
                      HEADS * V_HEAD + CONV_W:D] = out.astype(BF16)
            return carry

        lax.fori_loop(0, nblk, phase1, 0, unroll=True)

        z = z_s[...]
        pos = lax.broadcasted_iota(I32, (ROW, CONV_W), 0) & (seg - 1)
        zm = jnp.where(pos == 0, 0.0, pltpu.roll(z, 1, 0))
        zp = jnp.where(pos == seg - 1, 0.0, pltpu.roll(z, ROW - 1, 0))
        cw = cw_ref[...]
        conv = zm * cw[0:1, :] + z * cw[1:2, :] + zp * cw[2:3, :]
        mix_s[:, HEADS * V_HEAD:HEADS * V_HEAD + CONV_W] = (gb_s[...] * conv).astype(BF16)

        def attn(i, carry):
            r0 = pl.multiple_of(i * TQ, TQ)
            for h in range(HEADS):
                keys = kf_s[h, pl.ds(r0, SEQ), :] if ctx else kf_s[h]
                vals = v_s[h, pl.ds(r0, SEQ), :] if ctx else v_s[h]
                s = _dot_nt(qf_s[h, pl.ds(r0, TQ), :], keys) * scale
                p = jnp.exp(s - jnp.max(s, axis=-1, keepdims=True))
                o = _dot(p.astype(BF16), vals) / jnp.sum(p, axis=-1, keepdims=True)
                mix_s[pl.ds(r0, TQ), h * V_HEAD:(h + 1) * V_HEAD] = o.astype(BF16)
            return carry

        lax.fori_loop(0, ROW // TQ, attn, 0)

        wr = wr_ref[...]
        wr_hi = wr.astype(BF16)
        wr_hl = jnp.concatenate([wr_hi, (wr - wr_hi.astype(F32)).astype(BF16)], axis=0)

        def phase2(i, carry):
            r0 = pl.multiple_of(i * TB, TB)
            mo = _dot(mix_s[pl.ds(r0, TB), :], wout_ref[...])
            xm = x_ref[pl.ds(r0, TB), :] + g1 * mo
            xmid_ref[pl.ds(r0, TB), :] = xm
            h2 = _rms(xm, g2n_ref[...]) * (1.0 + sc2) + sh2
            h2_hi = h2.astype(BF16)
            h2_ref[pl.ds(r0, TB), :] = h2_hi
            h2_lo = (h2 - h2_hi.astype(F32)).astype(BF16)
            la = _dot_nt(wr_hl, h2_hi)
            lg = la[0:N_EXP] + la[N_EXP:2 * N_EXP] + _dot_nt(wr_hi, h2_lo)
            e = jnp.exp(lg - jnp.max(lg, axis=0, keepdims=True))
            aff_ref[:, pl.ds(r0, TB)] = e / jnp.sum(e, axis=0, keepdims=True)
            return carry

        lax.fori_loop(0, nblk, phase2, 0, unroll=True)

    is_ctx = pl.program_id(0) < P_ROWS

    @pl.when(is_ctx)
    def _():
        run(xp_ref, True)

    @pl.when(jnp.logical_not(is_ctx))
    def _():
        run(xs_ref, False)


def _layer_spec(l, shape):
    nd = len(shape)
    return pl.BlockSpec((None,) + shape, lambda r: (l,) + (0,) * nd, pipeline_mode=pl.Buffered(1))


def _mixer_call(l, xp, xs, xs_row0, mods, wts, rope_tabs, caches):
    lk = PAST_LEN + ROW
    lat = lambda r: jnp.maximum(r - P_ROWS, 0)
    ctx = lambda r: jnp.minimum(r, P_ROWS - 1)
    in_specs = [
        pl.BlockSpec((None, ROW, D), lambda r: (ctx(r), 0, 0)),
        pl.BlockSpec((None, ROW, D), lambda r: (xs_row0 + lat(r), 0, 0)),
        pl.BlockSpec((None, None, 1, 6 * D), lambda r: (l, jnp.where(r < P_ROWS, CTX_MOD_ROW, r - P_ROWS), 0, 0)),
        _layer_spec(l, (1, D)), _layer_spec(l, (1, D)),
        _layer_spec(l, (D, C_END)),
        _layer_spec(l, (1, Q_RANK)), _layer_spec(l, (1, KV_RANK)),
        _layer_spec(l, (Q_RANK, U_END)),
        _layer_spec(l, (KV_RANK, HEADS * (QK_NOPE + V_HEAD))),
        _layer_spec(l, (3, CONV_W)),
        _layer_spec(l, (CMLP_G * CHUNK, CHUNK)),
        _layer_spec(l, (CHUNK, CMLP_W)),
        _layer_spec(l, (D, D)),
        _layer_spec(l, (N_EXP, D)),
        pl.BlockSpec((ROW, LANES), lambda r: (0, 0), pipeline_mode=pl.Buffered(1)),
        pl.BlockSpec((ROW, LANES), lambda r: (0, 0), pipeline_mode=pl.Buffered(1)),
        pl.BlockSpec((None, None, PAST_LEN, KV_RANK), lambda r: (lat(r), l, 0, 0)),
        pl.BlockSpec((None, None, PAST_LEN, QK_ROPE), lambda r: (lat(r), l, 0, 0)),
    ]
    args = [xp, xs, mods, wts["norm1_g"], wts["norm2_g"], wts["w_in"], wts["q_norm_g"], wts["kv_norm_g"],
            wts["w_uq"], wts["w_ukv"], wts["conv_w"], wts["spatial_w"], wts["spatial_b"],
            wts["w_out"], wts["w_router_t"], rope_tabs[0], rope_tabs[1], caches[0], caches[1]]
    out_shape = [jax.ShapeDtypeStruct((N_ROWS, ROW, D), F32),
                 jax.ShapeDtypeStruct((N_ROWS, ROW, D), BF16),
                 jax.ShapeDtypeStruct((N_ROWS, N_EXP, ROW), F32),
                 jax.ShapeDtypeStruct((P_ROWS, ROW, KV_RANK), F32),
                 jax.ShapeDtypeStruct((P_ROWS, ROW, QK_ROPE), F32)]
    out_specs = [pl.BlockSpec((None, ROW, D), lambda r: (r, 0, 0)),
                 pl.BlockSpec((None, ROW, D), lambda r: (r, 0, 0)),
                 pl.BlockSpec((None, N_EXP, ROW), lambda r: (r, 0, 0)),
                 pl.BlockSpec((None, ROW, KV_RANK), lambda r: (ctx(r), 0, 0)),
                 pl.BlockSpec((None, ROW, QK_ROPE), lambda r: (ctx(r), 0, 0))]
    return pl.pallas_call(
        _mixer_kernel,
        out_shape=out_shape,
        grid=(N_ROWS,),
        in_specs=in_specs,
        out_specs=out_specs,
        scratch_shapes=[
            pltpu.VMEM((HEADS, ROW, QK_PAD), BF16),
            pltpu.VMEM((HEADS, lk, QK_PAD), BF16),
            pltpu.VMEM((HEADS, lk, V_HEAD), BF16),
            pltpu.VMEM((ROW, D), BF16),
            pltpu.VMEM((ROW, CONV_W), F32),
            pltpu.VMEM((ROW, CONV_W), F32),
        ],
        compiler_params=pltpu.CompilerParams(
            dimension_semantics=("arbitrary",), vmem_limit_bytes=VMEM_LIMIT),
        name="mixer",
    )(*args)


def _select_slots(aff, cap, tri):
    key = pltpu.bitcast(aff, I32)
    capf = float(cap)

    def bit_step(i, t):
        cand = t | jnp.left_shift(jnp.int32(1), 30 - i)
        cnt = jnp.sum(jnp.where(key >= cand, 1.0, 0.0), axis=1, keepdims=True)
        return jnp.where(cnt >= capf, cand, t)

    thr = lax.fori_loop(0, 31, bit_step, jnp.zeros((aff.shape[0], 1), I32))
    gt = key > thr
    eq = key == thr
    n_gt = jnp.sum(jnp.where(gt, 1.0, 0.0), axis=1, keepdims=True)
    eq_rank = _dot(jnp.where(eq, 1.0, 0.0).astype(BF16), tri)
    sel = jnp.where(gt, 1.0, jnp.where(eq, jnp.where(eq_rank < capf - n_gt, 1.0, 0.0), 0.0))
    pos = _dot(sel.astype(BF16), tri)
    return jnp.where(sel > 0.5, pos.astype(I32), -1)


def _route_kernel(aff_ref, slot_ref, tri_s):
    r = lax.broadcasted_iota(I32, (ROW, ROW), 0)
    c = lax.broadcasted_iota(I32, (ROW, ROW), 1)
    tri_s[...] = jnp.where(r < c, 1.0, 0.0).astype(BF16)
    np_ = P_ROWS * N_EXP
    for s in range(P_PER_ROW):
        a = aff_ref[0:np_, s * SEQ:(s + 1) * SEQ]
        sl = _select_slots(a, CAP_P, tri_s[0:SEQ, 0:SEQ])
        slot_ref[0:np_, s * SEQ:(s + 1) * SEQ] = jnp.where(sl >= 0, sl + s * CAP_P, -1)
    slot_ref[np_:, :] = _select_slots(aff_ref[np_:, :], CAP, tri_s[...])


def _route_call(aff):
    rows = N_ROWS * N_EXP
    return pl.pallas_call(
        _route_kernel,
        out_shape=jax.ShapeDtypeStruct((rows, ROW), I32),
        scratch_shapes=[pltpu.VMEM((ROW, ROW), BF16)],
        compiler_params=pltpu.CompilerParams(vmem_limit_bytes=VMEM_LIMIT),
        name="route",
    )(aff)


GATHER_NC = 256


def _gather_kernel(slot_ref, h2_ref, xs_ref, hot_s):
    j = lax.broadcasted_iota(I32, (CAP, ROW), 0)
    for e in range(N_EXP):
        hot_s[e * CAP:(e + 1) * CAP, :] = jnp.where(slot_ref[e:e + 1, :] == j, 1.0, 0.0).astype(BF16)
    for c in range(D // GATHER_NC):
        xs = _dot(hot_s[...], h2_ref[:, c * GATHER_NC:(c + 1) * GATHER_NC]).astype(BF16)
        for e in range(N_EXP):
            xs_ref[e, :, c * GATHER_NC:(c + 1) * GATHER_NC] = xs[e * CAP:(e + 1) * CAP, :]


def _gather_call(slot, h2):
    return pl.pallas_call(
        _gather_kernel,
        out_shape=jax.ShapeDtypeStruct((N_EXP, N_ROWS, CAP, D), BF16),
        grid=(N_ROWS,),
        in_specs=[
            pl.BlockSpec((N_EXP, ROW), lambda r: (r, 0)),
            pl.BlockSpec((None, ROW, D), lambda r: (r, 0, 0)),
        ],
        out_specs=pl.BlockSpec((N_EXP, None, CAP, D), lambda r: (0, r, 0, 0)),
        scratch_shapes=[pltpu.VMEM((N_EXP * CAP, ROW), BF16)],
        compiler_params=pltpu.CompilerParams(
            dimension_semantics=("arbitrary",), vmem_limit_bytes=VMEM_LIMIT),
        name="gather",
    )(slot, h2)


EXP_ROWS = 4


def _expert_kernel(slot_ref, aff_ref, xs_ref, wg_ref, wu_ref, wd_ref, ye_ref, wg_s, wu_s, wd_s):
    e = pl.program_id(0)
    wg_s[...] = wg_ref[...].astype(BF16)
    wu_s[...] = wu_ref[...].astype(BF16)
    wd_s[...] = wd_ref[...].astype(BF16)
    j = lax.broadcasted_iota(I32, (CAP, ROW), 0)

    def group(g, carry):
        r0 = g * EXP_ROWS
        xs = xs_ref[pl.ds(r0, EXP_ROWS)].reshape(EXP_ROWS * CAP, D)
        hid = _silu(_dot(xs, wg_s[...])) * _dot(xs, wu_s[...])
        ye = _dot(hid.astype(BF16), wd_s[...])
        for k in range(EXP_ROWS):
            row = (r0 + k) * N_EXP + e
            hit = slot_ref[pl.ds(row, 1), :] == j
            gate = jnp.sum(jnp.where(hit, aff_ref[pl.ds(row, 1), :], 0.0), axis=1, keepdims=True)
            ye_ref[r0 + k] = (ye[k * CAP:(k + 1) * CAP, :] * gate).astype(BF16)
        return carry

    lax.fori_loop(0, N_ROWS // EXP_ROWS, group, 0)


def _expert_call(l, slot, aff, xs, w_gate, w_up, w_down):
    rows = N_ROWS * N_EXP
    return pl.pallas_call(
        _expert_kernel,
        out_shape=jax.ShapeDtypeStruct((N_EXP, N_ROWS, CAP, D), BF16),
        grid=(N_EXP,),
        in_specs=[
            pl.BlockSpec((rows, ROW), lambda e: (0, 0), pipeline_mode=pl.Buffered(1)),
            pl.BlockSpec((rows, ROW), lambda e: (0, 0), pipeline_mode=pl.Buffered(1)),
            pl.BlockSpec((None, N_ROWS, CAP, D), lambda e: (e, 0, 0, 0)),
            pl.BlockSpec((None, None, D, FF), lambda e: (l, e, 0, 0)),
            pl.BlockSpec((None, None, D, FF), lambda e: (l, e, 0, 0)),
            pl.BlockSpec((None, None, FF, D), lambda e: (l, e, 0, 0)),
        ],
        out_specs=pl.BlockSpec((None, N_ROWS, CAP, D), lambda e: (e, 0, 0, 0)),
        scratch_shapes=[pltpu.VMEM((D, FF), BF16), pltpu.VMEM((D, FF), BF16), pltpu.VMEM((FF, D), BF16)],
        compiler_params=pltpu.CompilerParams(
            dimension_semantics=("arbitrary",), vmem_limit_bytes=VMEM_LIMIT),
        name="experts",
    )(slot, aff, xs, w_gate, w_up, w_down)


COMB_TC = 256


def _make_combine_kernel(final):
    def kern(*refs):
        if final:
            slot_ref, ye_ref, xmid_ref, g2_ref, fg_ref, o_ref = refs
        else:
            slot_ref, ye_ref, xmid_ref, g2_ref, o_ref = refs
        ye = ye_ref[...].reshape(N_EXP * CAP, D)
        g2 = g2_ref[...]
        j = lax.broadcasted_iota(I32, (CAP, COMB_TC), 0)
        for t in range(ROW // COMB_TC):
            cols = slice(t * COMB_TC, (t + 1) * COMB_TC)
            onehot = jnp.concatenate(
                [jnp.where(slot_ref[e:e + 1, cols] == j, 1.0, 0.0).astype(BF16) for e in range(N_EXP)],
                axis=0)
            moe = lax.dot_general(onehot, ye, TN, preferred_element_type=F32)
            xn = xmid_ref[cols, :] + g2 * moe
            if final:
                xn = _rms(xn, fg_ref[...])
            o_ref[cols, :] = xn

    return kern


def _combine_call(l, slot, ye, xmid, mods, final_g, row0, n_rows, mod_row):
    final = final_g is not None
    in_specs = [
        pl.BlockSpec((N_EXP, ROW), lambda r: (row0 + r, 0)),
        pl.BlockSpec((N_EXP, None, CAP, D), lambda r: (0, row0 + r, 0, 0)),
        pl.BlockSpec((None, ROW, D), lambda r: (row0 + r, 0, 0)),
        pl.BlockSpec((None, None, 1, D), lambda r: (l, mod_row(r), 0, 5)),
    ]
    args = [slot, ye, xmid, mods]
    if final:
        in_specs.append(pl.BlockSpec((1, D), lambda r: (0, 0)))
        args.append(final_g)
    return pl.pallas_call(
        _make_combine_kernel(final),
        out_shape=jax.ShapeDtypeStruct((n_rows, ROW, D), F32),
        grid=(n_rows,),
        in_specs=in_specs,
        out_specs=pl.BlockSpec((None, ROW, D), lambda r: (r, 0, 0)),
        compiler_params=pltpu.CompilerParams(
            dimension_semantics=("arbitrary",), vmem_limit_bytes=VMEM_LIMIT),
        name="combine",
    )(*args)


def _rope_rot_cols(w):
    q = QK_ROPE // 4
    return jnp.concatenate([-w[..., q:2 * q], w[..., 0:q], -w[..., 3 * q:4 * q], w[..., 2 * q:3 * q]], axis=-1)


def _pad_lanes(w):
    return jnp.pad(w, [(0, 0)] * (w.ndim - 1) + [(0, LANES - w.shape[-1])])


def _prep_weights(norm1_g, norm2_g, w_in, q_norm_g, kv_norm_g, w_uq, w_ukv, conv_w, spatial_w,
                  spatial_b, w_out, w_router):
    o_pe, o_conv = Q_RANK + KV_RANK, Q_RANK + KV_RANK + QK_ROPE
    w_kpe = w_in[..., o_pe:o_conv]
    w_in_p = jnp.concatenate(
        [w_in[..., :o_pe], _pad_lanes(w_kpe), _pad_lanes(_rope_rot_cols(w_kpe)), w_in[..., o_conv:]],
        axis=-1).astype(BF16)
    uq = w_uq.reshape(DEPTH, Q_RANK, HEADS, QK_NOPE + QK_ROPE)
    uq_pe = uq[..., QK_NOPE:]
    w_uq_p = jnp.concatenate(
        [uq[..., :QK_NOPE].reshape(DEPTH, Q_RANK, HEADS * QK_NOPE),
         _pad_lanes(uq_pe).reshape(DEPTH, Q_RANK, HEADS * LANES),
         _pad_lanes(_rope_rot_cols(uq_pe)).reshape(DEPTH, Q_RANK, HEADS * LANES)], axis=-1).astype(BF16)
    sb = jnp.repeat(jnp.swapaxes(spatial_b, 1, 2), CMLP_W // CMLP_G, axis=-1)
    return dict(
        norm1_g=norm1_g.reshape(DEPTH, 1, D), norm2_g=norm2_g.reshape(DEPTH, 1, D),
        w_in=w_in_p, q_norm_g=q_norm_g.reshape(DEPTH, 1, Q_RANK),
        kv_norm_g=kv_norm_g.reshape(DEPTH, 1, KV_RANK), w_uq=w_uq_p, w_ukv=w_ukv.astype(BF16),
        conv_w=conv_w, spatial_w=spatial_w.reshape(DEPTH, CMLP_G * CHUNK, CHUNK).astype(BF16),
        spatial_b=sb, w_out=w_out.astype(BF16), w_router_t=jnp.swapaxes(w_router, 1, 2))


def _rope_tables(n):
    rows = n // GRID_W
    row = jnp.repeat(jnp.arange(rows), GRID_W).astype(F32)
    col = jnp.tile(jnp.arange(GRID_W), rows).astype(F32)
    n_freq = QK_ROPE // 4
    inv = ROPE_BASE ** (-jnp.arange(n_freq, dtype=F32) / n_freq)
    ang_r, ang_c = row[:, None] * inv, col[:, None] * inv
    ang = jnp.concatenate([ang_r, ang_r, ang_c, ang_c], axis=-1)
    return _pad_lanes(jnp.cos(ang)), _pad_lanes(jnp.sin(ang))


def kernel(x_prompt, x_sample, cache_ckv, cache_krope, c, c_ctx, w_ada, b_ada, norm1_g, norm2_g, w_in,
           q_norm_g, kv_norm_g, w_uq, w_ukv, conv_w, spatial_w, spatial_b, w_out, w_router, w_gate,
           w_up, w_down, final_norm_g):
    wts = _prep_weights(norm1_g, norm2_g, w_in, q_norm_g, kv_norm_g, w_uq, w_ukv, conv_w, spatial_w,
                        spatial_b, w_out, w_router)
    rope_tabs = _rope_tables(DEC_SEQ)
    cond = jnp.concatenate(
        [c, c_ctx[None, :], jnp.zeros((MOD_ROWS - DEC_BATCH - 1, D), F32)], axis=0)
    mods = _ada_call(cond, w_ada, b_ada).reshape(DEPTH, MOD_ROWS, 1, 6 * D)
    final_g = final_norm_g.reshape(1, D)
    row_mod = lambda r: jnp.where(r < P_ROWS, CTX_MOD_ROW, r - P_ROWS)

    xp, xs, xs_row0 = x_prompt.reshape(P_ROWS, ROW, D), x_sample, 0
    ckv_list, kpe_list = [], []
    for l in range(DEPTH):
        xmid, h2, aff, ckv, kpe = _mixer_call(l, xp, xs, xs_row0, mods, wts, rope_tabs, (cache_ckv, cache_krope))
        ckv_list.append(ckv.reshape(BATCH, SEQ, KV_RANK))
        kpe_list.append(kpe.reshape(BATCH, SEQ, QK_ROPE))
        aff = aff.reshape(N_ROWS * N_EXP, ROW)
        slot = _route_call(aff)
        ye = _expert_call(l, slot, aff, _gather_call(slot, h2), w_gate, w_up, w_down)
        if l < DEPTH - 1:
            x = _combine_call(l, slot, ye, xmid, mods, None, 0, N_ROWS, row_mod)
            xp, xs, xs_row0 = x, x, P_ROWS
        else:
            y_prompt = _combine_call(l, slot, ye, xmid, mods, final_g, 0, P_ROWS, lambda r: CTX_MOD_ROW)
            y_sample = _combine_call(l, slot, ye, xmid, mods, final_g, P_ROWS, DEC_BATCH, lambda r: r)
    return (y_prompt.reshape(BATCH, SEQ, D), y_sample,
            jnp.stack(ckv_list, axis=1), jnp.stack(kpe_list, axis=1))
```

```python
import jax
import jax.numpy as jnp
from jax import lax
from jax.experimental import pallas as pl
from jax.experimental.pallas import tpu as pltpu

F32 = jnp.float32
BF16 = jnp.bfloat16
I32 = jnp.int32

D = 1024
BATCH = 16
SEQ = 256
DEPTH = 4
DEC_BATCH = 8
DEC_SEQ = 1024
PAST_LEN = 512
GRID_W = 64
HEADS = 4
QK_NOPE = 128
QK_ROPE = 64
V_HEAD = 128
Q_RANK = 256
KV_RANK = 128
CONV_W = 256
CMLP_W = 256
CMLP_G = 4
CHUNK = 128
N_EXP = 16
EC_FACTOR = 2
FF = D // 2
ROPE_BASE = 10000.0
EPS = 1e-6
LOG2_E = 1.4426950408889634

LANES = 128
ROW = 1024
N_ROWS = (BATCH * SEQ + DEC_BATCH * DEC_SEQ) // ROW
P_ROWS = BATCH * SEQ // ROW
P_PER_ROW = ROW // SEQ
CAP = EC_FACTOR * ROW // N_EXP
CAP_P = EC_FACTOR * SEQ // N_EXP
CTX_MOD_ROW = DEC_BATCH
MOD_ROWS = 16
QK_PAD = QK_NOPE + LANES

C_Q, C_KV, C_KPE, C_KPR, C_CONV, C_CMLP, C_END = 0, 256, 384, 512, 640, 1408, 1920
U_NOPE, U_PE, U_ROT, U_END = 0, HEADS * QK_NOPE, HEADS * (QK_NOPE + LANES), HEADS * (QK_NOPE + 2 * LANES)
TB = 256
TQ = 256
AT_BLOCKS = 2
VMEM_LIMIT = 56 * 1024 * 1024

NT = (((1,), (1,)), ((), ()))
TN = (((0,), (0,)), ((), ()))


def _rms(x, g):
    return x * lax.rsqrt(jnp.mean(x * x, axis=-1, keepdims=True) + EPS) * g


def _silu(x):
    return x / (1.0 + jnp.exp(-x))


def _dot(a, b):
    return jnp.dot(a, b, preferred_element_type=F32)


def _dot_nt(a, b):
    return lax.dot_general(a, b, NT, preferred_element_type=F32)


def _skewed(stages, items):
    n = len(stages)
    for t in range(len(items) + n - 1):
        for s in range(n - 1, -1, -1):
            if 0 <= t - s < len(items):
                stages[s](items[t - s])


def _ada_kernel(c_ref, w_ref, b_ref, o_ref):
    s = _silu(c_ref[...]).astype(BF16)
    o_ref[...] = _dot(s, w_ref[...].astype(BF16)) + b_ref[...]


def _ada_call(cond, w_ada, b_ada):
    nc = 6
    return pl.pallas_call(
        _ada_kernel,
        out_shape=jax.ShapeDtypeStruct((DEPTH, MOD_ROWS, 6 * D), F32),
        grid=(DEPTH, nc),
        in_specs=[
            pl.BlockSpec((MOD_ROWS, D), lambda l, j: (0, 0)),
            pl.BlockSpec((None, D, D), lambda l, j: (l, 0, j)),
            pl.BlockSpec((None, 1, D), lambda l, j: (l, 0, j)),
        ],
        out_specs=pl.BlockSpec((None, MOD_ROWS, D), lambda l, j: (l, 0, j)),
        compiler_params=pltpu.CompilerParams(
            dimension_semantics=("arbitrary", "arbitrary"), vmem_limit_bytes=VMEM_LIMIT),
        name="ada_mod",
    )(cond, w_ada, b_ada.reshape(DEPTH, 1, 6 * D))


def _mixer_kernel(l_ref, x_ref, mod_ref, g1n_ref, g2n_ref, win_ref, qg_ref, kvg_ref, wuq_ref, wukv_ref,
                  cw_ref, sw_ref, sb_ref, wout_ref, wr_ref, cos_ref, sin_ref, cckv_ref, ckr_ref,
                  xmid_ref, h2_ref, aff_ref, ckv_out, kpe_out,
                  qf_s, kf_s, v_s, mix_s, gb_s, z_s):
    qk_scale = float(QK_NOPE + QK_ROPE) ** -0.5 * LOG2_E
    nblk = ROW // TB

    def run(ctx):
        n_past = 0 if ctx else PAST_LEN
        seg = SEQ if ctx else ROW
        mod = mod_ref[...]
        sh1, sc1, g1 = mod[:, 0:D], mod[:, D:2 * D], mod[:, 2 * D:3 * D]
        sh2, sc2, g2 = mod[:, 3 * D:4 * D], mod[:, 4 * D:5 * D], mod[:, 5 * D:6 * D]

        def store_kv(row0, rows, kv, kpb):
            for h in range(HEADS):
                c0 = h * (QK_NOPE + V_HEAD)
                kf_s[h, pl.ds(row0, rows), 0:QK_NOPE] = kv[:, c0:c0 + QK_NOPE].astype(BF16)
                kf_s[h, pl.ds(row0, rows), QK_NOPE:QK_PAD] = kpb
                v_s[h, pl.ds(row0, rows), :] = kv[:, c0 + QK_NOPE:c0 + QK_NOPE + V_HEAD].astype(BF16)

        if not ctx:
            kr = ckr_ref[...]
            store_kv(0, PAST_LEN, _dot(cckv_ref[...].astype(BF16), wukv_ref[...]),
                     jnp.concatenate([kr, jnp.zeros_like(kr)], axis=1).astype(BF16))

        grp_shift = (CMLP_W // CMLP_G).bit_length() - 1
        lane_grp = lax.shift_right_logical(lax.broadcasted_iota(I32, (CHUNK, CMLP_W), 1), grp_shift)

        def p1_norm(c):
            x = x_ref[pl.ds(c["r0"], TB), :]
            c["hb"] = (_rms(x, g1n_ref[...]) * (1.0 + sc1) + sh1).astype(BF16)

        def p1_proj(c):
            c["pa"] = _dot(c["hb"], win_ref[:, C_Q:C_CONV])
            c["cv"] = _dot(c["hb"], win_ref[:, C_CONV:C_CMLP])
            c["cm"] = _dot(c.pop("hb"), win_ref[:, C_CMLP:C_END])

        def p1_mid(c):
            r0, pa, cv, cm = c["r0"], c.pop("pa"), c.pop("cv"), c.pop("cm")
            kp, kpr = pa[:, C_KPE:C_KPR], pa[:, C_KPR:C_CONV]
            c["cqn"] = _rms(pa[:, C_Q:C_KV], qg_ref[...]).astype(BF16)
            ckv_n = _rms(pa[:, C_KV:C_KPE], kvg_ref[...])
            if ctx:
                ckv_out[pl.ds(r0, TB), :] = ckv_n
                kpe_out[pl.ds(r0, TB), :] = kp[:, 0:QK_ROPE]
                kpe = kp
            else:
                kpe = kp * cos_ref[pl.ds(r0, TB), :] + kpr * sin_ref[pl.ds(r0, TB), :]
            c["ckv_b"], c["kpe_b"] = ckv_n.astype(BF16), kpe.astype(BF16)
            gb_s[pl.ds(r0, TB), :] = cv[:, 0:CONV_W]
            z_s[pl.ds(r0, TB), :] = cv[:, CONV_W:2 * CONV_W] * cv[:, 2 * CONV_W:3 * CONV_W]
            c["u"], c["vvb"] = cm[:, 0:CMLP_W], cm[:, CMLP_W:2 * CMLP_W].astype(BF16)

        def p1_up(c):
            c["qa"] = _dot(c.pop("cqn"), wuq_ref[...])
            c["kv"] = _dot(c.pop("ckv_b"), wukv_ref[...])
            vvb = c.pop("vvb")
            c["r"] = [_dot(sw_ref[...], vvb[k * CHUNK:(k + 1) * CHUNK, :]) for k in range(TB // CHUNK)]

        def p1_out(c):
            r0, qa, u = c["r0"], c.pop("qa"), c.pop("u")
            for h in range(HEADS):
                qn = qa[:, U_NOPE + h * QK_NOPE:U_NOPE + (h + 1) * QK_NOPE]
                qp = qa[:, U_PE + h * LANES:U_PE + (h + 1) * LANES]
                if not ctx:
                    qp = (qp * cos_ref[pl.ds(r0, TB), :]
                          + qa[:, U_ROT + h * LANES:U_ROT + (h + 1) * LANES] * sin_ref[pl.ds(r0, TB), :])
                qf_s[h, pl.ds(r0, TB), 0:QK_NOPE] = (qn * qk_scale).astype(BF16)
                qf_s[h, pl.ds(r0, TB), QK_NOPE:QK_PAD] = (qp * qk_scale).astype(BF16)
            store_kv(n_past + r0, TB, c.pop("kv"), c.pop("kpe_b"))
            for k, r in enumerate(c.pop("r")):
                mixed = jnp.where(
                    lane_grp == 0, r[0:CHUNK],
                    jnp.where(lane_grp == 1, r[CHUNK:2 * CHUNK],
                              jnp.where(lane_grp == 2, r[2 * CHUNK:3 * CHUNK], r[3 * CHUNK:4 * CHUNK])))
                out = u[k * CHUNK:(k + 1) * CHUNK, :] * (mixed + sb_ref[...])
                mix_s[pl.ds(r0 + k * CHUNK, CHUNK), HEADS * V_HEAD + CONV_W:D] = out.astype(BF16)

        _skewed([p1_norm, p1_proj, p1_mid, p1_up, p1_out], [dict(r0=i * TB) for i in range(nblk)])

        z = z_s[...]
        pos = lax.broadcasted_iota(I32, (ROW, CONV_W), 0) & (seg - 1)
        zm = jnp.where(pos == 0, 0.0, pltpu.roll(z, 1, 0))
        zp = jnp.where(pos == seg - 1, 0.0, pltpu.roll(z, ROW - 1, 0))
        cw = cw_ref[...]
        conv = zm * cw[0:1, :] + z * cw[1:2, :] + zp * cw[2:3, :]
        mix_s[:, HEADS * V_HEAD:HEADS * V_HEAD + CONV_W] = (gb_s[...] * conv).astype(BF16)

        def at_qk(c):
            keys = kf_s[c["h"], pl.ds(c["k0"], seg + n_past), :]
            c["s"] = _dot_nt(qf_s[c["h"], pl.ds(c["r0"], TQ), :], keys)

        def at_max(c):
            c["m"] = jnp.max(c["s"], axis=-1, keepdims=True)

        def at_exp(c):
            e = jnp.exp2(c.pop("s") - c.pop("m"))
            c["l"] = jnp.sum(e, axis=-1, keepdims=True)
            c["p"] = e.astype(BF16)

        def at_pv(c):
            c["o"] = _dot(c.pop("p"), v_s[c["h"], pl.ds(c["k0"], seg + n_past), :])

        def at_out(c):
            o = c.pop("o") / c.pop("l")
            mix_s[pl.ds(c["r0"], TQ), c["h"] * V_HEAD:(c["h"] + 1) * V_HEAD] = o.astype(BF16)

        at_stages = [at_qk, lambda c: (at_max(c), at_exp(c)), lambda c: (at_pv(c), at_out(c))]
        if ctx:
            _skewed(at_stages, [dict(r0=b * TQ, k0=b * TQ, h=h) for b in range(ROW // TQ) for h in range(HEADS)])
        else:
            def at_block(b, carry):
                r0 = pl.multiple_of(b * (AT_BLOCKS * TQ), AT_BLOCKS * TQ)
                _skewed(at_stages, [dict(r0=r0 + k * TQ, k0=0, h=h) for k in range(AT_BLOCKS) for h in range(HEADS)])
                return carry

            lax.fori_loop(0, ROW // (AT_BLOCKS * TQ), at_block, 0)

        wr = wr_ref[...]
        wr_hi = wr.astype(BF16)
        wr_hl = jnp.concatenate([wr_hi, (wr - wr_hi.astype(F32)).astype(BF16)], axis=0)

        def p2_proj(c):
            c["mo"] = _dot(mix_s[pl.ds(c["r0"], TB), :], wout_ref[...])

        def p2_norm(c):
            r0 = c["r0"]
            xm = x_ref[pl.ds(r0, TB), :] + g1 * c.pop("mo")
            xmid_ref[pl.ds(r0, TB), :] = xm
            h2 = _rms(xm, g2n_ref[...]) * (1.0 + sc2) + sh2
            h2_hi = h2.astype(BF16)
            h2_ref[pl.ds(r0, TB), :] = h2_hi
            c["hi"], c["lo"] = h2_hi, (h2 - h2_hi.astype(F32)).astype(BF16)

        def p2_route(c):
            la = _dot_nt(wr_hl, c["hi"])
            c["lg"] = la[0:N_EXP] + la[N_EXP:2 * N_EXP] + _dot_nt(wr_hi, c.pop("lo"))
            c.pop("hi")

        def p2_aff(c):
            lg = c.pop("lg")
            e = jnp.exp(lg - jnp.max(lg, axis=0, keepdims=True))
            aff_ref[:, pl.ds(c["r0"], TB)] = e / jnp.sum(e, axis=0, keepdims=True)

        _skewed([p2_proj, p2_norm, p2_route, p2_aff], [dict(r0=i * TB) for i in range(nblk)])

    is_ctx = pl.program_id(0) < P_ROWS

    @pl.when(is_ctx)
    def _():
        run(True)

    @pl.when(jnp.logical_not(is_ctx))
    def _():
        run(False)


def _layer_spec(shape):
    nd = len(shape)
    return pl.BlockSpec((None,) + shape, lambda r, l: (l[0],) + (0,) * nd, pipeline_mode=pl.Buffered(1))


def _layer_arg(l):
    return jnp.full((1,), l, I32)


def _mixer_call(l, x, mods, wts, rope_tabs, caches):
    lk = PAST_LEN + ROW
    lat = lambda r: jnp.maximum(r - P_ROWS, 0)
    ctx = lambda r: jnp.minimum(r, P_ROWS - 1)
    in_specs = [
        pl.BlockSpec((None, ROW, D), lambda r, l: (r, 0, 0)),
        pl.BlockSpec((None, None, 1, 6 * D),
                     lambda r, l: (l[0], jnp.where(r < P_ROWS, CTX_MOD_ROW, r - P_ROWS), 0, 0)),
        _layer_spec((1, D)), _layer_spec((1, D)),
        _layer_spec((D, C_END)),
        _layer_spec((1, Q_RANK)), _layer_spec((1, KV_RANK)),
        _layer_spec((Q_RANK, U_END)),
        _layer_spec((KV_RANK, HEADS * (QK_NOPE + V_HEAD))),
        _layer_spec((3, CONV_W)),
        _layer_spec((CMLP_G * CHUNK, CHUNK)),
        _layer_spec((CHUNK, CMLP_W)),
        _layer_spec((D, D)),
        _layer_spec((N_EXP, D)),
        pl.BlockSpec((ROW, LANES), lambda r, l: (0, 0), pipeline_mode=pl.Buffered(1)),
        pl.BlockSpec((ROW, LANES), lambda r, l: (0, 0), pipeline_mode=pl.Buffered(1)),
        pl.BlockSpec((None, None, PAST_LEN, KV_RANK), lambda r, l: (lat(r), l[0], 0, 0)),
        pl.BlockSpec((None, None, PAST_LEN, QK_ROPE), lambda r, l: (lat(r), l[0], 0, 0)),
    ]
    args = [x, mods, wts["norm1_g"], wts["norm2_g"], wts["w_in"], wts["q_norm_g"], wts["kv_norm_g"],
            wts["w_uq"], wts["w_ukv"], wts["conv_w"], wts["spatial_w"], wts["spatial_b"],
            wts["w_out"], wts["w_router_t"], rope_tabs[0], rope_tabs[1], caches[0], caches[1]]
    out_shape = [jax.ShapeDtypeStruct((N_ROWS, ROW, D), F32),
                 jax.ShapeDtypeStruct((N_ROWS, ROW, D), BF16),
                 jax.ShapeDtypeStruct((N_ROWS, N_EXP, ROW), F32),
                 jax.ShapeDtypeStruct((P_ROWS, ROW, KV_RANK), F32),
                 jax.ShapeDtypeStruct((P_ROWS, ROW, QK_ROPE), F32)]
    out_specs = [pl.BlockSpec((None, ROW, D), lambda r, l: (r, 0, 0)),
                 pl.BlockSpec((None, ROW, D), lambda r, l: (r, 0, 0)),
                 pl.BlockSpec((None, N_EXP, ROW), lambda r, l: (r, 0, 0)),
                 pl.BlockSpec((None, ROW, KV_RANK), lambda r, l: (ctx(r), 0, 0)),
                 pl.BlockSpec((None, ROW, QK_ROPE), lambda r, l: (ctx(r), 0, 0))]
    return pl.pallas_call(
        _mixer_kernel,
        out_shape=out_shape,
        grid_spec=pltpu.PrefetchScalarGridSpec(
            num_scalar_prefetch=1,
            grid=(N_ROWS,),
            in_specs=in_specs,
            out_specs=out_specs,
            scratch_shapes=[
                pltpu.VMEM((HEADS, ROW, QK_PAD), BF16),
                pltpu.VMEM((HEADS, lk, QK_PAD), BF16),
                pltpu.VMEM((HEADS, lk, V_HEAD), BF16),
                pltpu.VMEM((ROW, D), BF16),
                pltpu.VMEM((ROW, CONV_W), F32),
                pltpu.VMEM((ROW, CONV_W), F32),
            ]),
        compiler_params=pltpu.CompilerParams(
            dimension_semantics=("arbitrary",), vmem_limit_bytes=VMEM_LIMIT),
        name="mixer",
    )(_layer_arg(l), *args)


def _cap_thresholds(affs, caps):
    def bit_step(i, ts):
        bit = jnp.left_shift(jnp.int32(1), 30 - i)
        out = []
        for aff, cap, t in zip(affs, caps, ts):
            cand = t | bit
            cnt = jnp.sum(jnp.where(aff >= pltpu.bitcast(cand, F32), 1.0, 0.0), axis=1, keepdims=True)
            out.append(jnp.where(cnt >= float(cap), cand, t))
        return tuple(out)

    ts = lax.fori_loop(0, 31, bit_step, tuple(jnp.zeros((a.shape[0], 1), I32) for a in affs))
    return [pltpu.bitcast(t, F32) for t in ts]


def _select_slots(aff, thr, cap, tri):
    capf = float(cap)
    gt = aff > thr
    eq = aff == thr
    n_gt = jnp.sum(jnp.where(gt, 1.0, 0.0), axis=1, keepdims=True)
    eq_rank = _dot(jnp.where(eq, 1.0, 0.0).astype(BF16), tri)
    sel = jnp.where(gt, 1.0, jnp.where(eq, jnp.where(eq_rank < capf - n_gt, 1.0, 0.0), 0.0))
    pos = _dot(sel.astype(BF16), tri)
    return jnp.where(sel > 0.5, pos.astype(I32), -1)


def _route_kernel(aff_ref, slot_ref, tri_s):
    r = lax.broadcasted_iota(I32, (ROW, ROW), 0)
    c = lax.broadcasted_iota(I32, (ROW, ROW), 1)
    tri_s[...] = jnp.where(r < c, 1.0, 0.0).astype(BF16)
    np_ = P_ROWS * N_EXP
    affs = [aff_ref[0:np_, s * SEQ:(s + 1) * SEQ] for s in range(P_PER_ROW)] + [aff_ref[np_:, :]]
    caps = [CAP_P] * P_PER_ROW + [CAP]
    thrs = _cap_thresholds(affs, caps)
    for s in range(P_PER_ROW):
        sl = _select_slots(affs[s], thrs[s], CAP_P, tri_s[0:SEQ, 0:SEQ])
        slot_ref[0:np_, s * SEQ:(s + 1) * SEQ] = jnp.where(sl >= 0, sl + s * CAP_P, -1)
    slot_ref[np_:, :] = _select_slots(affs[-1], thrs[-1], CAP, tri_s[...])


def _route_call(aff):
    rows = N_ROWS * N_EXP
    return pl.pallas_call(
        _route_kernel,
        out_shape=jax.ShapeDtypeStruct((rows, ROW), I32),
        scratch_shapes=[pltpu.VMEM((ROW, ROW), BF16)],
        compiler_params=pltpu.CompilerParams(vmem_limit_bytes=VMEM_LIMIT),
        name="route",
    )(aff)


GATHER_NC = 256


def _gather_kernel(slot_ref, h2_ref, xs_ref, hot_s):
    j = lax.broadcasted_iota(I32, (CAP, ROW), 0)
    for e in range(N_EXP):
        hot_s[e * CAP:(e + 1) * CAP, :] = jnp.where(slot_ref[e:e + 1, :] == j, 1.0, 0.0).astype(BF16)
    for c in range(D // GATHER_NC):
        xs = _dot(hot_s[...], h2_ref[:, c * GATHER_NC:(c + 1) * GATHER_NC]).astype(BF16)
        for e in range(N_EXP):
            xs_ref[e, :, c * GATHER_NC:(c + 1) * GATHER_NC] = xs[e * CAP:(e + 1) * CAP, :]


def _gather_call(slot, h2):
    return pl.pallas_call(
        _gather_kernel,
        out_shape=jax.ShapeDtypeStruct((N_EXP, N_ROWS, CAP, D), BF16),
        grid=(N_ROWS,),
        in_specs=[
            pl.BlockSpec((N_EXP, ROW), lambda r: (r, 0)),
            pl.BlockSpec((None, ROW, D), lambda r: (r, 0, 0)),
        ],
        out_specs=pl.BlockSpec((N_EXP, None, CAP, D), lambda r: (0, r, 0, 0)),
        scratch_shapes=[pltpu.VMEM((N_EXP * CAP, ROW), BF16)],
        compiler_params=pltpu.CompilerParams(
            dimension_semantics=("arbitrary",), vmem_limit_bytes=VMEM_LIMIT),
        name="gather",
    )(slot, h2)


EXP_ROWS = 4


def _expert_kernel(l_ref, slot_ref, aff_ref, xs_ref, wg_ref, wu_ref, wd_ref, ye_ref, wg_s, wu_s, wd_s):
    e = pl.program_id(0)
    wg_s[...] = wg_ref[...].astype(BF16)
    wu_s[...] = wu_ref[...].astype(BF16)
    wd_s[...] = wd_ref[...].astype(BF16)
    j = lax.broadcasted_iota(I32, (CAP, ROW), 0)

    def group(g, carry):
        r0 = g * EXP_ROWS
        xs = xs_ref[pl.ds(r0, EXP_ROWS)].reshape(EXP_ROWS * CAP, D)
        hid = _silu(_dot(xs, wg_s[...])) * _dot(xs, wu_s[...])
        ye = _dot(hid.astype(BF16), wd_s[...])
        for k in range(EXP_ROWS):
            row = (r0 + k) * N_EXP + e
            hit = slot_ref[pl.ds(row, 1), :] == j
            gate = jnp.sum(jnp.where(hit, aff_ref[pl.ds(row, 1), :], 0.0), axis=1, keepdims=True)
            ye_ref[r0 + k] = (ye[k * CAP:(k + 1) * CAP, :] * gate).astype(BF16)
        return carry

    lax.fori_loop(0, N_ROWS // EXP_ROWS, group, 0)


def _expert_call(l, slot, aff, xs, w_gate, w_up, w_down):
    rows = N_ROWS * N_EXP
    return pl.pallas_call(
        _expert_kernel,
        out_shape=jax.ShapeDtypeStruct((N_EXP, N_ROWS, CAP, D), BF16),
        grid_spec=pltpu.PrefetchScalarGridSpec(
            num_scalar_prefetch=1,
            grid=(N_EXP,),
            in_specs=[
                pl.BlockSpec((rows, ROW), lambda e, l: (0, 0), pipeline_mode=pl.Buffered(1)),
                pl.BlockSpec((rows, ROW), lambda e, l: (0, 0), pipeline_mode=pl.Buffered(1)),
                pl.BlockSpec((None, N_ROWS, CAP, D), lambda e, l: (e, 0, 0, 0)),
                pl.BlockSpec((None, None, D, FF), lambda e, l: (l[0], e, 0, 0)),
                pl.BlockSpec((None, None, D, FF), lambda e, l: (l[0], e, 0, 0)),
                pl.BlockSpec((None, None, FF, D), lambda e, l: (l[0], e, 0, 0)),
            ],
            out_specs=pl.BlockSpec((None, N_ROWS, CAP, D), lambda e, l: (e, 0, 0, 0)),
            scratch_shapes=[pltpu.VMEM((D, FF), BF16), pltpu.VMEM((D, FF), BF16), pltpu.VMEM((FF, D), BF16)]),
        compiler_params=pltpu.CompilerParams(
            dimension_semantics=("arbitrary",), vmem_limit_bytes=VMEM_LIMIT),
        name="experts",
    )(_layer_arg(l), slot, aff, xs, w_gate, w_up, w_down)


COMB_TC = 256


def _make_combine_kernel(final):
    def kern(*refs):
        if final:
            _, slot_ref, ye_ref, xmid_ref, g2_ref, fg_ref, o_ref = refs
        else:
            _, slot_ref, ye_ref, xmid_ref, g2_ref, o_ref = refs
        ye = ye_ref[...].reshape(N_EXP * CAP, D)
        g2 = g2_ref[...]
        j = lax.broadcasted_iota(I32, (CAP, COMB_TC), 0)
        for t in range(ROW // COMB_TC):
            cols = slice(t * COMB_TC, (t + 1) * COMB_TC)
            onehot = jnp.concatenate(
                [jnp.where(slot_ref[e:e + 1, cols] == j, 1.0, 0.0).astype(BF16) for e in range(N_EXP)],
                axis=0)
            moe = lax.dot_general(onehot, ye, TN, preferred_element_type=F32)
            xn = xmid_ref[cols, :] + g2 * moe
            if final:
                xn = _rms(xn, fg_ref[...])
            o_ref[cols, :] = xn

    return kern


def _combine_call(l, slot, ye, xmid, mods, final_g, row0, n_rows, mod_row):
    final = final_g is not None
    in_specs = [
        pl.BlockSpec((N_EXP, ROW), lambda r, l: (row0 + r, 0)),
        pl.BlockSpec((N_EXP, None, CAP, D), lambda r, l: (0, row0 + r, 0, 0)),
        pl.BlockSpec((None, ROW, D), lambda r, l: (row0 + r, 0, 0)),
        pl.BlockSpec((None, None, 1, D), lambda r, l: (l[0], mod_row(r), 0, 5)),
    ]
    args = [slot, ye, xmid, mods]
    if final:
        in_specs.append(pl.BlockSpec((1, D), lambda r, l: (0, 0)))
        args.append(final_g)
    return pl.pallas_call(
        _make_combine_kernel(final),
        out_shape=jax.ShapeDtypeStruct((n_rows, ROW, D), F32),
        grid_spec=pltpu.PrefetchScalarGridSpec(
            num_scalar_prefetch=1,
            grid=(n_rows,),
            in_specs=in_specs,
            out_specs=pl.BlockSpec((None, ROW, D), lambda r, l: (r, 0, 0))),
        compiler_params=pltpu.CompilerParams(
            dimension_semantics=("arbitrary",), vmem_limit_bytes=VMEM_LIMIT),
        name="combine",
    )(_layer_arg(l), *args)


def _rope_rot_cols(w):
    q = QK_ROPE // 4
    return jnp.concatenate([-w[..., q:2 * q], w[..., 0:q], -w[..., 3 * q:4 * q], w[..., 2 * q:3 * q]], axis=-1)


def _pad_lanes(w):
    return jnp.pad(w, [(0, 0)] * (w.ndim - 1) + [(0, LANES - w.shape[-1])])


def _prep_weights(norm1_g, norm2_g, w_in, q_norm_g, kv_norm_g, w_uq, w_ukv, conv_w, spatial_w,
                  spatial_b, w_out, w_router):
    o_pe, o_conv = Q_RANK + KV_RANK, Q_RANK + KV_RANK + QK_ROPE
    w_kpe = w_in[..., o_pe:o_conv]
    w_in_p = jnp.concatenate(
        [w_in[..., :o_pe], _pad_lanes(w_kpe), _pad_lanes(_rope_rot_cols(w_kpe)), w_in[..., o_conv:]],
        axis=-1).astype(BF16)
    uq = w_uq.reshape(DEPTH, Q_RANK, HEADS, QK_NOPE + QK_ROPE)
    uq_pe = uq[..., QK_NOPE:]
    w_uq_p = jnp.concatenate(
        [uq[..., :QK_NOPE].reshape(DEPTH, Q_RANK, HEADS * QK_NOPE),
         _pad_lanes(uq_pe).reshape(DEPTH, Q_RANK, HEADS * LANES),
         _pad_lanes(_rope_rot_cols(uq_pe)).reshape(DEPTH, Q_RANK, HEADS * LANES)], axis=-1).astype(BF16)
    sb = jnp.repeat(jnp.swapaxes(spatial_b, 1, 2), CMLP_W // CMLP_G, axis=-1)
    return dict(
        norm1_g=norm1_g.reshape(DEPTH, 1, D), norm2_g=norm2_g.reshape(DEPTH, 1, D),
        w_in=w_in_p, q_norm_g=q_norm_g.reshape(DEPTH, 1, Q_RANK),
        kv_norm_g=kv_norm_g.reshape(DEPTH, 1, KV_RANK), w_uq=w_uq_p, w_ukv=w_ukv.astype(BF16),
        conv_w=conv_w, spatial_w=spatial_w.reshape(DEPTH, CMLP_G * CHUNK, CHUNK).astype(BF16),
        spatial_b=sb, w_out=w_out.astype(BF16), w_router_t=jnp.swapaxes(w_router, 1, 2))


def _rope_tables(n):
    rows = n // GRID_W
    row = jnp.repeat(jnp.arange(rows), GRID_W).astype(F32)
    col = jnp.tile(jnp.arange(GRID_W), rows).astype(F32)
    n_freq = QK_ROPE // 4
    inv = ROPE_BASE ** (-jnp.arange(n_freq, dtype=F32) / n_freq)
    ang_r, ang_c = row[:, None] * inv, col[:, None] * inv
    ang = jnp.concatenate([ang_r, ang_r, ang_c, ang_c], axis=-1)
    return _pad_lanes(jnp.cos(ang)), _pad_lanes(jnp.sin(ang))


def kernel(x_prompt, x_sample, cache_ckv, cache_krope, c, c_ctx, w_ada, b_ada, norm1_g, norm2_g, w_in,
           q_norm_g, kv_norm_g, w_uq, w_ukv, conv_w, spatial_w, spatial_b, w_out, w_router, w_gate,
           w_up, w_down, final_norm_g):
    wts = _prep_weights(norm1_g, norm2_g, w_in, q_norm_g, kv_norm_g, w_uq, w_ukv, conv_w, spatial_w,
                        spatial_b, w_out, w_router)
    rope_tabs = _rope_tables(DEC_SEQ)
    cond = jnp.concatenate(
        [c, c_ctx[None, :], jnp.zeros((MOD_ROWS - DEC_BATCH - 1, D), F32)], axis=0)
    mods = _ada_call(cond, w_ada, b_ada).reshape(DEPTH, MOD_ROWS, 1, 6 * D)
    final_g = final_norm_g.reshape(1, D)
    row_mod = lambda r: jnp.where(r < P_ROWS, CTX_MOD_ROW, r - P_ROWS)

    x = jnp.concatenate([x_prompt.reshape(P_ROWS, ROW, D), x_sample], axis=0)
    ckv_list, kpe_list = [], []
    for l in range(DEPTH):
        xmid, h2, aff, ckv, kpe = _mixer_call(l, x, mods, wts, rope_tabs, (cache_ckv, cache_krope))
        ckv_list.append(ckv.reshape(BATCH, SEQ, KV_RANK))
        kpe_list.append(kpe.reshape(BATCH, SEQ, QK_ROPE))
        aff = aff.reshape(N_ROWS * N_EXP, ROW)
        slot = _route_call(aff)
        ye = _expert_call(l, slot, aff, _gather_call(slot, h2), w_gate, w_up, w_down)
        if l < DEPTH - 1:
            x = _combine_call(l, slot, ye, xmid, mods, None, 0, N_ROWS, row_mod)
        else:
            y_prompt = _combine_call(l, slot, ye, xmid, mods, final_g, 0, P_ROWS, lambda r: CTX_MOD_ROW)
            y_sample = _combine_call(l, slot, ye, xmid, mods, final_g, P_ROWS, DEC_BATCH, lambda r: r)
    return (y_prompt.reshape(BATCH, SEQ, D), y_sample,
            jnp.stack(ckv_list, axis=1), jnp.stack(kpe_list, axis=1))
```

```python
import jax
import jax.numpy as jnp
from jax import lax
from jax.experimental import pallas as pl
from jax.experimental.pallas import tpu as pltpu

F32 = jnp.float32
BF16 = jnp.bfloat16
I32 = jnp.int32

D = 1024
BATCH = 16
SEQ = 256
DEPTH = 4
DEC_BATCH = 8
DEC_SEQ = 1024
PAST_LEN = 512
GRID_W = 64
HEADS = 4
QK_NOPE = 128
QK_ROPE = 64
V_HEAD = 128
Q_RANK = 256
KV_RANK = 128
CONV_W = 256
CMLP_W = 256
CMLP_G = 4
CHUNK = 128
N_EXP = 16
EC_FACTOR = 2
FF = D // 2
ROPE_BASE = 10000.0
EPS = 1e-6
LOG2_E = 1.4426950408889634

LANES = 128
ROW = 1024
N_ROWS = (BATCH * SEQ + DEC_BATCH * DEC_SEQ) // ROW
P_ROWS = BATCH * SEQ // ROW
P_PER_ROW = ROW // SEQ
CAP = EC_FACTOR * ROW // N_EXP
CAP_P = EC_FACTOR * SEQ // N_EXP
CTX_MOD_ROW = DEC_BATCH
MOD_ROWS = 16
QK_PAD = QK_NOPE + LANES

C_Q, C_KV, C_KPE, C_KPR, C_CONV, C_CMLP, C_END = 0, 256, 384, 512, 640, 1408, 1920
U_NOPE, U_PE, U_ROT, U_END = 0, HEADS * QK_NOPE, HEADS * (QK_NOPE + LANES), HEADS * (QK_NOPE + 2 * LANES)
TB = 256
TQ = 256
AT_BLOCKS = 1
VMEM_LIMIT = 56 * 1024 * 1024

NT = (((1,), (1,)), ((), ()))
TN = (((0,), (0,)), ((), ()))


def _rms(x, g):
    return x * lax.rsqrt(jnp.mean(x * x, axis=-1, keepdims=True) + EPS) * g


def _silu(x):
    return x / (1.0 + jnp.exp(-x))


def _dot(a, b):
    return jnp.dot(a, b, preferred_element_type=F32)


def _dot_nt(a, b):
    return lax.dot_general(a, b, NT, preferred_element_type=F32)


def _skewed(stages, items):
    n = len(stages)
    for t in range(len(items) + n - 1):
        for s in range(n - 1, -1, -1):
            if 0 <= t - s < len(items):
                stages[s](items[t - s])


def _ada_kernel(c_ref, w_ref, b_ref, o_ref):
    s = _silu(c_ref[...]).astype(BF16)
    o_ref[...] = _dot(s, w_ref[...].astype(BF16)) + b_ref[...]


def _ada_call(cond, w_ada, b_ada):
    nc = 6
    return pl.pallas_call(
        _ada_kernel,
        out_shape=jax.ShapeDtypeStruct((DEPTH, MOD_ROWS, 6 * D), F32),
        grid=(DEPTH, nc),
        in_specs=[
            pl.BlockSpec((MOD_ROWS, D), lambda l, j: (0, 0)),
            pl.BlockSpec((None, D, D), lambda l, j: (l, 0, j)),
            pl.BlockSpec((None, 1, D), lambda l, j: (l, 0, j)),
        ],
        out_specs=pl.BlockSpec((None, MOD_ROWS, D), lambda l, j: (l, 0, j)),
        compiler_params=pltpu.CompilerParams(
            dimension_semantics=("arbitrary", "arbitrary"), vmem_limit_bytes=VMEM_LIMIT),
        name="ada_mod",
    )(cond, w_ada, b_ada.reshape(DEPTH, 1, 6 * D))


def _mixer_kernel(l_ref, x_ref, mod_ref, g1n_ref, g2n_ref, win_ref, qg_ref, kvg_ref, wuq_ref, wukv_ref,
                  cw_ref, sw_ref, sb_ref, wout_ref, wr_ref, cos_ref, sin_ref, cckv_ref, ckr_ref,
                  xmid_ref, h2_ref, aff_ref, ckv_out, kpe_out,
                  qf_s, kf_s, v_s, mix_s, gb_s, z_s):
    qk_scale = float(QK_NOPE + QK_ROPE) ** -0.5 * LOG2_E
    nblk = ROW // TB

    def run(ctx):
        n_past = 0 if ctx else PAST_LEN
        seg = SEQ if ctx else ROW
        mod = mod_ref[...]
        sh1, sc1, g1 = mod[:, 0:D], mod[:, D:2 * D], mod[:, 2 * D:3 * D]
        sh2, sc2, g2 = mod[:, 3 * D:4 * D], mod[:, 4 * D:5 * D], mod[:, 5 * D:6 * D]

        def store_kv(row0, rows, kv, kpb):
            for h in range(HEADS):
                c0 = h * (QK_NOPE + V_HEAD)
                kf_s[h, pl.ds(row0, rows), 0:QK_NOPE] = kv[:, c0:c0 + QK_NOPE].astype(BF16)
                kf_s[h, pl.ds(row0, rows), QK_NOPE:QK_PAD] = kpb
                v_s[h, pl.ds(row0, rows), :] = kv[:, c0 + QK_NOPE:c0 + QK_NOPE + V_HEAD].astype(BF16)

        if not ctx:
            kr = ckr_ref[...]
            store_kv(0, PAST_LEN, _dot(cckv_ref[...].astype(BF16), wukv_ref[...]),
                     jnp.concatenate([kr, jnp.zeros_like(kr)], axis=1).astype(BF16))

        grp_shift = (CMLP_W // CMLP_G).bit_length() - 1
        lane_grp = lax.shift_right_logical(lax.broadcasted_iota(I32, (CHUNK, CMLP_W), 1), grp_shift)

        def p1_norm(c):
            x = x_ref[pl.ds(c["r0"], TB), :]
            c["hb"] = (_rms(x, g1n_ref[...]) * (1.0 + sc1) + sh1).astype(BF16)

        def p1_proj(c):
            c["pa"] = _dot(c["hb"], win_ref[:, C_Q:C_CONV])
            c["cv"] = _dot(c["hb"], win_ref[:, C_CONV:C_CMLP])
            c["cm"] = _dot(c.pop("hb"), win_ref[:, C_CMLP:C_END])

        def p1_mid(c):
            r0, pa, cv, cm = c["r0"], c.pop("pa"), c.pop("cv"), c.pop("cm")
            kp, kpr = pa[:, C_KPE:C_KPR], pa[:, C_KPR:C_CONV]
            c["cqn"] = _rms(pa[:, C_Q:C_KV], qg_ref[...]).astype(BF16)
            ckv_n = _rms(pa[:, C_KV:C_KPE], kvg_ref[...])
            if ctx:
                ckv_out[pl.ds(r0, TB), :] = ckv_n
                kpe_out[pl.ds(r0, TB), :] = kp[:, 0:QK_ROPE]
                kpe = kp
            else:
                kpe = kp * cos_ref[pl.ds(r0, TB), :] + kpr * sin_ref[pl.ds(r0, TB), :]
            c["ckv_b"], c["kpe_b"] = ckv_n.astype(BF16), kpe.astype(BF16)
            gb_s[pl.ds(r0, TB), :] = cv[:, 0:CONV_W]
            z_s[pl.ds(r0, TB), :] = cv[:, CONV_W:2 * CONV_W] * cv[:, 2 * CONV_W:3 * CONV_W]
            c["u"], c["vvb"] = cm[:, 0:CMLP_W], cm[:, CMLP_W:2 * CMLP_W].astype(BF16)

        def p1_up(c):
            c["qa"] = _dot(c.pop("cqn"), wuq_ref[...])
            c["kv"] = _dot(c.pop("ckv_b"), wukv_ref[...])
            vvb = c.pop("vvb")
            c["r"] = [_dot(sw_ref[...], vvb[k * CHUNK:(k + 1) * CHUNK, :]) for k in range(TB // CHUNK)]

        def p1_out(c):
            r0, qa, u = c["r0"], c.pop("qa"), c.pop("u")
            for h in range(HEADS):
                qn = qa[:, U_NOPE + h * QK_NOPE:U_NOPE + (h + 1) * QK_NOPE]
                qp = qa[:, U_PE + h * LANES:U_PE + (h + 1) * LANES]
                if not ctx:
                    qp = (qp * cos_ref[pl.ds(r0, TB), :]
                          + qa[:, U_ROT + h * LANES:U_ROT + (h + 1) * LANES] * sin_ref[pl.ds(r0, TB), :])
                qf_s[h, pl.ds(r0, TB), 0:QK_NOPE] = (qn * qk_scale).astype(BF16)
                qf_s[h, pl.ds(r0, TB), QK_NOPE:QK_PAD] = (qp * qk_scale).astype(BF16)
            store_kv(n_past + r0, TB, c.pop("kv"), c.pop("kpe_b"))
            for k, r in enumerate(c.pop("r")):
                mixed = jnp.where(
                    lane_grp == 0, r[0:CHUNK],
                    jnp.where(lane_grp == 1, r[CHUNK:2 * CHUNK],
                              jnp.where(lane_grp == 2, r[2 * CHUNK:3 * CHUNK], r[3 * CHUNK:4 * CHUNK])))
                out = u[k * CHUNK:(k + 1) * CHUNK, :] * (mixed + sb_ref[...])
                mix_s[pl.ds(r0 + k * CHUNK, CHUNK), HEADS * V_HEAD + CONV_W:D] = out.astype(BF16)

        _skewed([p1_norm, p1_proj, p1_mid, p1_up, p1_out], [dict(r0=i * TB) for i in range(nblk)])

        z = z_s[...]
        pos = lax.broadcasted_iota(I32, (ROW, CONV_W), 0) & (seg - 1)
        zm = jnp.where(pos == 0, 0.0, pltpu.roll(z, 1, 0))
        zp = jnp.where(pos == seg - 1, 0.0, pltpu.roll(z, ROW - 1, 0))
        cw = cw_ref[...]
        conv = zm * cw[0:1, :] + z * cw[1:2, :] + zp * cw[2:3, :]
        mix_s[:, HEADS * V_HEAD:HEADS * V_HEAD + CONV_W] = (gb_s[...] * conv).astype(BF16)

        def at_qk(c):
            keys = kf_s[c["h"], pl.ds(c["k0"], seg + n_past), :]
            c["s"] = _dot_nt(qf_s[c["h"], pl.ds(c["r0"], TQ), :], keys)

        def at_max(c):
            c["m"] = jnp.max(c["s"], axis=-1, keepdims=True)

        def at_exp(c):
            e = jnp.exp2(c.pop("s") - c.pop("m"))
            c["l"] = jnp.sum(e, axis=-1, keepdims=True)
            c["p"] = e.astype(BF16)

        def at_pv(c):
            c["o"] = _dot(c.pop("p"), v_s[c["h"], pl.ds(c["k0"], seg + n_past), :])

        def at_out(c):
            o = c.pop("o") / c.pop("l")
            mix_s[pl.ds(c["r0"], TQ), c["h"] * V_HEAD:(c["h"] + 1) * V_HEAD] = o.astype(BF16)

        at_stages = [at_qk, lambda c: (at_max(c), at_exp(c)), lambda c: (at_pv(c), at_out(c))]
        if ctx:
            _skewed(at_stages, [dict(r0=b * TQ, k0=b * TQ, h=h) for b in range(ROW // TQ) for h in range(HEADS)])
        else:
            def at_block(b, carry):
                r0 = pl.multiple_of(b * (AT_BLOCKS * TQ), AT_BLOCKS * TQ)
                _skewed(at_stages, [dict(r0=r0 + k * TQ, k0=0, h=h) for k in range(AT_BLOCKS) for h in range(HEADS)])
                return carry

            lax.fori_loop(0, ROW // (AT_BLOCKS * TQ), at_block, 0)

        wr = wr_ref[...]
        wr_hi = wr.astype(BF16)
        wr_hl = jnp.concatenate([wr_hi, (wr - wr_hi.astype(F32)).astype(BF16)], axis=0)

        def p2_proj(c):
            c["mo"] = _dot(mix_s[pl.ds(c["r0"], TB), :], wout_ref[...])

        def p2_norm(c):
            r0 = c["r0"]
            xm = x_ref[pl.ds(r0, TB), :] + g1 * c.pop("mo")
            xmid_ref[pl.ds(r0, TB), :] = xm
            h2 = _rms(xm, g2n_ref[...]) * (1.0 + sc2) + sh2
            h2_hi = h2.astype(BF16)
            h2_ref[pl.ds(r0, TB), :] = h2_hi
            c["hi"], c["lo"] = h2_hi, (h2 - h2_hi.astype(F32)).astype(BF16)

        def p2_route(c):
            la = _dot_nt(wr_hl, c["hi"])
            c["lg"] = la[0:N_EXP] + la[N_EXP:2 * N_EXP] + _dot_nt(wr_hi, c.pop("lo"))
            c.pop("hi")

        def p2_aff(c):
            lg = c.pop("lg")
            e = jnp.exp(lg - jnp.max(lg, axis=0, keepdims=True))
            aff_ref[:, pl.ds(c["r0"], TB)] = e / jnp.sum(e, axis=0, keepdims=True)

        _skewed([p2_proj, p2_norm, p2_route, p2_aff], [dict(r0=i * TB) for i in range(nblk)])

    is_ctx = pl.program_id(0) < P_ROWS

    @pl.when(is_ctx)
    def _():
        run(True)

    @pl.when(jnp.logical_not(is_ctx))
    def _():
        run(False)


def _layer_spec(shape):
    nd = len(shape)
    return pl.BlockSpec((None,) + shape, lambda r, l: (l[0],) + (0,) * nd, pipeline_mode=pl.Buffered(1))


def _layer_arg(l):
    return jnp.full((1,), l, I32)


def _mixer_call(l, x, mods, wts, rope_tabs, caches):
    lk = PAST_LEN + ROW
    lat = lambda r: jnp.maximum(r - P_ROWS, 0)
    ctx = lambda r: jnp.minimum(r, P_ROWS - 1)
    in_specs = [
        pl.BlockSpec((None, ROW, D), lambda r, l: (r, 0, 0)),
        pl.BlockSpec((None, None, 1, 6 * D),
                     lambda r, l: (l[0], jnp.where(r < P_ROWS, CTX_MOD_ROW, r - P_ROWS), 0, 0)),
        _layer_spec((1, D)), _layer_spec((1, D)),
        _layer_spec((D, C_END)),
        _layer_spec((1, Q_RANK)), _layer_spec((1, KV_RANK)),
        _layer_spec((Q_RANK, U_END)),
        _layer_spec((KV_RANK, HEADS * (QK_NOPE + V_HEAD))),
        _layer_spec((3, CONV_W)),
        _layer_spec((CMLP_G * CHUNK, CHUNK)),
        _layer_spec((CHUNK, CMLP_W)),
        _layer_spec((D, D)),
        _layer_spec((N_EXP, D)),
        pl.BlockSpec((ROW, LANES), lambda r, l: (0, 0), pipeline_mode=pl.Buffered(1)),
        pl.BlockSpec((ROW, LANES), lambda r, l: (0, 0), pipeline_mode=pl.Buffered(1)),
        pl.BlockSpec((None, None, PAST_LEN, KV_RANK), lambda r, l: (lat(r), l[0], 0, 0)),
        pl.BlockSpec((None, None, PAST_LEN, QK_ROPE), lambda r, l: (lat(r), l[0], 0, 0)),
    ]
    args = [x, mods, wts["norm1_g"], wts["norm2_g"], wts["w_in"], wts["q_norm_g"], wts["kv_norm_g"],
            wts["w_uq"], wts["w_ukv"], wts["conv_w"], wts["spatial_w"], wts["spatial_b"],
            wts["w_out"], wts["w_router_t"], rope_tabs[0], rope_tabs[1], caches[0], caches[1]]
    out_shape = [jax.ShapeDtypeStruct((N_ROWS, ROW, D), F32),
                 jax.ShapeDtypeStruct((N_ROWS, ROW, D), BF16),
                 jax.ShapeDtypeStruct((N_ROWS, N_EXP, ROW), F32),
                 jax.ShapeDtypeStruct((P_ROWS, ROW, KV_RANK), F32),
                 jax.ShapeDtypeStruct((P_ROWS, ROW, QK_ROPE), F32)]
    out_specs = [pl.BlockSpec((None, ROW, D), lambda r, l: (r, 0, 0)),
                 pl.BlockSpec((None, ROW, D), lambda r, l: (r, 0, 0)),
                 pl.BlockSpec((None, N_EXP, ROW), lambda r, l: (r, 0, 0)),
                 pl.BlockSpec((None, ROW, KV_RANK), lambda r, l: (ctx(r), 0, 0)),
                 pl.BlockSpec((None, ROW, QK_ROPE), lambda r, l: (ctx(r), 0, 0))]
    return pl.pallas_call(
        _mixer_kernel,
        out_shape=out_shape,
        grid_spec=pltpu.PrefetchScalarGridSpec(
            num_scalar_prefetch=1,
            grid=(N_ROWS,),
            in_specs=in_specs,
            out_specs=out_specs,
            scratch_shapes=[
                pltpu.VMEM((HEADS, ROW, QK_PAD), BF16),
                pltpu.VMEM((HEADS, lk, QK_PAD), BF16),
                pltpu.VMEM((HEADS, lk, V_HEAD), BF16),
                pltpu.VMEM((ROW, D), BF16),
                pltpu.VMEM((ROW, CONV_W), F32),
                pltpu.VMEM((ROW, CONV_W), F32),
            ]),
        compiler_params=pltpu.CompilerParams(
            dimension_semantics=("arbitrary",), vmem_limit_bytes=VMEM_LIMIT),
        name="mixer",
    )(_layer_arg(l), *args)


def _cap_thresholds(affs, caps):
    def bit_step(i, ts):
        bit = jnp.left_shift(jnp.int32(1), 30 - i)
        out = []
        for aff, cap, t in zip(affs, caps, ts):
            cand = t | bit
            cnt = jnp.sum(jnp.where(aff >= pltpu.bitcast(cand, F32), 1.0, 0.0), axis=1, keepdims=True)
            out.append(jnp.where(cnt >= float(cap), cand, t))
        return tuple(out)

    ts = lax.fori_loop(0, 31, bit_step, tuple(jnp.zeros((a.shape[0], 1), I32) for a in affs))
    return [pltpu.bitcast(t, F32) for t in ts]


def _select_slots(aff, thr, cap, tri):
    capf = float(cap)
    gt = aff > thr
    eq = aff == thr
    n_gt = jnp.sum(jnp.where(gt, 1.0, 0.0), axis=1, keepdims=True)
    eq_rank = _dot(jnp.where(eq, 1.0, 0.0).astype(BF16), tri)
    sel = jnp.where(gt, 1.0, jnp.where(eq, jnp.where(eq_rank < capf - n_gt, 1.0, 0.0), 0.0))
    pos = _dot(sel.astype(BF16), tri)
    return jnp.where(sel > 0.5, pos.astype(I32), -1)


def _route_kernel(aff_ref, slot_ref, tri_s):
    r = lax.broadcasted_iota(I32, (ROW, ROW), 0)
    c = lax.broadcasted_iota(I32, (ROW, ROW), 1)
    tri_s[...] = jnp.where(r < c, 1.0, 0.0).astype(BF16)
    np_ = P_ROWS * N_EXP
    affs = [aff_ref[0:np_, s * SEQ:(s + 1) * SEQ] for s in range(P_PER_ROW)] + [aff_ref[np_:, :]]
    caps = [CAP_P] * P_PER_ROW + [CAP]
    thrs = _cap_thresholds(affs, caps)
    for s in range(P_PER_ROW):
        sl = _select_slots(affs[s], thrs[s], CAP_P, tri_s[0:SEQ, 0:SEQ])
        slot_ref[0:np_, s * SEQ:(s + 1) * SEQ] = jnp.where(sl >= 0, sl + s * CAP_P, -1)
    slot_ref[np_:, :] = _select_slots(affs[-1], thrs[-1], CAP, tri_s[...])


def _route_call(aff):
    rows = N_ROWS * N_EXP
    return pl.pallas_call(
        _route_kernel,
        out_shape=jax.ShapeDtypeStruct((rows, ROW), I32),
        scratch_shapes=[pltpu.VMEM((ROW, ROW), BF16)],
        compiler_params=pltpu.CompilerParams(vmem_limit_bytes=VMEM_LIMIT),
        name="route",
    )(aff)


GATHER_NC = 256


def _gather_kernel(slot_ref, h2_ref, xs_ref, hot_s):
    j = lax.broadcasted_iota(I32, (CAP, ROW), 0)
    for e in range(N_EXP):
        hot_s[e * CAP:(e + 1) * CAP, :] = jnp.where(slot_ref[e:e + 1, :] == j, 1.0, 0.0).astype(BF16)
    for c in range(D // GATHER_NC):
        xs = _dot(hot_s[...], h2_ref[:, c * GATHER_NC:(c + 1) * GATHER_NC]).astype(BF16)
        for e in range(N_EXP):
            xs_ref[e, :, c * GATHER_NC:(c + 1) * GATHER_NC] = xs[e * CAP:(e + 1) * CAP, :]


def _gather_call(slot, h2):
    return pl.pallas_call(
        _gather_kernel,
        out_shape=jax.ShapeDtypeStruct((N_EXP, N_ROWS, CAP, D), BF16),
        grid=(N_ROWS,),
        in_specs=[
            pl.BlockSpec((N_EXP, ROW), lambda r: (r, 0)),
            pl.BlockSpec((None, ROW, D), lambda r: (r, 0, 0)),
        ],
        out_specs=pl.BlockSpec((N_EXP, None, CAP, D), lambda r: (0, r, 0, 0)),
        scratch_shapes=[pltpu.VMEM((N_EXP * CAP, ROW), BF16)],
        compiler_params=pltpu.CompilerParams(
            dimension_semantics=("arbitrary",), vmem_limit_bytes=VMEM_LIMIT),
        name="gather",
    )(slot, h2)


EXP_ROWS = 4


def _expert_kernel(l_ref, slot_ref, aff_ref, xs_ref, wg_ref, wu_ref, wd_ref, ye_ref, wg_s, wu_s, wd_s):
    e = pl.program_id(0)
    wg_s[...] = wg_ref[...].astype(BF16)
    wu_s[...] = wu_ref[...].astype(BF16)
    wd_s[...] = wd_ref[...].astype(BF16)
    j = lax.broadcasted_iota(I32, (CAP, ROW), 0)

    def gates(c):
        c["gate"] = []
        for k in range(EXP_ROWS):
            row = (c["r0"] + k) * N_EXP + e
            hit = slot_ref[pl.ds(row, 1), :] == j
            c["gate"].append(jnp.sum(jnp.where(hit, aff_ref[pl.ds(row, 1), :], 0.0), axis=1, keepdims=True))

    def up(c):
        xs = xs_ref[c["r0"]:c["r0"] + EXP_ROWS].reshape(EXP_ROWS * CAP, D)
        c["a"], c["u"] = _dot(xs, wg_s[...]), _dot(xs, wu_s[...])

    def act(c):
        c["hid"] = (_silu(c.pop("a")) * c.pop("u")).astype(BF16)

    def down(c):
        c["ye"] = _dot(c.pop("hid"), wd_s[...])

    def out(c):
        ye, gate = c.pop("ye"), c.pop("gate")
        for k in range(EXP_ROWS):
            ye_ref[c["r0"] + k] = (ye[k * CAP:(k + 1) * CAP, :] * gate[k]).astype(BF16)

    _skewed([gates, up, act, down, out], [dict(r0=g * EXP_ROWS) for g in range(N_ROWS // EXP_ROWS)])


def _expert_call(l, slot, aff, xs, w_gate, w_up, w_down):
    rows = N_ROWS * N_EXP
    return pl.pallas_call(
        _expert_kernel,
        out_shape=jax.ShapeDtypeStruct((N_EXP, N_ROWS, CAP, D), BF16),
        grid_spec=pltpu.PrefetchScalarGridSpec(
            num_scalar_prefetch=1,
            grid=(N_EXP,),
            in_specs=[
                pl.BlockSpec((rows, ROW), lambda e, l: (0, 0), pipeline_mode=pl.Buffered(1)),
                pl.BlockSpec((rows, ROW), lambda e, l: (0, 0), pipeline_mode=pl.Buffered(1)),
                pl.BlockSpec((None, N_ROWS, CAP, D), lambda e, l: (e, 0, 0, 0)),
                pl.BlockSpec((None, None, D, FF), lambda e, l: (l[0], e, 0, 0)),
                pl.BlockSpec((None, None, D, FF), lambda e, l: (l[0], e, 0, 0)),
                pl.BlockSpec((None, None, FF, D), lambda e, l: (l[0], e, 0, 0)),
            ],
            out_specs=pl.BlockSpec((None, N_ROWS, CAP, D), lambda e, l: (e, 0, 0, 0)),
            scratch_shapes=[pltpu.VMEM((D, FF), BF16), pltpu.VMEM((D, FF), BF16), pltpu.VMEM((FF, D), BF16)]),
        compiler_params=pltpu.CompilerParams(
            dimension_semantics=("arbitrary",), vmem_limit_bytes=VMEM_LIMIT),
        name="experts",
    )(_layer_arg(l), slot, aff, xs, w_gate, w_up, w_down)


COMB_TC = 256


def _make_combine_kernel(final):
    def kern(*refs):
        if final:
            _, slot_ref, ye_ref, xmid_ref, g2_ref, fg_ref, o_ref = refs
        else:
            _, slot_ref, ye_ref, xmid_ref, g2_ref, o_ref = refs
        ye = ye_ref[...].reshape(N_EXP * CAP, D)
        g2 = g2_ref[...]
        j = lax.broadcasted_iota(I32, (CAP, COMB_TC), 0)
        for t in range(ROW // COMB_TC):
            cols = slice(t * COMB_TC, (t + 1) * COMB_TC)
            onehot = jnp.concatenate(
                [jnp.where(slot_ref[e:e + 1, cols] == j, 1.0, 0.0).astype(BF16) for e in range(N_EXP)],
                axis=0)
            moe = lax.dot_general(onehot, ye, TN, preferred_element_type=F32)
            xn = xmid_ref[cols, :] + g2 * moe
            if final:
                xn = _rms(xn, fg_ref[...])
            o_ref[cols, :] = xn

    return kern


def _combine_call(l, slot, ye, xmid, mods, final_g, row0, n_rows, mod_row):
    final = final_g is not None
    in_specs = [
        pl.BlockSpec((N_EXP, ROW), lambda r, l: (row0 + r, 0)),
        pl.BlockSpec((N_EXP, None, CAP, D), lambda r, l: (0, row0 + r, 0, 0)),
        pl.BlockSpec((None, ROW, D), lambda r, l: (row0 + r, 0, 0)),
        pl.BlockSpec((None, None, 1, D), lambda r, l: (l[0], mod_row(r), 0, 5)),
    ]
    args = [slot, ye, xmid, mods]
    if final:
        in_specs.append(pl.BlockSpec((1, D), lambda r, l: (0, 0)))
        args.append(final_g)
    return pl.pallas_call(
        _make_combine_kernel(final),
        out_shape=jax.ShapeDtypeStruct((n_rows, ROW, D), F32),
        grid_spec=pltpu.PrefetchScalarGridSpec(
            num_scalar_prefetch=1,
            grid=(n_rows,),
            in_specs=in_specs,
            out_specs=pl.BlockSpec((None, ROW, D), lambda r, l: (r, 0, 0))),
        compiler_params=pltpu.CompilerParams(
            dimension_semantics=("arbitrary",), vmem_limit_bytes=VMEM_LIMIT),
        name="combine",
    )(_layer_arg(l), *args)


def _rope_rot_cols(w):
    q = QK_ROPE // 4
    return jnp.concatenate([-w[..., q:2 * q], w[..., 0:q], -w[..., 3 * q:4 * q], w[..., 2 * q:3 * q]], axis=-1)


def _pad_lanes(w):
    return jnp.pad(w, [(0, 0)] * (w.ndim - 1) + [(0, LANES - w.shape[-1])])


def _prep_weights(norm1_g, norm2_g, w_in, q_norm_g, kv_norm_g, w_uq, w_ukv, conv_w, spatial_w,
                  spatial_b, w_out, w_router):
    o_pe, o_conv = Q_RANK + KV_RANK, Q_RANK + KV_RANK + QK_ROPE
    w_kpe = w_in[..., o_pe:o_conv]
    w_in_p = jnp.concatenate(
        [w_in[..., :o_pe], _pad_lanes(w_kpe), _pad_lanes(_rope_rot_cols(w_kpe)), w_in[..., o_conv:]],
        axis=-1).astype(BF16)
    uq = w_uq.reshape(DEPTH, Q_RANK, HEADS, QK_NOPE + QK_ROPE)
    uq_pe = uq[..., QK_NOPE:]
    w_uq_p = jnp.concatenate(
        [uq[..., :QK_NOPE].reshape(DEPTH, Q_RANK, HEADS * QK_NOPE),
         _pad_lanes(uq_pe).reshape(DEPTH, Q_RANK, HEADS * LANES),
         _pad_lanes(_rope_rot_cols(uq_pe)).reshape(DEPTH, Q_RANK, HEADS * LANES)], axis=-1).astype(BF16)
    sb = jnp.repeat(jnp.swapaxes(spatial_b, 1, 2), CMLP_W // CMLP_G, axis=-1)
    return dict(
        norm1_g=norm1_g.reshape(DEPTH, 1, D), norm2_g=norm2_g.reshape(DEPTH, 1, D),
        w_in=w_in_p, q_norm_g=q_norm_g.reshape(DEPTH, 1, Q_RANK),
        kv_norm_g=kv_norm_g.reshape(DEPTH, 1, KV_RANK), w_uq=w_uq_p, w_ukv=w_ukv.astype(BF16),
        conv_w=conv_w, spatial_w=spatial_w.reshape(DEPTH, CMLP_G * CHUNK, CHUNK).astype(BF16),
        spatial_b=sb, w_out=w_out.astype(BF16), w_router_t=jnp.swapaxes(w_router, 1, 2))


def _rope_tables(n):
    rows = n // GRID_W
    row = jnp.repeat(jnp.arange(rows), GRID_W).astype(F32)
    col = jnp.tile(jnp.arange(GRID_W), rows).astype(F32)
    n_freq = QK_ROPE // 4
    inv = ROPE_BASE ** (-jnp.arange(n_freq, dtype=F32) / n_freq)
    ang_r, ang_c = row[:, None] * inv, col[:, None] * inv
    ang = jnp.concatenate([ang_r, ang_r, ang_c, ang_c], axis=-1)
    return _pad_lanes(jnp.cos(ang)), _pad_lanes(jnp.sin(ang))


def kernel(x_prompt, x_sample, cache_ckv, cache_krope, c, c_ctx, w_ada, b_ada, norm1_g, norm2_g, w_in,
           q_norm_g, kv_norm_g, w_uq, w_ukv, conv_w, spatial_w, spatial_b, w_out, w_router, w_gate,
           w_up, w_down, final_norm_g):
    wts = _prep_weights(norm1_g, norm2_g, w_in, q_norm_g, kv_norm_g, w_uq, w_ukv, conv_w, spatial_w,
                        spatial_b, w_out, w_router)
    rope_tabs = _rope_tables(DEC_SEQ)
    cond = jnp.concatenate(
        [c, c_ctx[None, :], jnp.zeros((MOD_ROWS - DEC_BATCH - 1, D), F32)], axis=0)
    mods = _ada_call(cond, w_ada, b_ada).reshape(DEPTH, MOD_ROWS, 1, 6 * D)
    final_g = final_norm_g.reshape(1, D)
    row_mod = lambda r: jnp.where(r < P_ROWS, CTX_MOD_ROW, r - P_ROWS)

    x = jnp.concatenate([x_prompt.reshape(P_ROWS, ROW, D), x_sample], axis=0)
    ckv_list, kpe_list = [], []
    for l in range(DEPTH):
        xmid, h2, aff, ckv, kpe = _mixer_call(l, x, mods, wts, rope_tabs, (cache_ckv, cache_krope))
        ckv_list.append(ckv.reshape(BATCH, SEQ, KV_RANK))
        kpe_list.append(kpe.reshape(BATCH, SEQ, QK_ROPE))
        aff = aff.reshape(N_ROWS * N_EXP, ROW)
        slot = _route_call(aff)
        ye = _expert_call(l, slot, aff, _gather_call(slot, h2), w_gate, w_up, w_down)
        if l < DEPTH - 1:
            x = _combine_call(l, slot, ye, xmid, mods, None, 0, N_ROWS, row_mod)
        else:
            y_prompt = _combine_call(l, slot, ye, xmid, mods, final_g, 0, P_ROWS, lambda r: CTX_MOD_ROW)
            y_sample = _combine_call(l, slot, ye, xmid, mods, final_g, P_ROWS, DEC_BATCH, lambda r: r)
    return (y_prompt.reshape(BATCH, SEQ, D), y_sample,
            jnp.stack(ckv_list, axis=1), jnp.stack(kpe_list, axis=1))
```

```python
import jax
import jax.numpy as jnp
from jax import lax
from jax.experimental import pallas as pl
from jax.experimental.pallas import tpu as pltpu

F32 = jnp.float32
BF16 = jnp.bfloat16
I32 = jnp.int32

D = 1024
BATCH = 16
SEQ = 256
DEPTH = 4
DEC_BATCH = 8
DEC_SEQ = 1024
PAST_LEN = 512
GRID_W = 64
HEADS = 4
QK_NOPE = 128
QK_ROPE = 64
V_HEAD = 128
Q_RANK = 256
KV_RANK = 128
CONV_W = 256
CMLP_W = 256
CMLP_G = 4
CHUNK = 128
N_EXP = 16
EC_FACTOR = 2
FF = D // 2
ROPE_BASE = 10000.0
EPS = 1e-6
LOG2_E = 1.4426950408889634

LANES = 128
ROW = 1024
N_ROWS = (BATCH * SEQ + DEC_BATCH * DEC_SEQ) // ROW
P_ROWS = BATCH * SEQ // ROW
P_PER_ROW = ROW // SEQ
CAP = EC_FACTOR * ROW // N_EXP
CAP_P = EC_FACTOR * SEQ // N_EXP
CTX_MOD_ROW = DEC_BATCH
MOD_ROWS = 16
QK_PAD = QK_NOPE + LANES

C_Q, C_KV, C_KPE, C_CONV, C_CMLP, C_END = 0, 256, 384, 512, 1280, 1792
U_NOPE, U_PE, U_END = 0, HEADS * QK_NOPE, HEADS * (QK_NOPE + LANES)
assert 2 * QK_ROPE == LANES
TB = 256
TQ = 256
AT_BLOCKS = 1
VMEM_LIMIT = 56 * 1024 * 1024

NT = (((1,), (1,)), ((), ()))
TN = (((0,), (0,)), ((), ()))


def _rms(x, g):
    return x * lax.rsqrt(jnp.mean(x * x, axis=-1, keepdims=True) + EPS) * g


def _silu(x):
    return x / (1.0 + jnp.exp(-x))


def _dot(a, b):
    return jnp.dot(a, b, preferred_element_type=F32)


def _dot_nt(a, b):
    return lax.dot_general(a, b, NT, preferred_element_type=F32)


def _skewed(stages, items):
    n = len(stages)
    for t in range(len(items) + n - 1):
        for s in range(n - 1, -1, -1):
            if 0 <= t - s < len(items):
                stages[s](items[t - s])


def _ada_kernel(c_ref, w_ref, b_ref, o_ref):
    s = _silu(c_ref[...]).astype(BF16)
    o_ref[...] = _dot(s, w_ref[...].astype(BF16)) + b_ref[...]


def _ada_call(cond, w_ada, b_ada):
    nc = 6
    return pl.pallas_call(
        _ada_kernel,
        out_shape=jax.ShapeDtypeStruct((DEPTH, MOD_ROWS, 6 * D), F32),
        grid=(DEPTH, nc),
        in_specs=[
            pl.BlockSpec((MOD_ROWS, D), lambda l, j: (0, 0)),
            pl.BlockSpec((None, D, D), lambda l, j: (l, 0, j)),
            pl.BlockSpec((None, 1, D), lambda l, j: (l, 0, j)),
        ],
        out_specs=pl.BlockSpec((None, MOD_ROWS, D), lambda l, j: (l, 0, j)),
        compiler_params=pltpu.CompilerParams(
            dimension_semantics=("arbitrary", "arbitrary"), vmem_limit_bytes=VMEM_LIMIT),
        name="ada_mod",
    )(cond, w_ada, b_ada.reshape(DEPTH, 1, 6 * D))


def _mixer_kernel(l_ref, x_ref, mod_ref, g1n_ref, g2n_ref, win_ref, qg_ref, kvg_ref, wuq_ref, wukv_ref,
                  cw_ref, sw_ref, sb_ref, wout_ref, wr_ref, cs_ref, cckv_ref, ckr_ref,
                  xmid_ref, h2_ref, aff_ref, ckv_out, kpe_out,
                  qf_s, kf_s, v_s, mix_s, gb_s, z_s):
    qk_scale = float(QK_NOPE + QK_ROPE) ** -0.5 * LOG2_E
    nblk = ROW // TB

    def run(ctx):
        n_past = 0 if ctx else PAST_LEN
        seg = SEQ if ctx else ROW
        mod = mod_ref[...]
        sh1, sc1, g1 = mod[:, 0:D], mod[:, D:2 * D], mod[:, 2 * D:3 * D]
        sh2, sc2, g2 = mod[:, 3 * D:4 * D], mod[:, 4 * D:5 * D], mod[:, 5 * D:6 * D]

        def store_kv(row0, rows, kv, kpb):
            for h in range(HEADS):
                c0 = h * (QK_NOPE + V_HEAD)
                kf_s[h, pl.ds(row0, rows), 0:QK_NOPE] = kv[:, c0:c0 + QK_NOPE].astype(BF16)
                kf_s[h, pl.ds(row0, rows), QK_NOPE:QK_PAD] = kpb
                v_s[h, pl.ds(row0, rows), :] = kv[:, c0 + QK_NOPE:c0 + QK_NOPE + V_HEAD].astype(BF16)

        if not ctx:
            kr = ckr_ref[...]
            store_kv(0, PAST_LEN, _dot(cckv_ref[...].astype(BF16), wukv_ref[...]),
                     jnp.concatenate([kr, jnp.zeros_like(kr)], axis=1).astype(BF16))

        grp_shift = (CMLP_W // CMLP_G).bit_length() - 1
        lane_grp = lax.shift_right_logical(lax.broadcasted_iota(I32, (CHUNK, CMLP_W), 1), grp_shift)

        def p1_norm(c):
            x = x_ref[pl.ds(c["r0"], TB), :]
            c["hb"] = (_rms(x, g1n_ref[...]) * (1.0 + sc1) + sh1).astype(BF16)

        def p1_proj(c):
            c["pa"] = _dot(c["hb"], win_ref[:, C_Q:C_CONV])
            c["cv"] = _dot(c["hb"], win_ref[:, C_CONV:C_CMLP])
            c["cm"] = _dot(c.pop("hb"), win_ref[:, C_CMLP:C_END])

        def p1_mid(c):
            r0, pa, cv, cm = c["r0"], c.pop("pa"), c.pop("cv"), c.pop("cm")
            kp = pa[:, C_KPE:C_CONV]
            c["cqn"] = _rms(pa[:, C_Q:C_KV], qg_ref[...]).astype(BF16)
            ckv_n = _rms(pa[:, C_KV:C_KPE], kvg_ref[...])
            if ctx:
                ckv_out[pl.ds(r0, TB), :] = ckv_n
                kpe_out[pl.ds(r0, TB), :] = kp[:, 0:QK_ROPE]
                kpe = kp
            else:
                y = kp * cs_ref[pl.ds(r0, TB), :]
                kpe = y + pltpu.roll(y, QK_ROPE, 1)
            c["ckv_b"], c["kpe_b"] = ckv_n.astype(BF16), kpe.astype(BF16)
            gb_s[pl.ds(r0, TB), :] = cv[:, 0:CONV_W]
            z_s[pl.ds(r0, TB), :] = cv[:, CONV_W:2 * CONV_W] * cv[:, 2 * CONV_W:3 * CONV_W]
            c["u"], c["vvb"] = cm[:, 0:CMLP_W], cm[:, CMLP_W:2 * CMLP_W].astype(BF16)

        def p1_up(c):
            c["qa"] = _dot(c.pop("cqn"), wuq_ref[...])
            c["kv"] = _dot(c.pop("ckv_b"), wukv_ref[...])
            vvb = c.pop("vvb")
            c["r"] = [_dot(sw_ref[...], vvb[k * CHUNK:(k + 1) * CHUNK, :]) for k in range(TB // CHUNK)]

        def p1_out(c):
            r0, qa, u = c["r0"], c.pop("qa"), c.pop("u")
            rope_lanes = lax.broadcasted_iota(I32, (TB, LANES), 1) < QK_ROPE
            for h in range(HEADS):
                qn = qa[:, U_NOPE + h * QK_NOPE:U_NOPE + (h + 1) * QK_NOPE]
                qp = qa[:, U_PE + h * LANES:U_PE + (h + 1) * LANES]
                if not ctx:
                    y = qp * cs_ref[pl.ds(r0, TB), :]
                    qp = y + pltpu.roll(y, QK_ROPE, 1)
                qp = jnp.where(rope_lanes, qp, 0.0)
                qf_s[h, pl.ds(r0, TB), 0:QK_NOPE] = (qn * qk_scale).astype(BF16)
                qf_s[h, pl.ds(r0, TB), QK_NOPE:QK_PAD] = (qp * qk_scale).astype(BF16)
            store_kv(n_past + r0, TB, c.pop("kv"), c.pop("kpe_b"))
            for k, r in enumerate(c.pop("r")):
                mixed = jnp.where(
                    lane_grp == 0, r[0:CHUNK],
                    jnp.where(lane_grp == 1, r[CHUNK:2 * CHUNK],
                              jnp.where(lane_grp == 2, r[2 * CHUNK:3 * CHUNK], r[3 * CHUNK:4 * CHUNK])))
                out = u[k * CHUNK:(k + 1) * CHUNK, :] * (mixed + sb_ref[...])
                mix_s[pl.ds(r0 + k * CHUNK, CHUNK), HEADS * V_HEAD + CONV_W:D] = out.astype(BF16)

        _skewed([p1_norm, p1_proj, p1_mid, p1_up, p1_out], [dict(r0=i * TB) for i in range(nblk)])

        z = z_s[...]
        pos = lax.broadcasted_iota(I32, (ROW, CONV_W), 0) & (seg - 1)
        zm = jnp.where(pos == 0, 0.0, pltpu.roll(z, 1, 0))
        zp = jnp.where(pos == seg - 1, 0.0, pltpu.roll(z, ROW - 1, 0))
        cw = cw_ref[...]
        conv = zm * cw[0:1, :] + z * cw[1:2, :] + zp * cw[2:3, :]
        mix_s[:, HEADS * V_HEAD:HEADS * V_HEAD + CONV_W] = (gb_s[...] * conv).astype(BF16)

        def at_qk(c):
            keys = kf_s[c["h"], pl.ds(c["k0"], seg + n_past), :]
            c["s"] = _dot_nt(qf_s[c["h"], pl.ds(c["r0"], TQ), :], keys)

        def at_max(c):
            c["m"] = jnp.max(c["s"], axis=-1, keepdims=True)

        def at_exp(c):
            e = jnp.exp2(c.pop("s") - c.pop("m"))
            c["l"] = jnp.sum(e, axis=-1, keepdims=True)
            c["p"] = e.astype(BF16)

        def at_pv(c):
            c["o"] = _dot(c.pop("p"), v_s[c["h"], pl.ds(c["k0"], seg + n_past), :])

        def at_out(c):
            o = c.pop("o") / c.pop("l")
            mix_s[pl.ds(c["r0"], TQ), c["h"] * V_HEAD:(c["h"] + 1) * V_HEAD] = o.astype(BF16)

        at_stages = [at_qk, lambda c: (at_max(c), at_exp(c)), lambda c: (at_pv(c), at_out(c))]
        if ctx:
            _skewed(at_stages, [dict(r0=b * TQ, k0=b * TQ, h=h) for b in range(ROW // TQ) for h in range(HEADS)])
        else:
            def at_block(b, carry):
                r0 = pl.multiple_of(b * (AT_BLOCKS * TQ), AT_BLOCKS * TQ)
                _skewed(at_stages, [dict(r0=r0 + k * TQ, k0=0, h=h) for k in range(AT_BLOCKS) for h in range(HEADS)])
                return carry

            lax.fori_loop(0, ROW // (AT_BLOCKS * TQ), at_block, 0)

        wr = wr_ref[...]
        wr_hi = wr.astype(BF16)
        wr_hl = jnp.concatenate([wr_hi, (wr - wr_hi.astype(F32)).astype(BF16)], axis=0)

        def p2_proj(c):
            c["mo"] = _dot(mix_s[pl.ds(c["r0"], TB), :], wout_ref[...])

        def p2_norm(c):
            r0 = c["r0"]
            xm = x_ref[pl.ds(r0, TB), :] + g1 * c.pop("mo")
            xmid_ref[pl.ds(r0, TB), :] = xm
            h2 = _rms(xm, g2n_ref[...]) * (1.0 + sc2) + sh2
            h2_hi = h2.astype(BF16)
            h2_ref[pl.ds(r0, TB), :] = h2_hi
            c["hi"], c["lo"] = h2_hi, (h2 - h2_hi.astype(F32)).astype(BF16)

        def p2_route(c):
            la = _dot_nt(wr_hl, c["hi"])
            c["lg"] = la[0:N_EXP] + la[N_EXP:2 * N_EXP] + _dot_nt(wr_hi, c.pop("lo"))
            c.pop("hi")

        def p2_aff(c):
            lg = c.pop("lg")
            e = jnp.exp(lg - jnp.max(lg, axis=0, keepdims=True))
            aff_ref[:, pl.ds(c["r0"], TB)] = e / jnp.sum(e, axis=0, keepdims=True)

        _skewed([p2_proj, p2_norm, p2_route, p2_aff], [dict(r0=i * TB) for i in range(nblk)])

    is_ctx = pl.program_id(0) < P_ROWS

    @pl.when(is_ctx)
    def _():
        run(True)

    @pl.when(jnp.logical_not(is_ctx))
    def _():
        run(False)


def _layer_spec(shape):
    nd = len(shape)
    return pl.BlockSpec((None,) + shape, lambda r, l: (l[0],) + (0,) * nd, pipeline_mode=pl.Buffered(1))


def _layer_arg(l):
    return jnp.full((1,), l, I32)


def _mixer_call(l, x, mods, wts, rope_tab, caches):
    lk = PAST_LEN + ROW
    lat = lambda r: jnp.maximum(r - P_ROWS, 0)
    ctx = lambda r: jnp.minimum(r, P_ROWS - 1)
    in_specs = [
        pl.BlockSpec((None, ROW, D), lambda r, l: (r, 0, 0)),
        pl.BlockSpec((None, None, 1, 6 * D),
                     lambda r, l: (l[0], jnp.where(r < P_ROWS, CTX_MOD_ROW, r - P_ROWS), 0, 0)),
        _layer_spec((1, D)), _layer_spec((1, D)),
        _layer_spec((D, C_END)),
        _layer_spec((1, Q_RANK)), _layer_spec((1, KV_RANK)),
        _layer_spec((Q_RANK, U_END)),
        _layer_spec((KV_RANK, HEADS * (QK_NOPE + V_HEAD))),
        _layer_spec((3, CONV_W)),
        _layer_spec((CMLP_G * CHUNK, CHUNK)),
        _layer_spec((CHUNK, CMLP_W)),
        _layer_spec((D, D)),
        _layer_spec((N_EXP, D)),
        pl.BlockSpec((ROW, LANES), lambda r, l: (0, 0), pipeline_mode=pl.Buffered(1)),
        pl.BlockSpec((None, None, PAST_LEN, KV_RANK), lambda r, l: (lat(r), l[0], 0, 0)),
        pl.BlockSpec((None, None, PAST_LEN, QK_ROPE), lambda r, l: (lat(r), l[0], 0, 0)),
    ]
    args = [x, mods, wts["norm1_g"], wts["norm2_g"], wts["w_in"], wts["q_norm_g"], wts["kv_norm_g"],
            wts["w_uq"], wts["w_ukv"], wts["conv_w"], wts["spatial_w"], wts["spatial_b"],
            wts["w_out"], wts["w_router_t"], rope_tab, caches[0], caches[1]]
    out_shape = [jax.ShapeDtypeStruct((N_ROWS, ROW, D), F32),
                 jax.ShapeDtypeStruct((N_ROWS, ROW, D), BF16),
                 jax.ShapeDtypeStruct((N_ROWS, N_EXP, ROW), F32),
                 jax.ShapeDtypeStruct((P_ROWS, ROW, KV_RANK), F32),
                 jax.ShapeDtypeStruct((P_ROWS, ROW, QK_ROPE), F32)]
    out_specs = [pl.BlockSpec((None, ROW, D), lambda r, l: (r, 0, 0)),
                 pl.BlockSpec((None, ROW, D), lambda r, l: (r, 0, 0)),
                 pl.BlockSpec((None, N_EXP, ROW), lambda r, l: (r, 0, 0)),
                 pl.BlockSpec((None, ROW, KV_RANK), lambda r, l: (ctx(r), 0, 0)),
                 pl.BlockSpec((None, ROW, QK_ROPE), lambda r, l: (ctx(r), 0, 0))]
    return pl.pallas_call(
        _mixer_kernel,
        out_shape=out_shape,
        grid_spec=pltpu.PrefetchScalarGridSpec(
            num_scalar_prefetch=1,
            grid=(N_ROWS,),
            in_specs=in_specs,
            out_specs=out_specs,
            scratch_shapes=[
                pltpu.VMEM((HEADS, ROW, QK_PAD), BF16),
                pltpu.VMEM((HEADS, lk, QK_PAD), BF16),
                pltpu.VMEM((HEADS, lk, V_HEAD), BF16),
                pltpu.VMEM((ROW, D), BF16),
                pltpu.VMEM((ROW, CONV_W), F32),
                pltpu.VMEM((ROW, CONV_W), F32),
            ]),
        compiler_params=pltpu.CompilerParams(
            dimension_semantics=("arbitrary",), vmem_limit_bytes=VMEM_LIMIT),
        name="mixer",
    )(_layer_arg(l), *args)


def _cap_thresholds(affs, caps):
    def bit_step(i, ts):
        bit = jnp.left_shift(jnp.int32(1), 30 - i)
        out = []
        for aff, cap, t in zip(affs, caps, ts):
            cand = t | bit
            cnt = jnp.sum(jnp.where(aff >= pltpu.bitcast(cand, F32), 1.0, 0.0), axis=1, keepdims=True)
            out.append(jnp.where(cnt >= float(cap), cand, t))
        return tuple(out)

    ts = lax.fori_loop(0, 31, bit_step, tuple(jnp.zeros((a.shape[0], 1), I32) for a in affs))
    return [pltpu.bitcast(t, F32) for t in ts]


def _select_slots(aff, thr, cap, tri):
    capf = float(cap)
    gt = aff > thr
    eq = aff == thr
    n_gt = jnp.sum(jnp.where(gt, 1.0, 0.0), axis=1, keepdims=True)
    eq_rank = _dot(jnp.where(eq, 1.0, 0.0).astype(BF16), tri)
    sel = jnp.where(gt, 1.0, jnp.where(eq, jnp.where(eq_rank < capf - n_gt, 1.0, 0.0), 0.0))
    pos = _dot(sel.astype(BF16), tri)
    return jnp.where(sel > 0.5, pos.astype(I32), -1)


def _route_kernel(aff_ref, slot_ref, tri_s):
    r = lax.broadcasted_iota(I32, (ROW, ROW), 0)
    c = lax.broadcasted_iota(I32, (ROW, ROW), 1)
    tri_s[...] = jnp.where(r < c, 1.0, 0.0).astype(BF16)
    np_ = P_ROWS * N_EXP
    affs = [aff_ref[0:np_, s * SEQ:(s + 1) * SEQ] for s in range(P_PER_ROW)] + [aff_ref[np_:, :]]
    caps = [CAP_P] * P_PER_ROW + [CAP]
    thrs = _cap_thresholds(affs, caps)
    for s in range(P_PER_ROW):
        sl = _select_slots(affs[s], thrs[s], CAP_P, tri_s[0:SEQ, 0:SEQ])
        slot_ref[0:np_, s * SEQ:(s + 1) * SEQ] = jnp.where(sl >= 0, sl + s * CAP_P, -1)
    slot_ref[np_:, :] = _select_slots(affs[-1], thrs[-1], CAP, tri_s[...])


def _route_call(aff):
    rows = N_ROWS * N_EXP
    return pl.pallas_call(
        _route_kernel,
        out_shape=jax.ShapeDtypeStruct((rows, ROW), I32),
        scratch_shapes=[pltpu.VMEM((ROW, ROW), BF16)],
        compiler_params=pltpu.CompilerParams(vmem_limit_bytes=VMEM_LIMIT),
        name="route",
    )(aff)


GATHER_NC = 256


def _gather_kernel(slot_ref, h2_ref, xs_ref, hot_s):
    @pl.when(pl.program_id(0) >= P_ROWS)
    def _():
        j = lax.broadcasted_iota(I32, (CAP, ROW), 0)
        for e in range(N_EXP):
            hot_s[e * CAP:(e + 1) * CAP, :] = jnp.where(slot_ref[e:e + 1, :] == j, 1.0, 0.0).astype(BF16)
        for c in range(D // GATHER_NC):
            xs = _dot(hot_s[...], h2_ref[:, c * GATHER_NC:(c + 1) * GATHER_NC]).astype(BF16)
            for e in range(N_EXP):
                xs_ref[e, :, c * GATHER_NC:(c + 1) * GATHER_NC] = xs[e * CAP:(e + 1) * CAP, :]

    @pl.when(pl.program_id(0) < P_ROWS)
    def _():
        j = lax.broadcasted_iota(I32, (CAP_P, SEQ), 0)
        for s in range(P_PER_ROW):
            toks = slice(s * SEQ, (s + 1) * SEQ)
            for e in range(N_EXP):
                hot_s[e * CAP_P:(e + 1) * CAP_P, 0:SEQ] = jnp.where(
                    slot_ref[e:e + 1, toks] == j + s * CAP_P, 1.0, 0.0).astype(BF16)
            xs = _dot(hot_s[0:N_EXP * CAP_P, 0:SEQ], h2_ref[toks, :]).astype(BF16)
            for e in range(N_EXP):
                xs_ref[e, s * CAP_P:(s + 1) * CAP_P, :] = xs[e * CAP_P:(e + 1) * CAP_P, :]


def _gather_call(slot, h2):
    return pl.pallas_call(
        _gather_kernel,
        out_shape=jax.ShapeDtypeStruct((N_EXP, N_ROWS, CAP, D), BF16),
        grid=(N_ROWS,),
        in_specs=[
            pl.BlockSpec((N_EXP, ROW), lambda r: (r, 0)),
            pl.BlockSpec((None, ROW, D), lambda r: (r, 0, 0)),
        ],
        out_specs=pl.BlockSpec((N_EXP, None, CAP, D), lambda r: (0, r, 0, 0)),
        scratch_shapes=[pltpu.VMEM((N_EXP * CAP, ROW), BF16)],
        compiler_params=pltpu.CompilerParams(
            dimension_semantics=("arbitrary",), vmem_limit_bytes=VMEM_LIMIT),
        name="gather",
    )(slot, h2)


EXP_ROWS = 4


def _expert_kernel(l_ref, slot_ref, aff_ref, xs_ref, wg_ref, wu_ref, wd_ref, ye_ref, wg_s, wu_s, wd_s):
    e = pl.program_id(0)
    wg_s[...] = wg_ref[...].astype(BF16)
    wu_s[...] = wu_ref[...].astype(BF16)
    wd_s[...] = wd_ref[...].astype(BF16)
    j = lax.broadcasted_iota(I32, (CAP, ROW), 0)

    def gates(c):
        c["gate"] = []
        for k in range(EXP_ROWS):
            row = (c["r0"] + k) * N_EXP + e
            hit = slot_ref[pl.ds(row, 1), :] == j
            c["gate"].append(jnp.sum(jnp.where(hit, aff_ref[pl.ds(row, 1), :], 0.0), axis=1, keepdims=True))

    def up(c):
        xs = xs_ref[c["r0"]:c["r0"] + EXP_ROWS].reshape(EXP_ROWS * CAP, D)
        c["a"], c["u"] = _dot(xs, wg_s[...]), _dot(xs, wu_s[...])

    def act(c):
        c["hid"] = (_silu(c.pop("a")) * c.pop("u")).astype(BF16)

    def down(c):
        c["ye"] = _dot(c.pop("hid"), wd_s[...])

    def out(c):
        ye, gate = c.pop("ye"), c.pop("gate")
        for k in range(EXP_ROWS):
            ye_ref[c["r0"] + k] = (ye[k * CAP:(k + 1) * CAP, :] * gate[k]).astype(BF16)

    _skewed([gates, up, act, down, out], [dict(r0=g * EXP_ROWS) for g in range(N_ROWS // EXP_ROWS)])


def _expert_call(l, slot, aff, xs, w_gate, w_up, w_down):
    rows = N_ROWS * N_EXP
    return pl.pallas_call(
        _expert_kernel,
        out_shape=jax.ShapeDtypeStruct((N_EXP, N_ROWS, CAP, D), BF16),
        grid_spec=pltpu.PrefetchScalarGridSpec(
            num_scalar_prefetch=1,
            grid=(N_EXP,),
            in_specs=[
                pl.BlockSpec((rows, ROW), lambda e, l: (0, 0), pipeline_mode=pl.Buffered(1)),
                pl.BlockSpec((rows, ROW), lambda e, l: (0, 0), pipeline_mode=pl.Buffered(1)),
                pl.BlockSpec((None, N_ROWS, CAP, D), lambda e, l: (e, 0, 0, 0)),
                pl.BlockSpec((None, None, D, FF), lambda e, l: (l[0], e, 0, 0)),
                pl.BlockSpec((None, None, D, FF), lambda e, l: (l[0], e, 0, 0)),
                pl.BlockSpec((None, None, FF, D), lambda e, l: (l[0], e, 0, 0)),
            ],
            out_specs=pl.BlockSpec((None, N_ROWS, CAP, D), lambda e, l: (e, 0, 0, 0)),
            scratch_shapes=[pltpu.VMEM((D, FF), BF16), pltpu.VMEM((D, FF), BF16), pltpu.VMEM((FF, D), BF16)]),
        compiler_params=pltpu.CompilerParams(
            dimension_semantics=("arbitrary",), vmem_limit_bytes=VMEM_LIMIT),
        name="experts",
    )(_layer_arg(l), slot, aff, xs, w_gate, w_up, w_down)


COMB_TC = 256


def _make_combine_kernel(final, first_row):
    def kern(*refs):
        if final:
            _, slot_ref, ye_ref, xmid_ref, g2_ref, fg_ref, o_ref = refs
        else:
            _, slot_ref, ye_ref, xmid_ref, g2_ref, o_ref = refs
        g2 = g2_ref[...]

        def run(ctx):
            nslot = CAP_P if ctx else CAP
            j = lax.broadcasted_iota(I32, (nslot, COMB_TC), 0)
            if not ctx:
                ye = ye_ref[...].reshape(N_EXP * CAP, D)
            for t in range(ROW // COMB_TC):
                cols = slice(t * COMB_TC, (t + 1) * COMB_TC)
                s0 = t * CAP_P if ctx else 0
                onehot = jnp.concatenate(
                    [jnp.where(slot_ref[e:e + 1, cols] == j + s0, 1.0, 0.0).astype(BF16) for e in range(N_EXP)],
                    axis=0)
                if ctx:
                    ye = jnp.concatenate([ye_ref[e, s0:s0 + CAP_P, :] for e in range(N_EXP)], axis=0)
                moe = lax.dot_general(onehot, ye, TN, preferred_element_type=F32)
                xn = xmid_ref[cols, :] + g2 * moe
                if final:
                    xn = _rms(xn, fg_ref[...])
                o_ref[cols, :] = xn

        is_ctx = first_row + pl.program_id(0) < P_ROWS

        @pl.when(is_ctx)
        def _():
            run(True)

        @pl.when(jnp.logical_not(is_ctx))
        def _():
            run(False)

    return kern


def _combine_call(l, slot, ye, xmid, mods, final_g, row0, n_rows, mod_row):
    final = final_g is not None
    in_specs = [
        pl.BlockSpec((N_EXP, ROW), lambda r, l: (row0 + r, 0)),
        pl.BlockSpec((N_EXP, None, CAP, D), lambda r, l: (0, row0 + r, 0, 0)),
        pl.BlockSpec((None, ROW, D), lambda r, l: (row0 + r, 0, 0)),
        pl.BlockSpec((None, None, 1, D), lambda r, l: (l[0], mod_row(r), 0, 5)),
    ]
    args = [slot, ye, xmid, mods]
    if final:
        in_specs.append(pl.BlockSpec((1, D), lambda r, l: (0, 0)))
        args.append(final_g)
    return pl.pallas_call(
        _make_combine_kernel(final, row0),
        out_shape=jax.ShapeDtypeStruct((n_rows, ROW, D), F32),
        grid_spec=pltpu.PrefetchScalarGridSpec(
            num_scalar_prefetch=1,
            grid=(n_rows,),
            in_specs=in_specs,
            out_specs=pl.BlockSpec((None, ROW, D), lambda r, l: (r, 0, 0))),
        compiler_params=pltpu.CompilerParams(
            dimension_semantics=("arbitrary",), vmem_limit_bytes=VMEM_LIMIT),
        name="combine",
    )(_layer_arg(l), *args)


def _rope_rot_cols(w):
    q = QK_ROPE // 4
    return jnp.concatenate([-w[..., q:2 * q], w[..., 0:q], -w[..., 3 * q:4 * q], w[..., 2 * q:3 * q]], axis=-1)


def _prep_weights(norm1_g, norm2_g, w_in, q_norm_g, kv_norm_g, w_uq, w_ukv, conv_w, spatial_w,
                  spatial_b, w_out, w_router):
    o_pe, o_conv = Q_RANK + KV_RANK, Q_RANK + KV_RANK + QK_ROPE
    w_kpe = w_in[..., o_pe:o_conv]
    w_in_p = jnp.concatenate(
        [w_in[..., :o_pe], w_kpe, _rope_rot_cols(w_kpe), w_in[..., o_conv:]], axis=-1).astype(BF16)
    uq = w_uq.reshape(DEPTH, Q_RANK, HEADS, QK_NOPE + QK_ROPE)
    uq_pe = uq[..., QK_NOPE:]
    w_uq_p = jnp.concatenate(
        [uq[..., :QK_NOPE].reshape(DEPTH, Q_RANK, HEADS * QK_NOPE),
         jnp.concatenate([uq_pe, _rope_rot_cols(uq_pe)], axis=-1).reshape(DEPTH, Q_RANK, HEADS * LANES)],
        axis=-1).astype(BF16)
    sb = jnp.repeat(jnp.swapaxes(spatial_b, 1, 2), CMLP_W // CMLP_G, axis=-1)
    return dict(
        norm1_g=norm1_g.reshape(DEPTH, 1, D), norm2_g=norm2_g.reshape(DEPTH, 1, D),
        w_in=w_in_p, q_norm_g=q_norm_g.reshape(DEPTH, 1, Q_RANK),
        kv_norm_g=kv_norm_g.reshape(DEPTH, 1, KV_RANK), w_uq=w_uq_p, w_ukv=w_ukv.astype(BF16),
        conv_w=conv_w, spatial_w=spatial_w.reshape(DEPTH, CMLP_G * CHUNK, CHUNK).astype(BF16),
        spatial_b=sb, w_out=w_out.astype(BF16), w_router_t=jnp.swapaxes(w_router, 1, 2))


def _rope_table(n):
    rows = n // GRID_W
    row = jnp.repeat(jnp.arange(rows), GRID_W).astype(F32)
    col = jnp.tile(jnp.arange(GRID_W), rows).astype(F32)
    n_freq = QK_ROPE // 4
    inv = ROPE_BASE ** (-jnp.arange(n_freq, dtype=F32) / n_freq)
    ang_r, ang_c = row[:, None] * inv, col[:, None] * inv
    ang = jnp.concatenate([ang_r, ang_r, ang_c, ang_c], axis=-1)
    return jnp.concatenate([jnp.cos(ang), jnp.sin(ang)], axis=-1)


def kernel(x_prompt, x_sample, cache_ckv, cache_krope, c, c_ctx, w_ada, b_ada, norm1_g, norm2_g, w_in,
           q_norm_g, kv_norm_g, w_uq, w_ukv, conv_w, spatial_w, spatial_b, w_out, w_router, w_gate,
           w_up, w_down, final_norm_g):
    wts = _prep_weights(norm1_g, norm2_g, w_in, q_norm_g, kv_norm_g, w_uq, w_ukv, conv_w, spatial_w,
                        spatial_b, w_out, w_router)
    rope_tab = _rope_table(DEC_SEQ)
    cond = jnp.concatenate(
        [c, c_ctx[None, :], jnp.zeros((MOD_ROWS - DEC_BATCH - 1, D), F32)], axis=0)
    mods = _ada_call(cond, w_ada, b_ada).reshape(DEPTH, MOD_ROWS, 1, 6 * D)
    final_g = final_norm_g.reshape(1, D)
    row_mod = lambda r: jnp.where(r < P_ROWS, CTX_MOD_ROW, r - P_ROWS)

    x = jnp.concatenate([x_prompt.reshape(P_ROWS, ROW, D), x_sample], axis=0)
    ckv_list, kpe_list = [], []
    for l in range(DEPTH):
        xmid, h2, aff, ckv, kpe = _mixer_call(l, x, mods, wts, rope_tab, (cache_ckv, cache_krope))
        ckv_list.append(ckv.reshape(BATCH, SEQ, KV_RANK))
        kpe_list.append(kpe.reshape(BATCH, SEQ, QK_ROPE))
        aff = aff.reshape(N_ROWS * N_EXP, ROW)
        slot = _route_call(aff)
        ye = _expert_call(l, slot, aff, _gather_call(slot, h2), w_gate, w_up, w_down)
        if l < DEPTH - 1:
            x = _combine_call(l, slot, ye, xmid, mods, None, 0, N_ROWS, row_mod)
        else:
            y_prompt = _combine_call(l, slot, ye, xmid, mods, final_g, 0, P_ROWS, lambda r: CTX_MOD_ROW)
            y_sample = _combine_call(l, slot, ye, xmid, mods, final_g, P_ROWS, DEC_BATCH, lambda r: r)
    return (y_prompt.reshape(BATCH, SEQ, D), y_sample,
            jnp.stack(ckv_list, axis=1), jnp.stack(kpe_list, axis=1))
```

```python
import jax
import jax.numpy as jnp
from jax import lax
from jax.experimental import pallas as pl
from jax.experimental.pallas import tpu as pltpu

F32 = jnp.float32
BF16 = jnp.bfloat16
I32 = jnp.int32

D = 1024
BATCH = 16
SEQ = 256
DEPTH = 4
DEC_BATCH = 8
DEC_SEQ = 1024
PAST_LEN = 512
GRID_W = 64
HEADS = 4
QK_NOPE = 128
QK_ROPE = 64
V_HEAD = 128
Q_RANK = 256
KV_RANK = 128
CONV_W = 256
CMLP_W = 256
CMLP_G = 4
CHUNK = 128
N_EXP = 16
EC_FACTOR = 2
FF = D // 2
ROPE_BASE = 10000.0
EPS = 1e-6
LOG2_E = 1.4426950408889634

LANES = 128
ROW = 1024
N_ROWS = (BATCH * SEQ + DEC_BATCH * DEC_SEQ) // ROW
P_ROWS = BATCH * SEQ // ROW
P_PER_ROW = ROW // SEQ
CAP = EC_FACTOR * ROW // N_EXP
CAP_P = EC_FACTOR * SEQ // N_EXP
CTX_MOD_ROW = DEC_BATCH
MOD_ROWS = 16
QK_PAD = QK_NOPE + LANES

C_Q, C_KV, C_KPE, C_CONV, C_CMLP, C_END = 0, 256, 384, 512, 1280, 1792
U_NOPE, U_PE, U_END = 0, HEADS * QK_NOPE, HEADS * (QK_NOPE + LANES)
assert 2 * QK_ROPE == LANES
TB = 256
TQ = 256
AT_BLOCKS = 1
VMEM_LIMIT = 56 * 1024 * 1024

NT = (((1,), (1,)), ((), ()))
TN = (((0,), (0,)), ((), ()))


def _rms(x, g):
    return x * lax.rsqrt(jnp.mean(x * x, axis=-1, keepdims=True) + EPS) * g


def _silu(x):
    return x / (1.0 + jnp.exp(-x))


def _dot(a, b):
    return jnp.dot(a, b, preferred_element_type=F32)


def _dot_nt(a, b):
    return lax.dot_general(a, b, NT, preferred_element_type=F32)


def _skewed(stages, items):
    n = len(stages)
    for t in range(len(items) + n - 1):
        for s in range(n - 1, -1, -1):
            if 0 <= t - s < len(items):
                stages[s](items[t - s])


def _ada_kernel(c_ref, w_ref, b_ref, o_ref):
    s = _silu(c_ref[...]).astype(BF16)
    o_ref[...] = _dot(s, w_ref[...].astype(BF16)) + b_ref[...]


def _ada_call(cond, w_ada, b_ada):
    nc = 6
    return pl.pallas_call(
        _ada_kernel,
        out_shape=jax.ShapeDtypeStruct((DEPTH, MOD_ROWS, 6 * D), F32),
        grid=(DEPTH, nc),
        in_specs=[
            pl.BlockSpec((MOD_ROWS, D), lambda l, j: (0, 0)),
            pl.BlockSpec((None, D, D), lambda l, j: (l, 0, j)),
            pl.BlockSpec((None, 1, D), lambda l, j: (l, 0, j)),
        ],
        out_specs=pl.BlockSpec((None, MOD_ROWS, D), lambda l, j: (l, 0, j)),
        compiler_params=pltpu.CompilerParams(
            dimension_semantics=("arbitrary", "arbitrary"), vmem_limit_bytes=VMEM_LIMIT),
        name="ada_mod",
    )(cond, w_ada, b_ada.reshape(DEPTH, 1, 6 * D))


def _mixer_kernel(l_ref, x_ref, mod_ref, g1n_ref, g2n_ref, win_ref, qg_ref, kvg_ref, wuq_ref, wukv_ref,
                  cw_ref, sw_ref, sb_ref, wout_ref, wr_ref, cs_ref, cckv_ref, ckr_ref,
                  xmid_ref, h2_ref, aff_ref, ckv_out, kpe_out,
                  qf_s, kf_s, v_s, mix_s, gb_s, z_s):
    qk_scale = float(QK_NOPE + QK_ROPE) ** -0.5 * LOG2_E
    nblk = ROW // TB

    def run(ctx):
        n_past = 0 if ctx else PAST_LEN
        seg = SEQ if ctx else ROW
        mod = mod_ref[...]
        sh1, sc1, g1 = mod[:, 0:D], mod[:, D:2 * D], mod[:, 2 * D:3 * D]
        sh2, sc2, g2 = mod[:, 3 * D:4 * D], mod[:, 4 * D:5 * D], mod[:, 5 * D:6 * D]

        def store_kv(row0, rows, kv, kpb):
            for h in range(HEADS):
                c0 = h * (QK_NOPE + V_HEAD)
                kf_s[h, pl.ds(row0, rows), 0:QK_NOPE] = kv[:, c0:c0 + QK_NOPE].astype(BF16)
                kf_s[h, pl.ds(row0, rows), QK_NOPE:QK_PAD] = kpb
                v_s[h, pl.ds(row0, rows), :] = kv[:, c0 + QK_NOPE:c0 + QK_NOPE + V_HEAD].astype(BF16)

        if not ctx:
            kr = ckr_ref[...]
            store_kv(0, PAST_LEN, _dot(cckv_ref[...].astype(BF16), wukv_ref[...]),
                     jnp.concatenate([kr, jnp.zeros_like(kr)], axis=1).astype(BF16))

        grp_shift = (CMLP_W // CMLP_G).bit_length() - 1
        lane_grp = lax.shift_right_logical(lax.broadcasted_iota(I32, (CHUNK, CMLP_W), 1), grp_shift)

        def p1_norm(c):
            x = x_ref[pl.ds(c["r0"], TB), :]
            c["hb"] = (_rms(x, g1n_ref[...]) * (1.0 + sc1) + sh1).astype(BF16)

        def p1_proj(c):
            c["pa"] = _dot(c["hb"], win_ref[:, C_Q:C_CONV])
            c["cv"] = _dot(c["hb"], win_ref[:, C_CONV:C_CMLP])
            c["cm"] = _dot(c.pop("hb"), win_ref[:, C_CMLP:C_END])

        def p1_mid(c):
            r0, pa, cv, cm = c["r0"], c.pop("pa"), c.pop("cv"), c.pop("cm")
            kp = pa[:, C_KPE:C_CONV]
            c["cqn"] = _rms(pa[:, C_Q:C_KV], qg_ref[...]).astype(BF16)
            ckv_n = _rms(pa[:, C_KV:C_KPE], kvg_ref[...])
            if ctx:
                ckv_out[pl.ds(r0, TB), :] = ckv_n
                kpe_out[pl.ds(r0, TB), :] = kp[:, 0:QK_ROPE]
                kpe = kp
            else:
                y = kp * cs_ref[pl.ds(r0, TB), :]
                kpe = y + pltpu.roll(y, QK_ROPE, 1)
            c["ckv_b"], c["kpe_b"] = ckv_n.astype(BF16), kpe.astype(BF16)
            gb_s[pl.ds(r0, TB), :] = cv[:, 0:CONV_W]
            z_s[pl.ds(r0, TB), :] = cv[:, CONV_W:2 * CONV_W] * cv[:, 2 * CONV_W:3 * CONV_W]
            c["u"], c["vvb"] = cm[:, 0:CMLP_W], cm[:, CMLP_W:2 * CMLP_W].astype(BF16)

        def p1_up(c):
            c["qa"] = _dot(c.pop("cqn"), wuq_ref[...])
            c["kv"] = _dot(c.pop("ckv_b"), wukv_ref[...])
            vvb = c.pop("vvb")
            c["r"] = [_dot(sw_ref[...], vvb[k * CHUNK:(k + 1) * CHUNK, :]) for k in range(TB // CHUNK)]

        def p1_out(c):
            r0, qa, u = c["r0"], c.pop("qa"), c.pop("u")
            rope_lanes = lax.broadcasted_iota(I32, (TB, LANES), 1) < QK_ROPE
            for h in range(HEADS):
                qn = qa[:, U_NOPE + h * QK_NOPE:U_NOPE + (h + 1) * QK_NOPE]
                qp = qa[:, U_PE + h * LANES:U_PE + (h + 1) * LANES]
                if not ctx:
                    y = qp * cs_ref[pl.ds(r0, TB), :]
                    qp = y + pltpu.roll(y, QK_ROPE, 1)
                qp = jnp.where(rope_lanes, qp, 0.0)
                qf_s[h, pl.ds(r0, TB), 0:QK_NOPE] = (qn * qk_scale).astype(BF16)
                qf_s[h, pl.ds(r0, TB), QK_NOPE:QK_PAD] = (qp * qk_scale).astype(BF16)
            store_kv(n_past + r0, TB, c.pop("kv"), c.pop("kpe_b"))
            for k, r in enumerate(c.pop("r")):
                mixed = jnp.where(
                    lane_grp == 0, r[0:CHUNK],
                    jnp.where(lane_grp == 1, r[CHUNK:2 * CHUNK],
                              jnp.where(lane_grp == 2, r[2 * CHUNK:3 * CHUNK], r[3 * CHUNK:4 * CHUNK])))
                out = u[k * CHUNK:(k + 1) * CHUNK, :] * (mixed + sb_ref[...])
                mix_s[pl.ds(r0 + k * CHUNK, CHUNK), HEADS * V_HEAD + CONV_W:D] = out.astype(BF16)

        _skewed([p1_norm, p1_proj, p1_mid, p1_up, p1_out], [dict(r0=i * TB) for i in range(nblk)])

        z = z_s[...]
        pos = lax.broadcasted_iota(I32, (ROW, CONV_W), 0) & (seg - 1)
        zm = jnp.where(pos == 0, 0.0, pltpu.roll(z, 1, 0))
        zp = jnp.where(pos == seg - 1, 0.0, pltpu.roll(z, ROW - 1, 0))
        cw = cw_ref[...]
        conv = zm * cw[0:1, :] + z * cw[1:2, :] + zp * cw[2:3, :]
        mix_s[:, HEADS * V_HEAD:HEADS * V_HEAD + CONV_W] = (gb_s[...] * conv).astype(BF16)

        def at_qk(c):
            keys = kf_s[c["h"], pl.ds(c["k0"], seg + n_past), :]
            c["s"] = _dot_nt(qf_s[c["h"], pl.ds(c["r0"], TQ), :], keys)

        def at_max(c):
            c["m"] = jnp.max(c["s"], axis=-1, keepdims=True)

        def at_exp(c):
            e = jnp.exp2(c.pop("s") - c.pop("m"))
            c["l"] = jnp.sum(e, axis=-1, keepdims=True)
            c["p"] = e.astype(BF16)

        def at_pv(c):
            c["o"] = _dot(c.pop("p"), v_s[c["h"], pl.ds(c["k0"], seg + n_past), :])

        def at_out(c):
            o = c.pop("o") / c.pop("l")
            mix_s[pl.ds(c["r0"], TQ), c["h"] * V_HEAD:(c["h"] + 1) * V_HEAD] = o.astype(BF16)

        at_stages = [at_qk, lambda c: (at_max(c), at_exp(c)), lambda c: (at_pv(c), at_out(c))]
        if ctx:
            _skewed(at_stages, [dict(r0=b * TQ, k0=b * TQ, h=h) for b in range(ROW // TQ) for h in range(HEADS)])
        else:
            def at_block(b, carry):
                r0 = pl.multiple_of(b * (AT_BLOCKS * TQ), AT_BLOCKS * TQ)
                _skewed(at_stages, [dict(r0=r0 + k * TQ, k0=0, h=h) for k in range(AT_BLOCKS) for h in range(HEADS)])
                return carry

            lax.fori_loop(0, ROW // (AT_BLOCKS * TQ), at_block, 0)

        wr = wr_ref[...]
        wr_hi = wr.astype(BF16)
        wr_hl = jnp.concatenate([wr_hi, (wr - wr_hi.astype(F32)).astype(BF16)], axis=0)

        def p2_proj(c):
            c["mo"] = _dot(mix_s[pl.ds(c["r0"], TB), :], wout_ref[...])

        def p2_norm(c):
            r0 = c["r0"]
            xm = x_ref[pl.ds(r0, TB), :] + g1 * c.pop("mo")
            xmid_ref[pl.ds(r0, TB), :] = xm
            h2 = _rms(xm, g2n_ref[...]) * (1.0 + sc2) + sh2
            h2_hi = h2.astype(BF16)
            h2_ref[pl.ds(r0, TB), :] = h2_hi
            c["hi"], c["lo"] = h2_hi, (h2 - h2_hi.astype(F32)).astype(BF16)

        def p2_route(c):
            la = _dot_nt(wr_hl, c["hi"])
            c["lg"] = la[0:N_EXP] + la[N_EXP:2 * N_EXP] + _dot_nt(wr_hi, c.pop("lo"))
            c.pop("hi")

        def p2_aff(c):
            lg = c.pop("lg")
            e = jnp.exp(lg - jnp.max(lg, axis=0, keepdims=True))
            aff_ref[:, pl.ds(c["r0"], TB)] = e / jnp.sum(e, axis=0, keepdims=True)

        _skewed([p2_proj, p2_norm, p2_route, p2_aff], [dict(r0=i * TB) for i in range(nblk)])

    is_ctx = pl.program_id(0) < P_ROWS

    @pl.when(is_ctx)
    def _():
        run(True)

    @pl.when(jnp.logical_not(is_ctx))
    def _():
        run(False)


def _layer_spec(shape):
    nd = len(shape)
    return pl.BlockSpec((None,) + shape, lambda r, l: (l[0],) + (0,) * nd, pipeline_mode=pl.Buffered(1))


def _layer_arg(l):
    return jnp.full((1,), l, I32)


def _mixer_call(l, x, mods, wts, rope_tab, caches):
    lk = PAST_LEN + ROW
    lat = lambda r: jnp.maximum(r - P_ROWS, 0)
    ctx = lambda r: jnp.minimum(r, P_ROWS - 1)
    in_specs = [
        pl.BlockSpec((None, ROW, D), lambda r, l: (r, 0, 0)),
        pl.BlockSpec((None, None, 1, 6 * D),
                     lambda r, l: (l[0], jnp.where(r < P_ROWS, CTX_MOD_ROW, r - P_ROWS), 0, 0)),
        _layer_spec((1, D)), _layer_spec((1, D)),
        _layer_spec((D, C_END)),
        _layer_spec((1, Q_RANK)), _layer_spec((1, KV_RANK)),
        _layer_spec((Q_RANK, U_END)),
        _layer_spec((KV_RANK, HEADS * (QK_NOPE + V_HEAD))),
        _layer_spec((3, CONV_W)),
        _layer_spec((CMLP_G * CHUNK, CHUNK)),
        _layer_spec((CHUNK, CMLP_W)),
        _layer_spec((D, D)),
        _layer_spec((N_EXP, D)),
        pl.BlockSpec((ROW, LANES), lambda r, l: (0, 0), pipeline_mode=pl.Buffered(1)),
        pl.BlockSpec((None, None, PAST_LEN, KV_RANK), lambda r, l: (lat(r), l[0], 0, 0)),
        pl.BlockSpec((None, None, PAST_LEN, QK_ROPE), lambda r, l: (lat(r), l[0], 0, 0)),
    ]
    args = [x, mods, wts["norm1_g"], wts["norm2_g"], wts["w_in"], wts["q_norm_g"], wts["kv_norm_g"],
            wts["w_uq"], wts["w_ukv"], wts["conv_w"], wts["spatial_w"], wts["spatial_b"],
            wts["w_out"], wts["w_router_t"], rope_tab, caches[0], caches[1]]
    out_shape = [jax.ShapeDtypeStruct((N_ROWS, ROW, D), F32),
                 jax.ShapeDtypeStruct((N_ROWS, ROW, D), BF16),
                 jax.ShapeDtypeStruct((N_ROWS, N_EXP, ROW), F32),
                 jax.ShapeDtypeStruct((P_ROWS, ROW, KV_RANK), F32),
                 jax.ShapeDtypeStruct((P_ROWS, ROW, QK_ROPE), F32)]
    out_specs = [pl.BlockSpec((None, ROW, D), lambda r, l: (r, 0, 0)),
                 pl.BlockSpec((None, ROW, D), lambda r, l: (r, 0, 0)),
                 pl.BlockSpec((None, N_EXP, ROW), lambda r, l: (r, 0, 0)),
                 pl.BlockSpec((None, ROW, KV_RANK), lambda r, l: (ctx(r), 0, 0)),
                 pl.BlockSpec((None, ROW, QK_ROPE), lambda r, l: (ctx(r), 0, 0))]
    return pl.pallas_call(
        _mixer_kernel,
        out_shape=out_shape,
        grid_spec=pltpu.PrefetchScalarGridSpec(
            num_scalar_prefetch=1,
            grid=(N_ROWS,),
            in_specs=in_specs,
            out_specs=out_specs,
            scratch_shapes=[
                pltpu.VMEM((HEADS, ROW, QK_PAD), BF16),
                pltpu.VMEM((HEADS, lk, QK_PAD), BF16),
                pltpu.VMEM((HEADS, lk, V_HEAD), BF16),
                pltpu.VMEM((ROW, D), BF16),
                pltpu.VMEM((ROW, CONV_W), F32),
                pltpu.VMEM((ROW, CONV_W), F32),
            ]),
        compiler_params=pltpu.CompilerParams(
            dimension_semantics=("arbitrary",), vmem_limit_bytes=VMEM_LIMIT),
        name="mixer",
    )(_layer_arg(l), *args)


def _cap_thresholds(affs, caps):
    def bit_step(i, ts):
        bit = jnp.left_shift(jnp.int32(1), 30 - i)
        out = []
        for aff, cap, t in zip(affs, caps, ts):
            cand = t | bit
            cnt = jnp.sum(jnp.where(aff >= pltpu.bitcast(cand, F32), 1.0, 0.0), axis=1, keepdims=True)
            out.append(jnp.where(cnt >= float(cap), cand, t))
        return tuple(out)

    ts = lax.fori_loop(0, 31, bit_step, tuple(jnp.zeros((a.shape[0], 1), I32) for a in affs))
    return [pltpu.bitcast(t, F32) for t in ts]


def _select_slots(aff, thr, cap, tri):
    capf = float(cap)
    gt = aff > thr
    eq = aff == thr
    n_gt = jnp.sum(jnp.where(gt, 1.0, 0.0), axis=1, keepdims=True)
    eq_rank = _dot(jnp.where(eq, 1.0, 0.0).astype(BF16), tri)
    sel = jnp.where(gt, 1.0, jnp.where(eq, jnp.where(eq_rank < capf - n_gt, 1.0, 0.0), 0.0))
    pos = _dot(sel.astype(BF16), tri)
    return jnp.where(sel > 0.5, pos.astype(I32), -1)


def _route_kernel(aff_ref, slot_ref, tri_s):
    r = lax.broadcasted_iota(I32, (ROW, ROW), 0)
    c = lax.broadcasted_iota(I32, (ROW, ROW), 1)
    tri_s[...] = jnp.where(r < c, 1.0, 0.0).astype(BF16)
    np_ = P_ROWS * N_EXP
    affs = [aff_ref[0:np_, s * SEQ:(s + 1) * SEQ] for s in range(P_PER_ROW)] + [aff_ref[np_:, :]]
    caps = [CAP_P] * P_PER_ROW + [CAP]
    thrs = _cap_thresholds(affs, caps)
    for s in range(P_PER_ROW):
        sl = _select_slots(affs[s], thrs[s], CAP_P, tri_s[0:SEQ, 0:SEQ])
        slot_ref[0:np_, s * SEQ:(s + 1) * SEQ] = jnp.where(sl >= 0, sl + s * CAP_P, -1)
    slot_ref[np_:, :] = _select_slots(affs[-1], thrs[-1], CAP, tri_s[...])


def _route_call(aff):
    rows = N_ROWS * N_EXP
    return pl.pallas_call(
        _route_kernel,
        out_shape=jax.ShapeDtypeStruct((rows, ROW), I32),
        scratch_shapes=[pltpu.VMEM((ROW, ROW), BF16)],
        compiler_params=pltpu.CompilerParams(vmem_limit_bytes=VMEM_LIMIT),
        name="route",
    )(aff)


GATHER_NC = 256


def _gather_kernel(slot_ref, h2_ref, xs_ref, hot_s):
    @pl.when(pl.program_id(0) >= P_ROWS)
    def _():
        j = lax.broadcasted_iota(I32, (CAP, ROW), 0)
        for e in range(N_EXP):
            hot_s[e * CAP:(e + 1) * CAP, :] = jnp.where(slot_ref[e:e + 1, :] == j, 1.0, 0.0).astype(BF16)
        for c in range(D // GATHER_NC):
            xs = _dot(hot_s[...], h2_ref[:, c * GATHER_NC:(c + 1) * GATHER_NC]).astype(BF16)
            for e in range(N_EXP):
                xs_ref[e, :, c * GATHER_NC:(c + 1) * GATHER_NC] = xs[e * CAP:(e + 1) * CAP, :]

    @pl.when(pl.program_id(0) < P_ROWS)
    def _():
        j = lax.broadcasted_iota(I32, (CAP_P, SEQ), 0)
        for s in range(P_PER_ROW):
            toks = slice(s * SEQ, (s + 1) * SEQ)
            for e in range(N_EXP):
                hot_s[e * CAP_P:(e + 1) * CAP_P, 0:SEQ] = jnp.where(
                    slot_ref[e:e + 1, toks] == j + s * CAP_P, 1.0, 0.0).astype(BF16)
            xs = _dot(hot_s[0:N_EXP * CAP_P, 0:SEQ], h2_ref[toks, :]).astype(BF16)
            for e in range(N_EXP):
                xs_ref[e, s * CAP_P:(s + 1) * CAP_P, :] = xs[e * CAP_P:(e + 1) * CAP_P, :]


def _gather_call(slot, h2):
    return pl.pallas_call(
        _gather_kernel,
        out_shape=jax.ShapeDtypeStruct((N_EXP, N_ROWS, CAP, D), BF16),
        grid=(N_ROWS,),
        in_specs=[
            pl.BlockSpec((N_EXP, ROW), lambda r: (r, 0)),
            pl.BlockSpec((None, ROW, D), lambda r: (r, 0, 0)),
        ],
        out_specs=pl.BlockSpec((N_EXP, None, CAP, D), lambda r: (0, r, 0, 0)),
        scratch_shapes=[pltpu.VMEM((N_EXP * CAP, ROW), BF16)],
        compiler_params=pltpu.CompilerParams(
            dimension_semantics=("arbitrary",), vmem_limit_bytes=VMEM_LIMIT),
        name="gather",
    )(slot, h2)


EXP_ROWS = 4


def _expert_kernel(l_ref, slot_ref, aff_ref, xs_ref, wg_ref, wu_ref, wd_ref, ye_ref, wg_s, wu_s, wd_s):
    e = pl.program_id(0)
    wg_s[...] = wg_ref[...].astype(BF16)
    wu_s[...] = wu_ref[...].astype(BF16)
    wd_s[...] = wd_ref[...].astype(BF16)
    j = lax.broadcasted_iota(I32, (CAP, ROW), 0)

    def gates(c):
        c["gate"] = []
        for k in range(EXP_ROWS):
            row = (c["r0"] + k) * N_EXP + e
            hit = slot_ref[pl.ds(row, 1), :] == j
            c["gate"].append(jnp.sum(jnp.where(hit, aff_ref[pl.ds(row, 1), :], 0.0), axis=1, keepdims=True))

    def up(c):
        xs = xs_ref[c["r0"]:c["r0"] + EXP_ROWS].reshape(EXP_ROWS * CAP, D)
        c["a"], c["u"] = _dot(xs, wg_s[...]), _dot(xs, wu_s[...])

    def act(c):
        c["hid"] = (_silu(c.pop("a")) * c.pop("u")).astype(BF16)

    def down(c):
        c["ye"] = _dot(c.pop("hid"), wd_s[...])

    def out(c):
        ye, gate = c.pop("ye"), c.pop("gate")
        for k in range(EXP_ROWS):
            ye_ref[c["r0"] + k] = (ye[k * CAP:(k + 1) * CAP, :] * gate[k]).astype(BF16)

    _skewed([gates, up, act, down, out], [dict(r0=g * EXP_ROWS) for g in range(N_ROWS // EXP_ROWS)])


def _expert_call(l, slot, aff, xs, w_gate, w_up, w_down):
    rows = N_ROWS * N_EXP
    return pl.pallas_call(
        _expert_kernel,
        out_shape=jax.ShapeDtypeStruct((N_EXP, N_ROWS, CAP, D), BF16),
        grid_spec=pltpu.PrefetchScalarGridSpec(
            num_scalar_prefetch=1,
            grid=(N_EXP,),
            in_specs=[
                pl.BlockSpec((rows, ROW), lambda e, l: (0, 0), pipeline_mode=pl.Buffered(1)),
                pl.BlockSpec((rows, ROW), lambda e, l: (0, 0), pipeline_mode=pl.Buffered(1)),
                pl.BlockSpec((None, N_ROWS, CAP, D), lambda e, l: (e, 0, 0, 0)),
                pl.BlockSpec((None, None, D, FF), lambda e, l: (l[0], e, 0, 0)),
                pl.BlockSpec((None, None, D, FF), lambda e, l: (l[0], e, 0, 0)),
                pl.BlockSpec((None, None, FF, D), lambda e, l: (l[0], e, 0, 0)),
            ],
            out_specs=pl.BlockSpec((None, N_ROWS, CAP, D), lambda e, l: (e, 0, 0, 0)),
            scratch_shapes=[pltpu.VMEM((D, FF), BF16), pltpu.VMEM((D, FF), BF16), pltpu.VMEM((FF, D), BF16)]),
        compiler_params=pltpu.CompilerParams(
            dimension_semantics=("arbitrary",), vmem_limit_bytes=VMEM_LIMIT),
        name="experts",
    )(_layer_arg(l), slot, aff, xs, w_gate, w_up, w_down)


COMB_TC = 256


def _make_combine_kernel(final, first_row):
    def kern(*refs):
        if final:
            _, slot_ref, ye_ref, xmid_ref, g2_ref, fg_ref, o_ref = refs
        else:
            _, slot_ref, ye_ref, xmid_ref, g2_ref, o_ref = refs
        g2 = g2_ref[...]

        def run(ctx):
            nslot = CAP_P if ctx else CAP
            j = lax.broadcasted_iota(I32, (nslot, COMB_TC), 0)
            if not ctx:
                ye = ye_ref[...].reshape(N_EXP * CAP, D)
            for t in range(ROW // COMB_TC):
                cols = slice(t * COMB_TC, (t + 1) * COMB_TC)
                s0 = t * CAP_P if ctx else 0
                onehot = jnp.concatenate(
                    [jnp.where(slot_ref[e:e + 1, cols] == j + s0, 1.0, 0.0).astype(BF16) for e in range(N_EXP)],
                    axis=0)
                if ctx:
                    ye = jnp.concatenate([ye_ref[e, s0:s0 + CAP_P, :] for e in range(N_EXP)], axis=0)
                moe = lax.dot_general(onehot, ye, TN, preferred_element_type=F32)
                xn = xmid_ref[cols, :] + g2 * moe
                if final:
                    xn = _rms(xn, fg_ref[...])
                o_ref[cols, :] = xn

        is_ctx = first_row + pl.program_id(0) < P_ROWS

        @pl.when(is_ctx)
        def _():
            run(True)

        @pl.when(jnp.logical_not(is_ctx))
        def _():
            run(False)

    return kern


def _combine_call(l, slot, ye, xmid, mods, final_g, row0, n_rows, mod_row):
    final = final_g is not None
    in_specs = [
        pl.BlockSpec((N_EXP, ROW), lambda r, l: (row0 + r, 0)),
        pl.BlockSpec((N_EXP, None, CAP, D), lambda r, l: (0, row0 + r, 0, 0)),
        pl.BlockSpec((None, ROW, D), lambda r, l: (row0 + r, 0, 0)),
        pl.BlockSpec((None, None, 1, D), lambda r, l: (l[0], mod_row(r), 0, 5)),
    ]
    args = [slot, ye, xmid, mods]
    if final:
        in_specs.append(pl.BlockSpec((1, D), lambda r, l: (0, 0)))
        args.append(final_g)
    return pl.pallas_call(
        _make_combine_kernel(final, row0),
        out_shape=jax.ShapeDtypeStruct((n_rows, ROW, D), F32),
        grid_spec=pltpu.PrefetchScalarGridSpec(
            num_scalar_prefetch=1,
            grid=(n_rows,),
            in_specs=in_specs,
            out_specs=pl.BlockSpec((None, ROW, D), lambda r, l: (r, 0, 0))),
        compiler_params=pltpu.CompilerParams(
            dimension_semantics=("arbitrary",), vmem_limit_bytes=VMEM_LIMIT),
        name="combine",
    )(_layer_arg(l), *args)


def _rope_rot_cols(w):
    q = QK_ROPE // 4
    return jnp.concatenate([-w[..., q:2 * q], w[..., 0:q], -w[..., 3 * q:4 * q], w[..., 2 * q:3 * q]], axis=-1)


def _repack_kernel(win_ref, wuq_ref, wout_ref, win_o, wuq_o, wout_o):
    o_conv = Q_RANK + KV_RANK + QK_ROPE
    w = win_ref[...]
    win_o[...] = jnp.concatenate(
        [w[:, :o_conv], _rope_rot_cols(w[:, o_conv - QK_ROPE:o_conv]), w[:, o_conv:]], axis=1).astype(BF16)
    u = wuq_ref[...]
    hd = QK_NOPE + QK_ROPE
    pe = [u[:, h * hd + QK_NOPE:(h + 1) * hd] for h in range(HEADS)]
    wuq_o[...] = jnp.concatenate(
        [u[:, h * hd:h * hd + QK_NOPE] for h in range(HEADS)]
        + [piece for h in range(HEADS) for piece in (pe[h], _rope_rot_cols(pe[h]))], axis=1).astype(BF16)
    wout_o[...] = wout_ref[...].astype(BF16)


def _repack_call(w_in, w_uq, w_out):
    in_w, uq_w = w_in.shape[-1], w_uq.shape[-1]
    return pl.pallas_call(
        _repack_kernel,
        out_shape=[jax.ShapeDtypeStruct((DEPTH, D, C_END), BF16),
                   jax.ShapeDtypeStruct((DEPTH, Q_RANK, U_END), BF16),
                   jax.ShapeDtypeStruct((DEPTH, D, D), BF16)],
        grid=(DEPTH,),
        in_specs=[pl.BlockSpec((None, D, in_w), lambda l: (l, 0, 0)),
                  pl.BlockSpec((None, Q_RANK, uq_w), lambda l: (l, 0, 0)),
                  pl.BlockSpec((None, D, D), lambda l: (l, 0, 0))],
        out_specs=[pl.BlockSpec((None, D, C_END), lambda l: (l, 0, 0)),
                   pl.BlockSpec((None, Q_RANK, U_END), lambda l: (l, 0, 0)),
                   pl.BlockSpec((None, D, D), lambda l: (l, 0, 0))],
        compiler_params=pltpu.CompilerParams(
            dimension_semantics=("arbitrary",), vmem_limit_bytes=VMEM_LIMIT),
        name="repack",
    )(w_in, w_uq, w_out)


def _prep_weights(norm1_g, norm2_g, w_in, q_norm_g, kv_norm_g, w_uq, w_ukv, conv_w, spatial_w,
                  spatial_b, w_out, w_router):
    w_in_p, w_uq_p, w_out_p = _repack_call(w_in, w_uq, w_out)
    sb = jnp.repeat(jnp.swapaxes(spatial_b, 1, 2), CMLP_W // CMLP_G, axis=-1)
    return dict(
        norm1_g=norm1_g.reshape(DEPTH, 1, D), norm2_g=norm2_g.reshape(DEPTH, 1, D),
        w_in=w_in_p, q_norm_g=q_norm_g.reshape(DEPTH, 1, Q_RANK),
        kv_norm_g=kv_norm_g.reshape(DEPTH, 1, KV_RANK), w_uq=w_uq_p, w_ukv=w_ukv.astype(BF16),
        conv_w=conv_w, spatial_w=spatial_w.reshape(DEPTH, CMLP_G * CHUNK, CHUNK).astype(BF16),
        spatial_b=sb, w_out=w_out_p, w_router_t=jnp.swapaxes(w_router, 1, 2))


def _rope_table(n):
    rows = n // GRID_W
    row = jnp.repeat(jnp.arange(rows), GRID_W).astype(F32)
    col = jnp.tile(jnp.arange(GRID_W), rows).astype(F32)
    n_freq = QK_ROPE // 4
    inv = ROPE_BASE ** (-jnp.arange(n_freq, dtype=F32) / n_freq)
    ang_r, ang_c = row[:, None] * inv, col[:, None] * inv
    ang = jnp.concatenate([ang_r, ang_r, ang_c, ang_c], axis=-1)
    return jnp.concatenate([jnp.cos(ang), jnp.sin(ang)], axis=-1)


def kernel(x_prompt, x_sample, cache_ckv, cache_krope, c, c_ctx, w_ada, b_ada, norm1_g, norm2_g, w_in,
           q_norm_g, kv_norm_g, w_uq, w_ukv, conv_w, spatial_w, spatial_b, w_out, w_router, w_gate,
           w_up, w_down, final_norm_g):
    wts = _prep_weights(norm1_g, norm2_g, w_in, q_norm_g, kv_norm_g, w_uq, w_ukv, conv_w, spatial_w,
                        spatial_b, w_out, w_router)
    rope_tab = _rope_table(DEC_SEQ)
    cond = jnp.concatenate(
        [c, c_ctx[None, :], jnp.zeros((MOD_ROWS - DEC_BATCH - 1, D), F32)], axis=0)
    mods = _ada_call(cond, w_ada, b_ada).reshape(DEPTH, MOD_ROWS, 1, 6 * D)
    final_g = final_norm_g.reshape(1, D)
    row_mod = lambda r: jnp.where(r < P_ROWS, CTX_MOD_ROW, r - P_ROWS)

    x = jnp.concatenate([x_prompt.reshape(P_ROWS, ROW, D), x_sample], axis=0)
    ckv_list, kpe_list = [], []
    for l in range(DEPTH):
        xmid, h2, aff, ckv, kpe = _mixer_call(l, x, mods, wts, rope_tab, (cache_ckv, cache_krope))
        ckv_list.append(ckv.reshape(BATCH, SEQ, KV_RANK))
        kpe_list.append(kpe.reshape(BATCH, SEQ, QK_ROPE))
        aff = aff.reshape(N_ROWS * N_EXP, ROW)
        slot = _route_call(aff)
        ye = _expert_call(l, slot, aff, _gather_call(slot, h2), w_gate, w_up, w_down)
        if l < DEPTH - 1:
            x = _combine_call(l, slot, ye, xmid, mods, None, 0, N_ROWS, row_mod)
        else:
            y_prompt = _combine_call(l, slot, ye, xmid, mods, final_g, 0, P_ROWS, lambda r: CTX_MOD_ROW)
            y_sample = _combine_call(l, slot, ye, xmid, mods, final_g, P_ROWS, DEC_BATCH, lambda r: r)
    return (y_prompt.reshape(BATCH, SEQ, D), y_sample,
            jnp.stack(ckv_list, axis=1), jnp.stack(kpe_list, axis=1))
```

```python
import jax
import jax.numpy as jnp
from jax import lax
from jax.experimental import pallas as pl
from jax.experimental.pallas import tpu as pltpu

F32 = jnp.float32
BF16 = jnp.bfloat16
I32 = jnp.int32

D = 1024
BATCH = 16
SEQ = 256
DEPTH = 4
DEC_BATCH = 8
DEC_SEQ = 1024
PAST_LEN = 512
GRID_W = 64
HEADS = 4
QK_NOPE = 128
QK_ROPE = 64
V_HEAD = 128
Q_RANK = 256
KV_RANK = 128
CONV_W = 256
CMLP_W = 256
CMLP_G = 4
CHUNK = 128
N_EXP = 16
EC_FACTOR = 2
FF = D // 2
ROPE_BASE = 10000.0
EPS = 1e-6
LOG2_E = 1.4426950408889634

LANES = 128
ROW = 1024
N_ROWS = (BATCH * SEQ + DEC_BATCH * DEC_SEQ) // ROW
P_ROWS = BATCH * SEQ // ROW
P_PER_ROW = ROW // SEQ
CAP = EC_FACTOR * ROW // N_EXP
CAP_P = EC_FACTOR * SEQ // N_EXP
CTX_MOD_ROW = DEC_BATCH
MOD_ROWS = 16
QK_PAD = QK_NOPE + LANES

C_Q, C_KV, C_KPE, C_CONV, C_CMLP, C_END = 0, 256, 384, 512, 1280, 1792
U_NOPE, U_PE, U_END = 0, HEADS * QK_NOPE, HEADS * (QK_NOPE + LANES)
assert 2 * QK_ROPE == LANES
TB = 256
TQ = 256
AT_BLOCKS = 1
VMEM_LIMIT = 56 * 1024 * 1024

NT = (((1,), (1,)), ((), ()))
TN = (((0,), (0,)), ((), ()))


def _rms(x, g):
    return x * lax.rsqrt(jnp.mean(x * x, axis=-1, keepdims=True) + EPS) * g


def _silu(x):
    return x / (1.0 + jnp.exp(-x))


def _dot(a, b):
    return jnp.dot(a, b, preferred_element_type=F32)


def _dot_nt(a, b):
    return lax.dot_general(a, b, NT, preferred_element_type=F32)


def _skewed(stages, items):
    n = len(stages)
    for t in range(len(items) + n - 1):
        for s in range(n - 1, -1, -1):
            if 0 <= t - s < len(items):
                stages[s](items[t - s])


def _ada_kernel(c_ref, w_ref, b_ref, o_ref):
    s = _silu(c_ref[...]).astype(BF16)
    o_ref[...] = _dot(s, w_ref[...].astype(BF16)) + b_ref[...]


def _ada_call(cond, w_ada, b_ada):
    nc = 6
    return pl.pallas_call(
        _ada_kernel,
        out_shape=jax.ShapeDtypeStruct((DEPTH, MOD_ROWS, 6 * D), F32),
        grid=(DEPTH, nc),
        in_specs=[
            pl.BlockSpec((MOD_ROWS, D), lambda l, j: (0, 0)),
            pl.BlockSpec((None, D, D), lambda l, j: (l, 0, j)),
            pl.BlockSpec((None, 1, D), lambda l, j: (l, 0, j)),
        ],
        out_specs=pl.BlockSpec((None, MOD_ROWS, D), lambda l, j: (l, 0, j)),
        compiler_params=pltpu.CompilerParams(
            dimension_semantics=("arbitrary", "arbitrary"), vmem_limit_bytes=VMEM_LIMIT),
        name="ada_mod",
    )(cond, w_ada, b_ada.reshape(DEPTH, 1, 6 * D))


def _mixer_kernel(l_ref, x_ref, mod_ref, g1n_ref, g2n_ref, win_ref, qg_ref, kvg_ref, wuq_ref, wukv_ref,
                  cw_ref, sw_ref, sb_ref, wout_ref, wr_ref, cs_ref, cckv_ref, ckr_ref,
                  xmid_ref, h2_ref, aff_ref, ckv_out, kpe_out,
                  qf_s, kf_s, v_s, mix_s, gb_s, z_s):
    qk_scale = float(QK_NOPE + QK_ROPE) ** -0.5 * LOG2_E
    nblk = ROW // TB

    def run(ctx):
        n_past = 0 if ctx else PAST_LEN
        seg = SEQ if ctx else ROW
        mod = mod_ref[...]
        sh1, sc1, g1 = mod[:, 0:D], mod[:, D:2 * D], mod[:, 2 * D:3 * D]
        sh2, sc2, g2 = mod[:, 3 * D:4 * D], mod[:, 4 * D:5 * D], mod[:, 5 * D:6 * D]

        def store_kv(row0, rows, kv, kpb):
            for h in range(HEADS):
                c0 = h * (QK_NOPE + V_HEAD)
                kf_s[h, pl.ds(row0, rows), 0:QK_NOPE] = kv[:, c0:c0 + QK_NOPE].astype(BF16)
                kf_s[h, pl.ds(row0, rows), QK_NOPE:QK_PAD] = kpb
                v_s[h, pl.ds(row0, rows), :] = kv[:, c0 + QK_NOPE:c0 + QK_NOPE + V_HEAD].astype(BF16)

        if not ctx:
            kr_t = ckr_ref[...]
            store_kv(0, PAST_LEN, _dot(cckv_ref[...].astype(BF16), wukv_ref[...]),
                     jnp.concatenate([kr_t, jnp.zeros_like(kr_t)], axis=0).T.astype(BF16))

        grp_shift = (CMLP_W // CMLP_G).bit_length() - 1
        lane_grp = lax.shift_right_logical(lax.broadcasted_iota(I32, (CHUNK, CMLP_W), 1), grp_shift)

        def p1_norm(c):
            x = x_ref[pl.ds(c["r0"], TB), :]
            c["hb"] = (_rms(x, g1n_ref[...]) * (1.0 + sc1) + sh1).astype(BF16)

        def p1_proj(c):
            c["pa"] = _dot(c["hb"], win_ref[:, C_Q:C_CONV])
            c["cv"] = _dot(c["hb"], win_ref[:, C_CONV:C_CMLP])
            c["cm"] = _dot(c.pop("hb"), win_ref[:, C_CMLP:C_END])

        def p1_mid(c):
            r0, pa, cv, cm = c["r0"], c.pop("pa"), c.pop("cv"), c.pop("cm")
            kp = pa[:, C_KPE:C_CONV]
            c["cqn"] = _rms(pa[:, C_Q:C_KV], qg_ref[...]).astype(BF16)
            ckv_n = _rms(pa[:, C_KV:C_KPE], kvg_ref[...])
            if ctx:
                ckv_out[pl.ds(r0, TB), :] = ckv_n
                kpe_out[pl.ds(r0, TB), :] = kp[:, 0:QK_ROPE]
                kpe = kp
            else:
                y = kp * cs_ref[pl.ds(r0, TB), :]
                kpe = y + pltpu.roll(y, QK_ROPE, 1)
            c["ckv_b"], c["kpe_b"] = ckv_n.astype(BF16), kpe.astype(BF16)
            gb_s[pl.ds(r0, TB), :] = cv[:, 0:CONV_W]
            z_s[pl.ds(r0, TB), :] = cv[:, CONV_W:2 * CONV_W] * cv[:, 2 * CONV_W:3 * CONV_W]
            c["u"], c["vvb"] = cm[:, 0:CMLP_W], cm[:, CMLP_W:2 * CMLP_W].astype(BF16)

        def p1_up(c):
            c["qa"] = _dot(c.pop("cqn"), wuq_ref[...])
            c["kv"] = _dot(c.pop("ckv_b"), wukv_ref[...])
            vvb = c.pop("vvb")
            c["r"] = [_dot(sw_ref[...], vvb[k * CHUNK:(k + 1) * CHUNK, :]) for k in range(TB // CHUNK)]

        def p1_out(c):
            r0, qa, u = c["r0"], c.pop("qa"), c.pop("u")
            rope_lanes = lax.broadcasted_iota(I32, (TB, LANES), 1) < QK_ROPE
            for h in range(HEADS):
                qn = qa[:, U_NOPE + h * QK_NOPE:U_NOPE + (h + 1) * QK_NOPE]
                qp = qa[:, U_PE + h * LANES:U_PE + (h + 1) * LANES]
                if not ctx:
                    y = qp * cs_ref[pl.ds(r0, TB), :]
                    qp = y + pltpu.roll(y, QK_ROPE, 1)
                qp = jnp.where(rope_lanes, qp, 0.0)
                qf_s[h, pl.ds(r0, TB), 0:QK_NOPE] = (qn * qk_scale).astype(BF16)
                qf_s[h, pl.ds(r0, TB), QK_NOPE:QK_PAD] = (qp * qk_scale).astype(BF16)
            store_kv(n_past + r0, TB, c.pop("kv"), c.pop("kpe_b"))
            for k, r in enumerate(c.pop("r")):
                mixed = jnp.where(
                    lane_grp == 0, r[0:CHUNK],
                    jnp.where(lane_grp == 1, r[CHUNK:2 * CHUNK],
                              jnp.where(lane_grp == 2, r[2 * CHUNK:3 * CHUNK], r[3 * CHUNK:4 * CHUNK])))
                out = u[k * CHUNK:(k + 1) * CHUNK, :] * (mixed + sb_ref[...])
                mix_s[pl.ds(r0 + k * CHUNK, CHUNK), HEADS * V_HEAD + CONV_W:D] = out.astype(BF16)

        _skewed([p1_norm, p1_proj, p1_mid, p1_up, p1_out], [dict(r0=i * TB) for i in range(nblk)])

        z = z_s[...]
        pos = lax.broadcasted_iota(I32, (ROW, CONV_W), 0) & (seg - 1)
        zm = jnp.where(pos == 0, 0.0, pltpu.roll(z, 1, 0))
        zp = jnp.where(pos == seg - 1, 0.0, pltpu.roll(z, ROW - 1, 0))
        cw = cw_ref[...]
        conv = zm * cw[0:1, :] + z * cw[1:2, :] + zp * cw[2:3, :]
        mix_s[:, HEADS * V_HEAD:HEADS * V_HEAD + CONV_W] = (gb_s[...] * conv).astype(BF16)

        def at_qk(c):
            keys = kf_s[c["h"], pl.ds(c["k0"], seg + n_past), :]
            c["s"] = _dot_nt(qf_s[c["h"], pl.ds(c["r0"], TQ), :], keys)

        def at_max(c):
            c["m"] = jnp.max(c["s"], axis=-1, keepdims=True)

        def at_exp(c):
            e = jnp.exp2(c.pop("s") - c.pop("m"))
            c["l"] = jnp.sum(e, axis=-1, keepdims=True)
            c["p"] = e.astype(BF16)

        def at_pv(c):
            c["o"] = _dot(c.pop("p"), v_s[c["h"], pl.ds(c["k0"], seg + n_past), :])

        def at_out(c):
            o = c.pop("o") / c.pop("l")
            mix_s[pl.ds(c["r0"], TQ), c["h"] * V_HEAD:(c["h"] + 1) * V_HEAD] = o.astype(BF16)

        at_stages = [at_qk, lambda c: (at_max(c), at_exp(c)), lambda c: (at_pv(c), at_out(c))]
        if ctx:
            _skewed(at_stages, [dict(r0=b * TQ, k0=b * TQ, h=h) for b in range(ROW // TQ) for h in range(HEADS)])
        else:
            def at_block(b, carry):
                r0 = pl.multiple_of(b * (AT_BLOCKS * TQ), AT_BLOCKS * TQ)
                _skewed(at_stages, [dict(r0=r0 + k * TQ, k0=0, h=h) for k in range(AT_BLOCKS) for h in range(HEADS)])
                return carry

            lax.fori_loop(0, ROW // (AT_BLOCKS * TQ), at_block, 0)

        wr = wr_ref[...]
        wr_hi = wr.astype(BF16)
        wr_hl = jnp.concatenate([wr_hi, (wr - wr_hi.astype(F32)).astype(BF16)], axis=0)

        def p2_proj(c):
            c["mo"] = _dot(mix_s[pl.ds(c["r0"], TB), :], wout_ref[...])

        def p2_norm(c):
            r0 = c["r0"]
            xm = x_ref[pl.ds(r0, TB), :] + g1 * c.pop("mo")
            xmid_ref[pl.ds(r0, TB), :] = xm
            h2 = _rms(xm, g2n_ref[...]) * (1.0 + sc2) + sh2
            h2_hi = h2.astype(BF16)
            h2_ref[pl.ds(r0, TB), :] = h2_hi
            c["hi"], c["lo"] = h2_hi, (h2 - h2_hi.astype(F32)).astype(BF16)

        def p2_route(c):
            la = _dot_nt(wr_hl, c["hi"])
            c["lg"] = la[0:N_EXP] + la[N_EXP:2 * N_EXP] + _dot_nt(wr_hi, c.pop("lo"))
            c.pop("hi")

        def p2_aff(c):
            lg = c.pop("lg")
            e = jnp.exp(lg - jnp.max(lg, axis=0, keepdims=True))
            aff_ref[:, pl.ds(c["r0"], TB)] = e / jnp.sum(e, axis=0, keepdims=True)

        _skewed([p2_proj, p2_norm, p2_route, p2_aff], [dict(r0=i * TB) for i in range(nblk)])

    is_ctx = pl.program_id(0) < P_ROWS

    @pl.when(is_ctx)
    def _():
        run(True)

    @pl.when(jnp.logical_not(is_ctx))
    def _():
        run(False)


def _layer_spec(shape):
    nd = len(shape)
    return pl.BlockSpec((None,) + shape, lambda r, l: (l[0],) + (0,) * nd, pipeline_mode=pl.Buffered(1))


def _layer_arg(l):
    return jnp.full((1,), l, I32)


def _mixer_call(l, x, mods, wts, rope_tab, caches):
    lk = PAST_LEN + ROW
    lat = lambda r: jnp.maximum(r - P_ROWS, 0)
    ctx = lambda r: jnp.minimum(r, P_ROWS - 1)
    in_specs = [
        pl.BlockSpec((None, ROW, D), lambda r, l: (r, 0, 0)),
        pl.BlockSpec((None, None, 1, 6 * D),
                     lambda r, l: (l[0], jnp.where(r < P_ROWS, CTX_MOD_ROW, r - P_ROWS), 0, 0)),
        _layer_spec((1, D)), _layer_spec((1, D)),
        _layer_spec((D, C_END)),
        _layer_spec((1, Q_RANK)), _layer_spec((1, KV_RANK)),
        _layer_spec((Q_RANK, U_END)),
        _layer_spec((KV_RANK, HEADS * (QK_NOPE + V_HEAD))),
        _layer_spec((3, CONV_W)),
        _layer_spec((CMLP_G * CHUNK, CHUNK)),
        _layer_spec((CHUNK, CMLP_W)),
        _layer_spec((D, D)),
        _layer_spec((N_EXP, D)),
        pl.BlockSpec((ROW, LANES), lambda r, l: (0, 0), pipeline_mode=pl.Buffered(1)),
        pl.BlockSpec((None, None, PAST_LEN, KV_RANK), lambda r, l: (lat(r), l[0], 0, 0)),
        pl.BlockSpec((None, None, QK_ROPE, PAST_LEN), lambda r, l: (lat(r), l[0], 0, 0)),
    ]
    args = [x, mods, wts["norm1_g"], wts["norm2_g"], wts["w_in"], wts["q_norm_g"], wts["kv_norm_g"],
            wts["w_uq"], wts["w_ukv"], wts["conv_w"], wts["spatial_w"], wts["spatial_b"],
            wts["w_out"], wts["w_router_t"], rope_tab, caches[0], caches[1]]
    out_shape = [jax.ShapeDtypeStruct((N_ROWS, ROW, D), F32),
                 jax.ShapeDtypeStruct((N_ROWS, ROW, D), BF16),
                 jax.ShapeDtypeStruct((N_ROWS, N_EXP, ROW), F32),
                 jax.ShapeDtypeStruct((P_ROWS, ROW, KV_RANK), F32),
                 jax.ShapeDtypeStruct((P_ROWS, ROW, QK_ROPE), F32)]
    out_specs = [pl.BlockSpec((None, ROW, D), lambda r, l: (r, 0, 0)),
                 pl.BlockSpec((None, ROW, D), lambda r, l: (r, 0, 0)),
                 pl.BlockSpec((None, N_EXP, ROW), lambda r, l: (r, 0, 0)),
                 pl.BlockSpec((None, ROW, KV_RANK), lambda r, l: (ctx(r), 0, 0)),
                 pl.BlockSpec((None, ROW, QK_ROPE), lambda r, l: (ctx(r), 0, 0))]
    return pl.pallas_call(
        _mixer_kernel,
        out_shape=out_shape,
        grid_spec=pltpu.PrefetchScalarGridSpec(
            num_scalar_prefetch=1,
            grid=(N_ROWS,),
            in_specs=in_specs,
            out_specs=out_specs,
            scratch_shapes=[
                pltpu.VMEM((HEADS, ROW, QK_PAD), BF16),
                pltpu.VMEM((HEADS, lk, QK_PAD), BF16),
                pltpu.VMEM((HEADS, lk, V_HEAD), BF16),
                pltpu.VMEM((ROW, D), BF16),
                pltpu.VMEM((ROW, CONV_W), F32),
                pltpu.VMEM((ROW, CONV_W), F32),
            ]),
        compiler_params=pltpu.CompilerParams(
            dimension_semantics=("arbitrary",), vmem_limit_bytes=VMEM_LIMIT),
        name="mixer",
    )(_layer_arg(l), *args)


def _cap_thresholds(affs, caps):
    def bit_step(i, ts):
        bit = jnp.left_shift(jnp.int32(1), 30 - i)
        out = []
        for aff, cap, t in zip(affs, caps, ts):
            cand = t | bit
            cnt = jnp.sum(jnp.where(aff >= pltpu.bitcast(cand, F32), 1.0, 0.0), axis=1, keepdims=True)
            out.append(jnp.where(cnt >= float(cap), cand, t))
        return tuple(out)

    ts = lax.fori_loop(0, 31, bit_step, tuple(jnp.zeros((a.shape[0], 1), I32) for a in affs))
    return [pltpu.bitcast(t, F32) for t in ts]


def _select_slots(aff, thr, cap, tri):
    capf = float(cap)
    gt = aff > thr
    eq = aff == thr
    n_gt = jnp.sum(jnp.where(gt, 1.0, 0.0), axis=1, keepdims=True)
    eq_rank = _dot(jnp.where(eq, 1.0, 0.0).astype(BF16), tri)
    sel = jnp.where(gt, 1.0, jnp.where(eq, jnp.where(eq_rank < capf - n_gt, 1.0, 0.0), 0.0))
    pos = _dot(sel.astype(BF16), tri)
    return jnp.where(sel > 0.5, pos.astype(I32), -1)


def _route_kernel(aff_ref, slot_ref, tri_s):
    r = lax.broadcasted_iota(I32, (ROW, ROW), 0)
    c = lax.broadcasted_iota(I32, (ROW, ROW), 1)
    tri_s[...] = jnp.where(r < c, 1.0, 0.0).astype(BF16)
    np_ = P_ROWS * N_EXP
    affs = [aff_ref[0:np_, s * SEQ:(s + 1) * SEQ] for s in range(P_PER_ROW)] + [aff_ref[np_:, :]]
    caps = [CAP_P] * P_PER_ROW + [CAP]
    thrs = _cap_thresholds(affs, caps)
    for s in range(P_PER_ROW):
        sl = _select_slots(affs[s], thrs[s], CAP_P, tri_s[0:SEQ, 0:SEQ])
        slot_ref[0:np_, s * SEQ:(s + 1) * SEQ] = jnp.where(sl >= 0, sl + s * CAP_P, -1)
    slot_ref[np_:, :] = _select_slots(affs[-1], thrs[-1], CAP, tri_s[...])


def _route_call(aff):
    rows = N_ROWS * N_EXP
    return pl.pallas_call(
        _route_kernel,
        out_shape=jax.ShapeDtypeStruct((rows, ROW), I32),
        scratch_shapes=[pltpu.VMEM((ROW, ROW), BF16)],
        compiler_params=pltpu.CompilerParams(vmem_limit_bytes=VMEM_LIMIT),
        name="route",
    )(aff)


GATHER_NC = 256


def _gather_kernel(slot_ref, h2_ref, xs_ref, hot_s):
    @pl.when(pl.program_id(0) >= P_ROWS)
    def _():
        j = lax.broadcasted_iota(I32, (CAP, ROW), 0)
        for e in range(N_EXP):
            hot_s[e * CAP:(e + 1) * CAP, :] = jnp.where(slot_ref[e:e + 1, :] == j, 1.0, 0.0).astype(BF16)
        for c in range(D // GATHER_NC):
            xs = _dot(hot_s[...], h2_ref[:, c * GATHER_NC:(c + 1) * GATHER_NC]).astype(BF16)
            for e in range(N_EXP):
                xs_ref[e, :, c * GATHER_NC:(c + 1) * GATHER_NC] = xs[e * CAP:(e + 1) * CAP, :]

    @pl.when(pl.program_id(0) < P_ROWS)
    def _():
        j = lax.broadcasted_iota(I32, (CAP_P, SEQ), 0)
        for s in range(P_PER_ROW):
            toks = slice(s * SEQ, (s + 1) * SEQ)
            for e in range(N_EXP):
                hot_s[e * CAP_P:(e + 1) * CAP_P, 0:SEQ] = jnp.where(
                    slot_ref[e:e + 1, toks] == j + s * CAP_P, 1.0, 0.0).astype(BF16)
            xs = _dot(hot_s[0:N_EXP * CAP_P, 0:SEQ], h2_ref[toks, :]).astype(BF16)
            for e in range(N_EXP):
                xs_ref[e, s * CAP_P:(s + 1) * CAP_P, :] = xs[e * CAP_P:(e + 1) * CAP_P, :]


def _gather_call(slot, h2):
    return pl.pallas_call(
        _gather_kernel,
        out_shape=jax.ShapeDtypeStruct((N_EXP, N_ROWS, CAP, D), BF16),
        grid=(N_ROWS,),
        in_specs=[
            pl.BlockSpec((N_EXP, ROW), lambda r: (r, 0)),
            pl.BlockSpec((None, ROW, D), lambda r: (r, 0, 0)),
        ],
        out_specs=pl.BlockSpec((N_EXP, None, CAP, D), lambda r: (0, r, 0, 0)),
        scratch_shapes=[pltpu.VMEM((N_EXP * CAP, ROW), BF16)],
        compiler_params=pltpu.CompilerParams(
            dimension_semantics=("arbitrary",), vmem_limit_bytes=VMEM_LIMIT),
        name="gather",
    )(slot, h2)


EXP_ROWS = 4


def _expert_kernel(l_ref, slot_ref, aff_ref, xs_ref, wg_ref, wu_ref, wd_ref, ye_ref, wg_s, wu_s, wd_s):
    e = pl.program_id(0)
    wg_s[...] = wg_ref[...].astype(BF16)
    wu_s[...] = wu_ref[...].astype(BF16)
    wd_s[...] = wd_ref[...].astype(BF16)
    j = lax.broadcasted_iota(I32, (CAP, ROW), 0)

    def gates(c):
        c["gate"] = []
        for k in range(EXP_ROWS):
            row = (c["r0"] + k) * N_EXP + e
            hit = slot_ref[pl.ds(row, 1), :] == j
            c["gate"].append(jnp.sum(jnp.where(hit, aff_ref[pl.ds(row, 1), :], 0.0), axis=1, keepdims=True))

    def up(c):
        xs = xs_ref[c["r0"]:c["r0"] + EXP_ROWS].reshape(EXP_ROWS * CAP, D)
        c["a"], c["u"] = _dot(xs, wg_s[...]), _dot(xs, wu_s[...])

    def act(c):
        c["hid"] = (_silu(c.pop("a")) * c.pop("u")).astype(BF16)

    def down(c):
        c["ye"] = _dot(c.pop("hid"), wd_s[...])

    def out(c):
        ye, gate = c.pop("ye"), c.pop("gate")
        for k in range(EXP_ROWS):
            ye_ref[c["r0"] + k] = (ye[k * CAP:(k + 1) * CAP, :] * gate[k]).astype(BF16)

    _skewed([gates, up, act, down, out], [dict(r0=g * EXP_ROWS) for g in range(N_ROWS // EXP_ROWS)])


def _expert_call(l, slot, aff, xs, w_gate, w_up, w_down):
    rows = N_ROWS * N_EXP
    return pl.pallas_call(
        _expert_kernel,
        out_shape=jax.ShapeDtypeStruct((N_EXP, N_ROWS, CAP, D), BF16),
        grid_spec=pltpu.PrefetchScalarGridSpec(
            num_scalar_prefetch=1,
            grid=(N_EXP,),
            in_specs=[
                pl.BlockSpec((rows, ROW), lambda e, l: (0, 0), pipeline_mode=pl.Buffered(1)),
                pl.BlockSpec((rows, ROW), lambda e, l: (0, 0), pipeline_mode=pl.Buffered(1)),
                pl.BlockSpec((None, N_ROWS, CAP, D), lambda e, l: (e, 0, 0, 0)),
                pl.BlockSpec((None, None, D, FF), lambda e, l: (l[0], e, 0, 0)),
                pl.BlockSpec((None, None, D, FF), lambda e, l: (l[0], e, 0, 0)),
                pl.BlockSpec((None, None, FF, D), lambda e, l: (l[0], e, 0, 0)),
            ],
            out_specs=pl.BlockSpec((None, N_ROWS, CAP, D), lambda e, l: (e, 0, 0, 0)),
            scratch_shapes=[pltpu.VMEM((D, FF), BF16), pltpu.VMEM((D, FF), BF16), pltpu.VMEM((FF, D), BF16)]),
        compiler_params=pltpu.CompilerParams(
            dimension_semantics=("arbitrary",), vmem_limit_bytes=VMEM_LIMIT),
        name="experts",
    )(_layer_arg(l), slot, aff, xs, w_gate, w_up, w_down)


COMB_TC = 256


def _make_combine_kernel(final, first_row):
    def kern(*refs):
        if final:
            _, slot_ref, ye_ref, xmid_ref, g2_ref, fg_ref, o_ref = refs
        else:
            _, slot_ref, ye_ref, xmid_ref, g2_ref, o_ref = refs
        g2 = g2_ref[...]

        def run(ctx):
            nslot = CAP_P if ctx else CAP
            j = lax.broadcasted_iota(I32, (nslot, COMB_TC), 0)
            if not ctx:
                ye = ye_ref[...].reshape(N_EXP * CAP, D)
            for t in range(ROW // COMB_TC):
                cols = slice(t * COMB_TC, (t + 1) * COMB_TC)
                s0 = t * CAP_P if ctx else 0
                onehot = jnp.concatenate(
                    [jnp.where(slot_ref[e:e + 1, cols] == j + s0, 1.0, 0.0).astype(BF16) for e in range(N_EXP)],
                    axis=0)
                if ctx:
                    ye = jnp.concatenate([ye_ref[e, s0:s0 + CAP_P, :] for e in range(N_EXP)], axis=0)
                moe = lax.dot_general(onehot, ye, TN, preferred_element_type=F32)
                xn = xmid_ref[cols, :] + g2 * moe
                if final:
                    xn = _rms(xn, fg_ref[...])
                o_ref[cols, :] = xn

        is_ctx = first_row + pl.program_id(0) < P_ROWS

        @pl.when(is_ctx)
        def _():
            run(True)

        @pl.when(jnp.logical_not(is_ctx))
        def _():
            run(False)

    return kern


def _combine_call(l, slot, ye, xmid, mods, final_g, row0, n_rows, mod_row):
    final = final_g is not None
    in_specs = [
        pl.BlockSpec((N_EXP, ROW), lambda r, l: (row0 + r, 0)),
        pl.BlockSpec((N_EXP, None, CAP, D), lambda r, l: (0, row0 + r, 0, 0)),
        pl.BlockSpec((None, ROW, D), lambda r, l: (row0 + r, 0, 0)),
        pl.BlockSpec((None, None, 1, D), lambda r, l: (l[0], mod_row(r), 0, 5)),
    ]
    args = [slot, ye, xmid, mods]
    if final:
        in_specs.append(pl.BlockSpec((1, D), lambda r, l: (0, 0)))
        args.append(final_g)
    return pl.pallas_call(
        _make_combine_kernel(final, row0),
        out_shape=jax.ShapeDtypeStruct((n_rows, ROW, D), F32),
        grid_spec=pltpu.PrefetchScalarGridSpec(
            num_scalar_prefetch=1,
            grid=(n_rows,),
            in_specs=in_specs,
            out_specs=pl.BlockSpec((None, ROW, D), lambda r, l: (r, 0, 0))),
        compiler_params=pltpu.CompilerParams(
            dimension_semantics=("arbitrary",), vmem_limit_bytes=VMEM_LIMIT),
        name="combine",
    )(_layer_arg(l), *args)


def _rope_rot(w, axis):
    q = QK_ROPE // 4
    part = lambda a, b: lax.slice_in_dim(w, a * q, b * q, axis=axis)
    return jnp.concatenate([-part(1, 2), part(0, 1), -part(3, 4), part(2, 3)], axis=axis)


def _repack_kernel(win_ref, wuq_ref, wout_ref, win_o, wuq_o, wout_o):
    o_conv = Q_RANK + KV_RANK + QK_ROPE
    wt = win_ref[...]
    win_o[...] = jnp.concatenate(
        [wt[:o_conv], _rope_rot(wt[o_conv - QK_ROPE:o_conv], 0), wt[o_conv:]], axis=0).T.astype(BF16)
    u = wuq_ref[...]
    hd = QK_NOPE + QK_ROPE
    pe = [u[:, h * hd + QK_NOPE:(h + 1) * hd] for h in range(HEADS)]
    wuq_o[...] = jnp.concatenate(
        [u[:, h * hd:h * hd + QK_NOPE] for h in range(HEADS)]
        + [piece for h in range(HEADS) for piece in (pe[h], _rope_rot(pe[h], 1))], axis=1).astype(BF16)
    wout_o[...] = wout_ref[...].astype(BF16)


def _repack_call(w_in, w_uq, w_out):
    in_w, uq_w = w_in.shape[-1], w_uq.shape[-1]
    w_in = jnp.swapaxes(w_in, 1, 2)
    return pl.pallas_call(
        _repack_kernel,
        out_shape=[jax.ShapeDtypeStruct((DEPTH, D, C_END), BF16),
                   jax.ShapeDtypeStruct((DEPTH, Q_RANK, U_END), BF16),
                   jax.ShapeDtypeStruct((DEPTH, D, D), BF16)],
        grid=(DEPTH,),
        in_specs=[pl.BlockSpec((None, in_w, D), lambda l: (l, 0, 0)),
                  pl.BlockSpec((None, Q_RANK, uq_w), lambda l: (l, 0, 0)),
                  pl.BlockSpec((None, D, D), lambda l: (l, 0, 0))],
        out_specs=[pl.BlockSpec((None, D, C_END), lambda l: (l, 0, 0)),
                   pl.BlockSpec((None, Q_RANK, U_END), lambda l: (l, 0, 0)),
                   pl.BlockSpec((None, D, D), lambda l: (l, 0, 0))],
        compiler_params=pltpu.CompilerParams(
            dimension_semantics=("arbitrary",), vmem_limit_bytes=VMEM_LIMIT),
        name="repack",
    )(w_in, w_uq, w_out)


def _prep_weights(norm1_g, norm2_g, w_in, q_norm_g, kv_norm_g, w_uq, w_ukv, conv_w, spatial_w,
                  spatial_b, w_out, w_router):
    w_in_p, w_uq_p, w_out_p = _repack_call(w_in, w_uq, w_out)
    sb = jnp.repeat(jnp.swapaxes(spatial_b, 1, 2), CMLP_W // CMLP_G, axis=-1)
    return dict(
        norm1_g=norm1_g.reshape(DEPTH, 1, D), norm2_g=norm2_g.reshape(DEPTH, 1, D),
        w_in=w_in_p, q_norm_g=q_norm_g.reshape(DEPTH, 1, Q_RANK),
        kv_norm_g=kv_norm_g.reshape(DEPTH, 1, KV_RANK), w_uq=w_uq_p, w_ukv=w_ukv.astype(BF16),
        conv_w=conv_w, spatial_w=spatial_w.reshape(DEPTH, CMLP_G * CHUNK, CHUNK).astype(BF16),
        spatial_b=sb, w_out=w_out_p, w_router_t=jnp.swapaxes(w_router, 1, 2))


def _rope_table(n):
    rows = n // GRID_W
    row = jnp.repeat(jnp.arange(rows), GRID_W).astype(F32)
    col = jnp.tile(jnp.arange(GRID_W), rows).astype(F32)
    n_freq = QK_ROPE // 4
    inv = ROPE_BASE ** (-jnp.arange(n_freq, dtype=F32) / n_freq)
    ang_r, ang_c = row[:, None] * inv, col[:, None] * inv
    ang = jnp.concatenate([ang_r, ang_r, ang_c, ang_c], axis=-1)
    return jnp.concatenate([jnp.cos(ang), jnp.sin(ang)], axis=-1)


def kernel(x_prompt, x_sample, cache_ckv, cache_krope, c, c_ctx, w_ada, b_ada, norm1_g, norm2_g, w_in,
           q_norm_g, kv_norm_g, w_uq, w_ukv, conv_w, spatial_w, spatial_b, w_out, w_router, w_gate,
           w_up, w_down, final_norm_g):
    wts = _prep_weights(norm1_g, norm2_g, w_in, q_norm_g, kv_norm_g, w_uq, w_ukv, conv_w, spatial_w,
                        spatial_b, w_out, w_router)
    rope_tab = _rope_table(DEC_SEQ)
    krope_t = jnp.swapaxes(cache_krope, 2, 3)
    cond = jnp.concatenate(
        [c, c_ctx[None, :], jnp.zeros((MOD_ROWS - DEC_BATCH - 1, D), F32)], axis=0)
    mods = _ada_call(cond, w_ada, b_ada).reshape(DEPTH, MOD_ROWS, 1, 6 * D)
    final_g = final_norm_g.reshape(1, D)
    row_mod = lambda r: jnp.where(r < P_ROWS, CTX_MOD_ROW, r - P_ROWS)

    x = jnp.concatenate([x_prompt.reshape(P_ROWS, ROW, D), x_sample], axis=0)
    ckv_list, kpe_list = [], []
    for l in range(DEPTH):
        xmid, h2, aff, ckv, kpe = _mixer_call(l, x, mods, wts, rope_tab, (cache_ckv, krope_t))
        ckv_list.append(ckv.reshape(BATCH, SEQ, KV_RANK))
        kpe_list.append(kpe.reshape(BATCH, SEQ, QK_ROPE))
        aff = aff.reshape(N_ROWS * N_EXP, ROW)
        slot = _route_call(aff)
        ye = _expert_call(l, slot, aff, _gather_call(slot, h2), w_gate, w_up, w_down)
        if l < DEPTH - 1:
            x = _combine_call(l, slot, ye, xmid, mods, None, 0, N_ROWS, row_mod)
        else:
            y_prompt = _combine_call(l, slot, ye, xmid, mods, final_g, 0, P_ROWS, lambda r: CTX_MOD_ROW)
            y_sample = _combine_call(l, slot, ye, xmid, mods, final_g, P_ROWS, DEC_BATCH, lambda r: r)
    return (y_prompt.reshape(BATCH, SEQ, D), y_sample,
            jnp.stack(ckv_list, axis=1), jnp.stack(kpe_list, axis=1))
```

```python
import jax
import jax.numpy as jnp
from jax import lax
from jax.experimental import pallas as pl
from jax.experimental.pallas import tpu as pltpu

F32 = jnp.float32
BF16 = jnp.bfloat16
I32 = jnp.int32

D = 1024
BATCH = 16
SEQ = 256
DEPTH = 4
DEC_BATCH = 8
DEC_SEQ = 1024
PAST_LEN = 512
GRID_W = 64
HEADS = 4
QK_NOPE = 128
QK_ROPE = 64
V_HEAD = 128
Q_RANK = 256
KV_RANK = 128
CONV_W = 256
CMLP_W = 256
CMLP_G = 4
CHUNK = 128
N_EXP = 16
EC_FACTOR = 2
FF = D // 2
ROPE_BASE = 10000.0
EPS = 1e-6
LOG2_E = 1.4426950408889634

LANES = 128
ROW = 1024
N_ROWS = (BATCH * SEQ + DEC_BATCH * DEC_SEQ) // ROW
P_ROWS = BATCH * SEQ // ROW
P_PER_ROW = ROW // SEQ
CAP = EC_FACTOR * ROW // N_EXP
CAP_P = EC_FACTOR * SEQ // N_EXP
CTX_MOD_ROW = DEC_BATCH
MOD_ROWS = 16
QK_PAD = QK_NOPE + LANES

C_Q, C_KV, C_KPE, C_CONV, C_CMLP, C_END = 0, 256, 384, 512, 1280, 1792
U_NOPE, U_PE, U_END = 0, HEADS * QK_NOPE, HEADS * (QK_NOPE + LANES)
assert 2 * QK_ROPE == LANES
TB = 256
TQ = 256
AT_BLOCKS = 1
VMEM_LIMIT = 56 * 1024 * 1024

NT = (((1,), (1,)), ((), ()))
TN = (((0,), (0,)), ((), ()))


def _rms(x, g):
    return x * lax.rsqrt(jnp.mean(x * x, axis=-1, keepdims=True) + EPS) * g


def _silu(x):
    return x / (1.0 + jnp.exp(-x))


def _dot(a, b):
    return jnp.dot(a, b, preferred_element_type=F32)


def _dot_nt(a, b):
    return lax.dot_general(a, b, NT, preferred_element_type=F32)


def _skewed(stages, items):
    n = len(stages)
    for t in range(len(items) + n - 1):
        for s in range(n - 1, -1, -1):
            if 0 <= t - s < len(items):
                stages[s](items[t - s])


def _ada_kernel(c_ref, w_ref, b_ref, o_ref):
    s = _silu(c_ref[...]).astype(BF16)
    o_ref[...] = _dot(s, w_ref[...].astype(BF16)) + b_ref[...]


def _ada_call(cond, w_ada, b_ada):
    nc = 6
    return pl.pallas_call(
        _ada_kernel,
        out_shape=jax.ShapeDtypeStruct((DEPTH, MOD_ROWS, 6 * D), F32),
        grid=(DEPTH, nc),
        in_specs=[
            pl.BlockSpec((MOD_ROWS, D), lambda l, j: (0, 0)),
            pl.BlockSpec((None, D, D), lambda l, j: (l, 0, j)),
            pl.BlockSpec((None, 1, D), lambda l, j: (l, 0, j)),
        ],
        out_specs=pl.BlockSpec((None, MOD_ROWS, D), lambda l, j: (l, 0, j)),
        compiler_params=pltpu.CompilerParams(
            dimension_semantics=("arbitrary", "arbitrary"), vmem_limit_bytes=VMEM_LIMIT),
        name="ada_mod",
    )(cond, w_ada, b_ada.reshape(DEPTH, 1, 6 * D))


def _make_mixer_kernel(split_x):
    def kern(l_ref, *refs):
        if split_x:
            _mixer_body(*refs)
        else:
            _mixer_body(refs[0], *refs)
    return kern


def _mixer_body(xp_ref, xs_ref, mod_ref, g1n_ref, g2n_ref, win_ref, qg_ref, kvg_ref, wuq_ref, wukv_ref,
                cw_ref, sw_ref, sb_ref, wout_ref, wr_ref, cs_ref, cckv_ref, ckr_ref,
                xmid_ref, h2_ref, aff_ref, ckv_out, kpe_out,
                qf_s, kf_s, v_s, mix_s, gb_s, z_s):
    qk_scale = float(QK_NOPE + QK_ROPE) ** -0.5 * LOG2_E
    nblk = ROW // TB

    def run(ctx):
        x_ref = xp_ref if ctx else xs_ref
        n_past = 0 if ctx else PAST_LEN
        seg = SEQ if ctx else ROW
        mod = mod_ref[...]
        sh1, sc1, g1 = mod[:, 0:D], mod[:, D:2 * D], mod[:, 2 * D:3 * D]
        sh2, sc2, g2 = mod[:, 3 * D:4 * D], mod[:, 4 * D:5 * D], mod[:, 5 * D:6 * D]

        def store_kv(row0, rows, kv, kpb):
            for h in range(HEADS):
                c0 = h * (QK_NOPE + V_HEAD)
                kf_s[h, pl.ds(row0, rows), 0:QK_NOPE] = kv[:, c0:c0 + QK_NOPE].astype(BF16)
                kf_s[h, pl.ds(row0, rows), QK_NOPE:QK_PAD] = kpb
                v_s[h, pl.ds(row0, rows), :] = kv[:, c0 + QK_NOPE:c0 + QK_NOPE + V_HEAD].astype(BF16)

        if not ctx:
            kr_t = ckr_ref[...]
            store_kv(0, PAST_LEN, _dot(cckv_ref[...].astype(BF16), wukv_ref[...]),
                     jnp.concatenate([kr_t, jnp.zeros_like(kr_t)], axis=0).T.astype(BF16))

        grp_shift = (CMLP_W // CMLP_G).bit_length() - 1
        lane_grp = lax.shift_right_logical(lax.broadcasted_iota(I32, (CHUNK, CMLP_W), 1), grp_shift)

        def p1_norm(c):
            x = x_ref[pl.ds(c["r0"], TB), :]
            c["hb"] = (_rms(x, g1n_ref[...]) * (1.0 + sc1) + sh1).astype(BF16)

        def p1_proj(c):
            c["pa"] = _dot(c["hb"], win_ref[:, C_Q:C_CONV])
            c["cv"] = _dot(c["hb"], win_ref[:, C_CONV:C_CMLP])
            c["cm"] = _dot(c.pop("hb"), win_ref[:, C_CMLP:C_END])

        def p1_mid(c):
            r0, pa, cv, cm = c["r0"], c.pop("pa"), c.pop("cv"), c.pop("cm")
            kp = pa[:, C_KPE:C_CONV]
            c["cqn"] = _rms(pa[:, C_Q:C_KV], qg_ref[...]).astype(BF16)
            ckv_n = _rms(pa[:, C_KV:C_KPE], kvg_ref[...])
            if ctx:
                ckv_out[pl.ds(r0, TB), :] = ckv_n
                kpe_out[pl.ds(r0, TB), :] = kp[:, 0:QK_ROPE]
                kpe = kp
            else:
                y = kp * cs_ref[pl.ds(r0, TB), :]
                kpe = y + pltpu.roll(y, QK_ROPE, 1)
            c["ckv_b"], c["kpe_b"] = ckv_n.astype(BF16), kpe.astype(BF16)
            gb_s[pl.ds(r0, TB), :] = cv[:, 0:CONV_W]
            z_s[pl.ds(r0, TB), :] = cv[:, CONV_W:2 * CONV_W] * cv[:, 2 * CONV_W:3 * CONV_W]
            c["u"], c["vvb"] = cm[:, 0:CMLP_W], cm[:, CMLP_W:2 * CMLP_W].astype(BF16)

        def p1_up(c):
            c["qa"] = _dot(c.pop("cqn"), wuq_ref[...])
            c["kv"] = _dot(c.pop("ckv_b"), wukv_ref[...])
            vvb = c.pop("vvb")
            c["r"] = [_dot(sw_ref[...], vvb[k * CHUNK:(k + 1) * CHUNK, :]) for k in range(TB // CHUNK)]

        def p1_out(c):
            r0, qa, u = c["r0"], c.pop("qa"), c.pop("u")
            rope_lanes = lax.broadcasted_iota(I32, (TB, LANES), 1) < QK_ROPE
            for h in range(HEADS):
                qn = qa[:, U_NOPE + h * QK_NOPE:U_NOPE + (h + 1) * QK_NOPE]
                qp = qa[:, U_PE + h * LANES:U_PE + (h + 1) * LANES]
                if not ctx:
                    y = qp * cs_ref[pl.ds(r0, TB), :]
                    qp = y + pltpu.roll(y, QK_ROPE, 1)
                qp = jnp.where(rope_lanes, qp, 0.0)
                qf_s[h, pl.ds(r0, TB), 0:QK_NOPE] = (qn * qk_scale).astype(BF16)
                qf_s[h, pl.ds(r0, TB), QK_NOPE:QK_PAD] = (qp * qk_scale).astype(BF16)
            store_kv(n_past + r0, TB, c.pop("kv"), c.pop("kpe_b"))
            for k, r in enumerate(c.pop("r")):
                mixed = jnp.where(
                    lane_grp == 0, r[0:CHUNK],
                    jnp.where(lane_grp == 1, r[CHUNK:2 * CHUNK],
                              jnp.where(lane_grp == 2, r[2 * CHUNK:3 * CHUNK], r[3 * CHUNK:4 * CHUNK])))
                out = u[k * CHUNK:(k + 1) * CHUNK, :] * (mixed + sb_ref[...])
                mix_s[pl.ds(r0 + k * CHUNK, CHUNK), HEADS * V_HEAD + CONV_W:D] = out.astype(BF16)

        _skewed([p1_norm, p1_proj, p1_mid, p1_up, p1_out], [dict(r0=i * TB) for i in range(nblk)])

        z = z_s[...]
        pos = lax.broadcasted_iota(I32, (ROW, CONV_W), 0) & (seg - 1)
        zm = jnp.where(pos == 0, 0.0, pltpu.roll(z, 1, 0))
        zp = jnp.where(pos == seg - 1, 0.0, pltpu.roll(z, ROW - 1, 0))
        cw = cw_ref[...]
        conv = zm * cw[0:1, :] + z * cw[1:2, :] + zp * cw[2:3, :]
        mix_s[:, HEADS * V_HEAD:HEADS * V_HEAD + CONV_W] = (gb_s[...] * conv).astype(BF16)

        def at_qk(c):
            keys = kf_s[c["h"], pl.ds(c["k0"], seg + n_past), :]
            c["s"] = _dot_nt(qf_s[c["h"], pl.ds(c["r0"], TQ), :], keys)

        def at_max(c):
            c["m"] = jnp.max(c["s"], axis=-1, keepdims=True)

        def at_exp(c):
            e = jnp.exp2(c.pop("s") - c.pop("m"))
            c["l"] = jnp.sum(e, axis=-1, keepdims=True)
            c["p"] = e.astype(BF16)

        def at_pv(c):
            c["o"] = _dot(c.pop("p"), v_s[c["h"], pl.ds(c["k0"], seg + n_past), :])

        def at_out(c):
            o = c.pop("o") / c.pop("l")
            mix_s[pl.ds(c["r0"], TQ), c["h"] * V_HEAD:(c["h"] + 1) * V_HEAD] = o.astype(BF16)

        at_stages = [at_qk, lambda c: (at_max(c), at_exp(c)), lambda c: (at_pv(c), at_out(c))]
        if ctx:
            _skewed(at_stages, [dict(r0=b * TQ, k0=b * TQ, h=h) for b in range(ROW // TQ) for h in range(HEADS)])
        else:
            def at_block(b, carry):
                r0 = pl.multiple_of(b * (AT_BLOCKS * TQ), AT_BLOCKS * TQ)
                _skewed(at_stages, [dict(r0=r0 + k * TQ, k0=0, h=h) for k in range(AT_BLOCKS) for h in range(HEADS)])
                return carry

            lax.fori_loop(0, ROW // (AT_BLOCKS * TQ), at_block, 0)

        wr = wr_ref[...]
        wr_hi = wr.astype(BF16)
        wr_hl = jnp.concatenate([wr_hi, (wr - wr_hi.astype(F32)).astype(BF16)], axis=0)

        def p2_proj(c):
            c["mo"] = _dot(mix_s[pl.ds(c["r0"], TB), :], wout_ref[...])

        def p2_norm(c):
            r0 = c["r0"]
            xm = x_ref[pl.ds(r0, TB), :] + g1 * c.pop("mo")
            xmid_ref[pl.ds(r0, TB), :] = xm
            h2 = _rms(xm, g2n_ref[...]) * (1.0 + sc2) + sh2
            h2_hi = h2.astype(BF16)
            h2_ref[pl.ds(r0, TB), :] = h2_hi
            c["hi"], c["lo"] = h2_hi, (h2 - h2_hi.astype(F32)).astype(BF16)

        def p2_route(c):
            la = _dot_nt(wr_hl, c["hi"])
            c["lg"] = la[0:N_EXP] + la[N_EXP:2 * N_EXP] + _dot_nt(wr_hi, c.pop("lo"))
            c.pop("hi")

        def p2_aff(c):
            lg = c.pop("lg")
            e = jnp.exp(lg - jnp.max(lg, axis=0, keepdims=True))
            aff_ref[:, pl.ds(c["r0"], TB)] = e / jnp.sum(e, axis=0, keepdims=True)

        _skewed([p2_proj, p2_norm, p2_route, p2_aff], [dict(r0=i * TB) for i in range(nblk)])

    is_ctx = pl.program_id(0) < P_ROWS

    @pl.when(is_ctx)
    def _():
        run(True)

    @pl.when(jnp.logical_not(is_ctx))
    def _():
        run(False)


def _layer_spec(shape):
    nd = len(shape)
    return pl.BlockSpec((None,) + shape, lambda r, l: (l[0],) + (0,) * nd, pipeline_mode=pl.Buffered(1))


def _layer_arg(l):
    return jnp.full((1,), l, I32)


def _mixer_call(l, x, mods, wts, rope_tab, caches):
    lk = PAST_LEN + ROW
    lat = lambda r: jnp.maximum(r - P_ROWS, 0)
    ctx = lambda r: jnp.minimum(r, P_ROWS - 1)
    split_x = isinstance(x, tuple)
    if split_x:
        x_specs = [pl.BlockSpec((None, ROW, D), lambda r, l: (ctx(r), 0, 0), pipeline_mode=pl.Buffered(1)),
                   pl.BlockSpec((None, ROW, D), lambda r, l: (lat(r), 0, 0))]
    else:
        x_specs, x = [pl.BlockSpec((None, ROW, D), lambda r, l: (r, 0, 0))], (x,)
    in_specs = x_specs + [
        pl.BlockSpec((None, None, 1, 6 * D),
                     lambda r, l: (l[0], jnp.where(r < P_ROWS, CTX_MOD_ROW, r - P_ROWS), 0, 0)),
        _layer_spec((1, D)), _layer_spec((1, D)),
        _layer_spec((D, C_END)),
        _layer_spec((1, Q_RANK)), _layer_spec((1, KV_RANK)),
        _layer_spec((Q_RANK, U_END)),
        _layer_spec((KV_RANK, HEADS * (QK_NOPE + V_HEAD))),
        _layer_spec((3, CONV_W)),
        _layer_spec((CMLP_G * CHUNK, CHUNK)),
        _layer_spec((CHUNK, CMLP_W)),
        _layer_spec((D, D)),
        _layer_spec((N_EXP, D)),
        pl.BlockSpec((ROW, LANES), lambda r, l: (0, 0), pipeline_mode=pl.Buffered(1)),
        pl.BlockSpec((None, None, PAST_LEN, KV_RANK), lambda r, l: (lat(r), l[0], 0, 0)),
        pl.BlockSpec((None, None, QK_ROPE, PAST_LEN), lambda r, l: (lat(r), l[0], 0, 0)),
    ]
    args = list(x) + [mods, wts["norm1_g"], wts["norm2_g"], wts["w_in"], wts["q_norm_g"], wts["kv_norm_g"],
            wts["w_uq"], wts["w_ukv"], wts["conv_w"], wts["spatial_w"], wts["spatial_b"],
            wts["w_out"], wts["w_router_t"], rope_tab, caches[0], caches[1]]
    out_shape = [jax.ShapeDtypeStruct((N_ROWS, ROW, D), F32),
                 jax.ShapeDtypeStruct((N_ROWS, ROW, D), BF16),
                 jax.ShapeDtypeStruct((N_ROWS, N_EXP, ROW), F32),
                 jax.ShapeDtypeStruct((P_ROWS, ROW, KV_RANK), F32),
                 jax.ShapeDtypeStruct((P_ROWS, ROW, QK_ROPE), F32)]
    out_specs = [pl.BlockSpec((None, ROW, D), lambda r, l: (r, 0, 0)),
                 pl.BlockSpec((None, ROW, D), lambda r, l: (r, 0, 0)),
                 pl.BlockSpec((None, N_EXP, ROW), lambda r, l: (r, 0, 0)),
                 pl.BlockSpec((None, ROW, KV_RANK), lambda r, l: (ctx(r), 0, 0)),
                 pl.BlockSpec((None, ROW, QK_ROPE), lambda r, l: (ctx(r), 0, 0))]
    return pl.pallas_call(
        _make_mixer_kernel(split_x),
        out_shape=out_shape,
        grid_spec=pltpu.PrefetchScalarGridSpec(
            num_scalar_prefetch=1,
            grid=(N_ROWS,),
            in_specs=in_specs,
            out_specs=out_specs,
            scratch_shapes=[
                pltpu.VMEM((HEADS, ROW, QK_PAD), BF16),
                pltpu.VMEM((HEADS, lk, QK_PAD), BF16),
                pltpu.VMEM((HEADS, lk, V_HEAD), BF16),
                pltpu.VMEM((ROW, D), BF16),
                pltpu.VMEM((ROW, CONV_W), F32),
                pltpu.VMEM((ROW, CONV_W), F32),
            ]),
        compiler_params=pltpu.CompilerParams(
            dimension_semantics=("arbitrary",), vmem_limit_bytes=VMEM_LIMIT),
        name="mixer",
    )(_layer_arg(l), *args)


def _cap_thresholds(affs, caps):
    def bit_step(i, ts):
        bit = jnp.left_shift(jnp.int32(1), 30 - i)
        out = []
        for aff, cap, t in zip(affs, caps, ts):
            cand = t | bit
            cnt = jnp.sum(jnp.where(aff >= pltpu.bitcast(cand, F32), 1.0, 0.0), axis=1, keepdims=True)
            out.append(jnp.where(cnt >= float(cap), cand, t))
        return tuple(out)

    ts = lax.fori_loop(0, 31, bit_step, tuple(jnp.zeros((a.shape[0], 1), I32) for a in affs))
    return [pltpu.bitcast(t, F32) for t in ts]


def _select_slots(aff, thr, cap, tri):
    capf = float(cap)
    gt = aff > thr
    eq = aff == thr
    n_gt = jnp.sum(jnp.where(gt, 1.0, 0.0), axis=1, keepdims=True)
    eq_rank = _dot(jnp.where(eq, 1.0, 0.0).astype(BF16), tri)
    sel = jnp.where(gt, 1.0, jnp.where(eq, jnp.where(eq_rank < capf - n_gt, 1.0, 0.0), 0.0))
    pos = _dot(sel.astype(BF16), tri)
    return jnp.where(sel > 0.5, pos.astype(I32), -1)


def _route_kernel(aff_ref, slot_ref, tri_s):
    r = lax.broadcasted_iota(I32, (ROW, ROW), 0)
    c = lax.broadcasted_iota(I32, (ROW, ROW), 1)
    tri_s[...] = jnp.where(r < c, 1.0, 0.0).astype(BF16)
    np_ = P_ROWS * N_EXP
    affs = [aff_ref[0:np_, s * SEQ:(s + 1) * SEQ] for s in range(P_PER_ROW)] + [aff_ref[np_:, :]]
    caps = [CAP_P] * P_PER_ROW + [CAP]
    thrs = _cap_thresholds(affs, caps)
    for s in range(P_PER_ROW):
        sl = _select_slots(affs[s], thrs[s], CAP_P, tri_s[0:SEQ, 0:SEQ])
        slot_ref[0:np_, s * SEQ:(s + 1) * SEQ] = jnp.where(sl >= 0, sl + s * CAP_P, -1)
    slot_ref[np_:, :] = _select_slots(affs[-1], thrs[-1], CAP, tri_s[...])


def _route_call(aff):
    rows = N_ROWS * N_EXP
    return pl.pallas_call(
        _route_kernel,
        out_shape=jax.ShapeDtypeStruct((rows, ROW), I32),
        scratch_shapes=[pltpu.VMEM((ROW, ROW), BF16)],
        compiler_params=pltpu.CompilerParams(vmem_limit_bytes=VMEM_LIMIT),
        name="route",
    )(aff)


GATHER_NC = 256


def _gather_kernel(slot_ref, h2_ref, xs_ref, hot_s):
    @pl.when(pl.program_id(0) >= P_ROWS)
    def _():
        j = lax.broadcasted_iota(I32, (CAP, ROW), 0)
        for e in range(N_EXP):
            hot_s[e * CAP:(e + 1) * CAP, :] = jnp.where(slot_ref[e:e + 1, :] == j, 1.0, 0.0).astype(BF16)
        for c in range(D // GATHER_NC):
            xs = _dot(hot_s[...], h2_ref[:, c * GATHER_NC:(c + 1) * GATHER_NC]).astype(BF16)
            for e in range(N_EXP):
                xs_ref[e, :, c * GATHER_NC:(c + 1) * GATHER_NC] = xs[e * CAP:(e + 1) * CAP, :]

    @pl.when(pl.program_id(0) < P_ROWS)
    def _():
        j = lax.broadcasted_iota(I32, (CAP_P, SEQ), 0)
        for s in range(P_PER_ROW):
            toks = slice(s * SEQ, (s + 1) * SEQ)
            for e in range(N_EXP):
                hot_s[e * CAP_P:(e + 1) * CAP_P, 0:SEQ] = jnp.where(
                    slot_ref[e:e + 1, toks] == j + s * CAP_P, 1.0, 0.0).astype(BF16)
            xs = _dot(hot_s[0:N_EXP * CAP_P, 0:SEQ], h2_ref[toks, :]).astype(BF16)
            for e in range(N_EXP):
                xs_ref[e, s * CAP_P:(s + 1) * CAP_P, :] = xs[e * CAP_P:(e + 1) * CAP_P, :]


def _gather_call(slot, h2):
    return pl.pallas_call(
        _gather_kernel,
        out_shape=jax.ShapeDtypeStruct((N_EXP, N_ROWS, CAP, D), BF16),
        grid=(N_ROWS,),
        in_specs=[
            pl.BlockSpec((N_EXP, ROW), lambda r: (r, 0)),
            pl.BlockSpec((None, ROW, D), lambda r: (r, 0, 0)),
        ],
        out_specs=pl.BlockSpec((N_EXP, None, CAP, D), lambda r: (0, r, 0, 0)),
        scratch_shapes=[pltpu.VMEM((N_EXP * CAP, ROW), BF16)],
        compiler_params=pltpu.CompilerParams(
            dimension_semantics=("arbitrary",), vmem_limit_bytes=VMEM_LIMIT),
        name="gather",
    )(slot, h2)


EXP_ROWS = 4


def _expert_kernel(l_ref, slot_ref, aff_ref, xs_ref, wg_ref, wu_ref, wd_ref, ye_ref, wg_s, wu_s, wd_s):
    e = pl.program_id(0)
    wg_s[...] = wg_ref[...].astype(BF16)
    wu_s[...] = wu_ref[...].astype(BF16)
    wd_s[...] = wd_ref[...].astype(BF16)
    j = lax.broadcasted_iota(I32, (CAP, ROW), 0)

    def gates(c):
        c["gate"] = []
        for k in range(EXP_ROWS):
            row = (c["r0"] + k) * N_EXP + e
            hit = slot_ref[pl.ds(row, 1), :] == j
            c["gate"].append(jnp.sum(jnp.where(hit, aff_ref[pl.ds(row, 1), :], 0.0), axis=1, keepdims=True))

    def up(c):
        xs = xs_ref[c["r0"]:c["r0"] + EXP_ROWS].reshape(EXP_ROWS * CAP, D)
        c["a"], c["u"] = _dot(xs, wg_s[...]), _dot(xs, wu_s[...])

    def act(c):
        c["hid"] = (_silu(c.pop("a")) * c.pop("u")).astype(BF16)

    def down(c):
        c["ye"] = _dot(c.pop("hid"), wd_s[...])

    def out(c):
        ye, gate = c.pop("ye"), c.pop("gate")
        for k in range(EXP_ROWS):
            ye_ref[c["r0"] + k] = (ye[k * CAP:(k + 1) * CAP, :] * gate[k]).astype(BF16)

    _skewed([gates, up, act, down, out], [dict(r0=g * EXP_ROWS) for g in range(N_ROWS // EXP_ROWS)])


def _expert_call(l, slot, aff, xs, w_gate, w_up, w_down):
    rows = N_ROWS * N_EXP
    return pl.pallas_call(
        _expert_kernel,
        out_shape=jax.ShapeDtypeStruct((N_EXP, N_ROWS, CAP, D), BF16),
        grid_spec=pltpu.PrefetchScalarGridSpec(
            num_scalar_prefetch=1,
            grid=(N_EXP,),
            in_specs=[
                pl.BlockSpec((rows, ROW), lambda e, l: (0, 0), pipeline_mode=pl.Buffered(1)),
                pl.BlockSpec((rows, ROW), lambda e, l: (0, 0), pipeline_mode=pl.Buffered(1)),
                pl.BlockSpec((None, N_ROWS, CAP, D), lambda e, l: (e, 0, 0, 0)),
                pl.BlockSpec((None, None, D, FF), lambda e, l: (l[0], e, 0, 0)),
                pl.BlockSpec((None, None, D, FF), lambda e, l: (l[0], e, 0, 0)),
                pl.BlockSpec((None, None, FF, D), lambda e, l: (l[0], e, 0, 0)),
            ],
            out_specs=pl.BlockSpec((None, N_ROWS, CAP, D), lambda e, l: (e, 0, 0, 0)),
            scratch_shapes=[pltpu.VMEM((D, FF), BF16), pltpu.VMEM((D, FF), BF16), pltpu.VMEM((FF, D), BF16)]),
        compiler_params=pltpu.CompilerParams(
            dimension_semantics=("arbitrary",), vmem_limit_bytes=VMEM_LIMIT),
        name="experts",
    )(_layer_arg(l), slot, aff, xs, w_gate, w_up, w_down)


COMB_TC = 256


def _make_combine_kernel(final, first_row):
    def kern(*refs):
        if final:
            _, slot_ref, ye_ref, xmid_ref, g2_ref, fg_ref, o_ref = refs
        else:
            _, slot_ref, ye_ref, xmid_ref, g2_ref, o_ref = refs
        g2 = g2_ref[...]

        def run(ctx):
            nslot = CAP_P if ctx else CAP
            j = lax.broadcasted_iota(I32, (nslot, COMB_TC), 0)
            if not ctx:
                ye = ye_ref[...].reshape(N_EXP * CAP, D)
            for t in range(ROW // COMB_TC):
                cols = slice(t * COMB_TC, (t + 1) * COMB_TC)
                s0 = t * CAP_P if ctx else 0
                onehot = jnp.concatenate(
                    [jnp.where(slot_ref[e:e + 1, cols] == j + s0, 1.0, 0.0).astype(BF16) for e in range(N_EXP)],
                    axis=0)
                if ctx:
                    ye = jnp.concatenate([ye_ref[e, s0:s0 + CAP_P, :] for e in range(N_EXP)], axis=0)
                moe = lax.dot_general(onehot, ye, TN, preferred_element_type=F32)
                xn = xmid_ref[cols, :] + g2 * moe
                if final:
                    xn = _rms(xn, fg_ref[...])
                o_ref[cols, :] = xn

        is_ctx = first_row + pl.program_id(0) < P_ROWS

        @pl.when(is_ctx)
        def _():
            run(True)

        @pl.when(jnp.logical_not(is_ctx))
        def _():
            run(False)

    return kern


def _combine_call(l, slot, ye, xmid, mods, final_g, row0, n_rows, mod_row):
    final = final_g is not None
    in_specs = [
        pl.BlockSpec((N_EXP, ROW), lambda r, l: (row0 + r, 0)),
        pl.BlockSpec((N_EXP, None, CAP, D), lambda r, l: (0, row0 + r, 0, 0)),
        pl.BlockSpec((None, ROW, D), lambda r, l: (row0 + r, 0, 0)),
        pl.BlockSpec((None, None, 1, D), lambda r, l: (l[0], mod_row(r), 0, 5)),
    ]
    args = [slot, ye, xmid, mods]
    if final:
        in_specs.append(pl.BlockSpec((1, D), lambda r, l: (0, 0)))
        args.append(final_g)
    return pl.pallas_call(
        _make_combine_kernel(final, row0),
        out_shape=jax.ShapeDtypeStruct((n_rows, ROW, D), F32),
        grid_spec=pltpu.PrefetchScalarGridSpec(
            num_scalar_prefetch=1,
            grid=(n_rows,),
            in_specs=in_specs,
            out_specs=pl.BlockSpec((None, ROW, D), lambda r, l: (r, 0, 0))),
        compiler_params=pltpu.CompilerParams(
            dimension_semantics=("arbitrary",), vmem_limit_bytes=VMEM_LIMIT),
        name="combine",
    )(_layer_arg(l), *args)


def _rope_rot(w, axis):
    q = QK_ROPE // 4
    part = lambda a, b: lax.slice_in_dim(w, a * q, b * q, axis=axis)
    return jnp.concatenate([-part(1, 2), part(0, 1), -part(3, 4), part(2, 3)], axis=axis)


def _repack_kernel(win_ref, wuq_ref, wout_ref, win_o, wuq_o, wout_o):
    o_conv = Q_RANK + KV_RANK + QK_ROPE
    wt = win_ref[...]
    win_o[...] = jnp.concatenate(
        [wt[:o_conv], _rope_rot(wt[o_conv - QK_ROPE:o_conv], 0), wt[o_conv:]], axis=0).T.astype(BF16)
    u = wuq_ref[...]
    hd = QK_NOPE + QK_ROPE
    pe = [u[:, h * hd + QK_NOPE:(h + 1) * hd] for h in range(HEADS)]
    wuq_o[...] = jnp.concatenate(
        [u[:, h * hd:h * hd + QK_NOPE] for h in range(HEADS)]
        + [piece for h in range(HEADS) for piece in (pe[h], _rope_rot(pe[h], 1))], axis=1).astype(BF16)
    wout_o[...] = wout_ref[...].astype(BF16)


def _repack_call(w_in, w_uq, w_out):
    in_w, uq_w = w_in.shape[-1], w_uq.shape[-1]
    w_in = jnp.swapaxes(w_in, 1, 2)
    return pl.pallas_call(
        _repack_kernel,
        out_shape=[jax.ShapeDtypeStruct((DEPTH, D, C_END), BF16),
                   jax.ShapeDtypeStruct((DEPTH, Q_RANK, U_END), BF16),
                   jax.ShapeDtypeStruct((DEPTH, D, D), BF16)],
        grid=(DEPTH,),
        in_specs=[pl.BlockSpec((None, in_w, D), lambda l: (l, 0, 0)),
                  pl.BlockSpec((None, Q_RANK, uq_w), lambda l: (l, 0, 0)),
                  pl.BlockSpec((None, D, D), lambda l: (l, 0, 0))],
        out_specs=[pl.BlockSpec((None, D, C_END), lambda l: (l, 0, 0)),
                   pl.BlockSpec((None, Q_RANK, U_END), lambda l: (l, 0, 0)),
                   pl.BlockSpec((None, D, D), lambda l: (l, 0, 0))],
        compiler_params=pltpu.CompilerParams(
            dimension_semantics=("arbitrary",), vmem_limit_bytes=VMEM_LIMIT),
        name="repack",
    )(w_in, w_uq, w_out)


def _prep_weights(norm1_g, norm2_g, w_in, q_norm_g, kv_norm_g, w_uq, w_ukv, conv_w, spatial_w,
                  spatial_b, w_out, w_router):
    w_in_p, w_uq_p, w_out_p = _repack_call(w_in, w_uq, w_out)
    sb = jnp.repeat(jnp.swapaxes(spatial_b, 1, 2), CMLP_W // CMLP_G, axis=-1)
    return dict(
        norm1_g=norm1_g.reshape(DEPTH, 1, D), norm2_g=norm2_g.reshape(DEPTH, 1, D),
        w_in=w_in_p, q_norm_g=q_norm_g.reshape(DEPTH, 1, Q_RANK),
        kv_norm_g=kv_norm_g.reshape(DEPTH, 1, KV_RANK), w_uq=w_uq_p, w_ukv=w_ukv.astype(BF16),
        conv_w=conv_w, spatial_w=spatial_w.reshape(DEPTH, CMLP_G * CHUNK, CHUNK).astype(BF16),
        spatial_b=sb, w_out=w_out_p, w_router_t=jnp.swapaxes(w_router, 1, 2))


def _rope_table(n):
    rows = n // GRID_W
    row = jnp.repeat(jnp.arange(rows), GRID_W).astype(F32)
    col = jnp.tile(jnp.arange(GRID_W), rows).astype(F32)
    n_freq = QK_ROPE // 4
    inv = ROPE_BASE ** (-jnp.arange(n_freq, dtype=F32) / n_freq)
    ang_r, ang_c = row[:, None] * inv, col[:, None] * inv
    ang = jnp.concatenate([ang_r, ang_r, ang_c, ang_c], axis=-1)
    return jnp.concatenate([jnp.cos(ang), jnp.sin(ang)], axis=-1)


def kernel(x_prompt, x_sample, cache_ckv, cache_krope, c, c_ctx, w_ada, b_ada, norm1_g, norm2_g, w_in,
           q_norm_g, kv_norm_g, w_uq, w_ukv, conv_w, spatial_w, spatial_b, w_out, w_router, w_gate,
           w_up, w_down, final_norm_g):
    wts = _prep_weights(norm1_g, norm2_g, w_in, q_norm_g, kv_norm_g, w_uq, w_ukv, conv_w, spatial_w,
                        spatial_b, w_out, w_router)
    rope_tab = _rope_table(DEC_SEQ)
    krope_t = jnp.swapaxes(cache_krope, 2, 3)
    cond = jnp.concatenate(
        [c, c_ctx[None, :], jnp.zeros((MOD_ROWS - DEC_BATCH - 1, D), F32)], axis=0)
    mods = _ada_call(cond, w_ada, b_ada).reshape(DEPTH, MOD_ROWS, 1, 6 * D)
    final_g = final_norm_g.reshape(1, D)
    row_mod = lambda r: jnp.where(r < P_ROWS, CTX_MOD_ROW, r - P_ROWS)

    x = (x_prompt.reshape(P_ROWS, ROW, D), x_sample)
    ckv_list, kpe_list = [], []
    for l in range(DEPTH):
        xmid, h2, aff, ckv, kpe = _mixer_call(l, x, mods, wts, rope_tab, (cache_ckv, krope_t))
        ckv_list.append(ckv.reshape(BATCH, SEQ, KV_RANK))
        kpe_list.append(kpe.reshape(BATCH, SEQ, QK_ROPE))
        aff = aff.reshape(N_ROWS * N_EXP, ROW)
        slot = _route_call(aff)
        ye = _expert_call(l, slot, aff, _gather_call(slot, h2), w_gate, w_up, w_down)
        if l < DEPTH - 1:
            x = _combine_call(l, slot, ye, xmid, mods, None, 0, N_ROWS, row_mod)
        else:
            y_prompt = _combine_call(l, slot, ye, xmid, mods, final_g, 0, P_ROWS, lambda r: CTX_MOD_ROW)
            y_sample = _combine_call(l, slot, ye, xmid, mods, final_g, P_ROWS, DEC_BATCH, lambda r: r)
    return (y_prompt.reshape(BATCH, SEQ, D), y_sample,
            jnp.stack(ckv_list, axis=1), jnp.stack(kpe_list, axis=1))
```

```python
import jax
import jax.numpy as jnp
from jax import lax
from jax.experimental import pallas as pl
from jax.experimental.pallas import tpu as pltpu

F32 = jnp.float32
BF16 = jnp.bfloat16
I32 = jnp.int32

D = 1024
BATCH = 16
SEQ = 256
DEPTH = 4
DEC_BATCH = 8
DEC_SEQ = 1024
PAST_LEN = 512
GRID_W = 64
HEADS = 4
QK_NOPE = 128
QK_ROPE = 64
V_HEAD = 128
Q_RANK = 256
KV_RANK = 128
CONV_W = 256
CMLP_W = 256
CMLP_G = 4
CHUNK = 128
N_EXP = 16
EC_FACTOR = 2
FF = D // 2
ROPE_BASE = 10000.0
EPS = 1e-6
LOG2_E = 1.4426950408889634

LANES = 128
ROW = 1024
N_ROWS = (BATCH * SEQ + DEC_BATCH * DEC_SEQ) // ROW
P_ROWS = BATCH * SEQ // ROW
P_PER_ROW = ROW // SEQ
CAP = EC_FACTOR * ROW // N_EXP
CAP_P = EC_FACTOR * SEQ // N_EXP
CTX_MOD_ROW = DEC_BATCH
MOD_ROWS = 16
QK_PAD = QK_NOPE + LANES

C_Q, C_KV, C_KPE, C_CONV, C_CMLP, C_END = 0, 256, 384, 512, 1280, 1792
U_NOPE, U_PE, U_END = 0, HEADS * QK_NOPE, HEADS * (QK_NOPE + LANES)
TB = 256
assert 2 * QK_ROPE == LANES and TB == SEQ
TQ = 256
AT_BLOCKS = 1
VMEM_LIMIT = 56 * 1024 * 1024

NT = (((1,), (1,)), ((), ()))
TN = (((0,), (0,)), ((), ()))


def _rms(x, g):
    return x * lax.rsqrt(jnp.mean(x * x, axis=-1, keepdims=True) + EPS) * g


def _silu(x):
    return x / (1.0 + jnp.exp(-x))


def _dot(a, b):
    return jnp.dot(a, b, preferred_element_type=F32)


def _dot_nt(a, b):
    return lax.dot_general(a, b, NT, preferred_element_type=F32)


def _skewed(stages, items):
    n = len(stages)
    for t in range(len(items) + n - 1):
        for s in range(n - 1, -1, -1):
            if 0 <= t - s < len(items):
                stages[s](items[t - s])


def _ada_kernel(c_ref, w_ref, b_ref, o_ref):
    s = _silu(c_ref[...]).astype(BF16)
    o_ref[...] = _dot(s, w_ref[...].astype(BF16)) + b_ref[...]


def _ada_call(cond, w_ada, b_ada):
    nc, wc = 3, 2 * D
    return pl.pallas_call(
        _ada_kernel,
        out_shape=jax.ShapeDtypeStruct((DEPTH, MOD_ROWS, 6 * D), F32),
        grid=(DEPTH, nc),
        in_specs=[
            pl.BlockSpec((MOD_ROWS, D), lambda l, j: (0, 0)),
            pl.BlockSpec((None, D, wc), lambda l, j: (l, 0, j)),
            pl.BlockSpec((None, 1, wc), lambda l, j: (l, 0, j)),
        ],
        out_specs=pl.BlockSpec((None, MOD_ROWS, wc), lambda l, j: (l, 0, j)),
        compiler_params=pltpu.CompilerParams(
            dimension_semantics=("arbitrary", "arbitrary"), vmem_limit_bytes=VMEM_LIMIT),
        name="ada_mod",
    )(cond, w_ada, b_ada.reshape(DEPTH, 1, 6 * D))


def _make_mixer_kernel(split_x):
    def kern(l_ref, *refs):
        if split_x:
            _mixer_body(*refs)
        else:
            _mixer_body(refs[0], *refs)
    return kern


def _mixer_body(xp_ref, xs_ref, mod_ref, g1n_ref, g2n_ref, win_ref, qg_ref, kvg_ref, wuq_ref, wukv_ref,
                cw_ref, sw_ref, sb_ref, wout_ref, wr_ref, cs_ref, cckv_ref, ckr_ref,
                xmid_ref, h2_ref, aff_ref, ckv_out, kpe_out,
                qf_s, kf_s, v_s, mix_s, gb_s, z_s):
    qk_scale = float(QK_NOPE + QK_ROPE) ** -0.5 * LOG2_E
    nblk = ROW // TB

    def run(ctx):
        x_ref = xp_ref if ctx else xs_ref
        n_past = 0 if ctx else PAST_LEN
        seg = SEQ if ctx else ROW
        mod_row = CTX_MOD_ROW if ctx else pl.program_id(0) - P_ROWS
        mod = mod_ref[pl.ds(mod_row, 1), :]
        sh1, sc1, g1 = mod[:, 0:D], mod[:, D:2 * D], mod[:, 2 * D:3 * D]
        sh2, sc2, g2 = mod[:, 3 * D:4 * D], mod[:, 4 * D:5 * D], mod[:, 5 * D:6 * D]

        def store_kv(row0, rows, kv, kpb):
            for h in range(HEADS):
                c0 = h * (QK_NOPE + V_HEAD)
                kf_s[h, pl.ds(row0, rows), 0:QK_NOPE] = kv[:, c0:c0 + QK_NOPE].astype(BF16)
                kf_s[h, pl.ds(row0, rows), QK_NOPE:QK_PAD] = kpb
                v_s[h, pl.ds(row0, rows), :] = kv[:, c0 + QK_NOPE:c0 + QK_NOPE + V_HEAD].astype(BF16)

        if not ctx:
            kr_t = ckr_ref[...]
            store_kv(0, PAST_LEN, _dot(cckv_ref[...].astype(BF16), wukv_ref[...]),
                     jnp.concatenate([kr_t, jnp.zeros_like(kr_t)], axis=0).T.astype(BF16))

        grp_shift = (CMLP_W // CMLP_G).bit_length() - 1
        lane_grp = lax.shift_right_logical(lax.broadcasted_iota(I32, (CHUNK, CMLP_W), 1), grp_shift)

        def p1_norm(c):
            x = x_ref[pl.ds(c["r0"], TB), :]
            c["hb"] = (_rms(x, g1n_ref[...]) * (1.0 + sc1) + sh1).astype(BF16)

        def p1_proj(c):
            c["pa"] = _dot(c["hb"], win_ref[:, C_Q:C_CONV])
            c["cv"] = _dot(c["hb"], win_ref[:, C_CONV:C_CMLP])
            c["cm"] = _dot(c.pop("hb"), win_ref[:, C_CMLP:C_END])

        def p1_mid(c):
            r0, pa, cv, cm = c["r0"], c.pop("pa"), c.pop("cv"), c.pop("cm")
            kp = pa[:, C_KPE:C_CONV]
            c["cqn"] = _rms(pa[:, C_Q:C_KV], qg_ref[...]).astype(BF16)
            ckv_n = _rms(pa[:, C_KV:C_KPE], kvg_ref[...])
            if ctx:
                ckv_out[pl.ds(r0, TB), :] = ckv_n
                kpe_out[r0 // SEQ] = kp.T[0:QK_ROPE, :]
                kpe = kp
            else:
                y = kp * cs_ref[pl.ds(r0, TB), :]
                kpe = y + pltpu.roll(y, QK_ROPE, 1)
            c["ckv_b"], c["kpe_b"] = ckv_n.astype(BF16), kpe.astype(BF16)
            gb_s[pl.ds(r0, TB), :] = cv[:, 0:CONV_W]
            z_s[pl.ds(r0, TB), :] = cv[:, CONV_W:2 * CONV_W] * cv[:, 2 * CONV_W:3 * CONV_W]
            c["u"], c["vvb"] = cm[:, 0:CMLP_W], cm[:, CMLP_W:2 * CMLP_W].astype(BF16)

        def p1_up(c):
            c["qa"] = _dot(c.pop("cqn"), wuq_ref[...])
            c["kv"] = _dot(c.pop("ckv_b"), wukv_ref[...])
            vvb = c.pop("vvb")
            c["r"] = [_dot(sw_ref[...], vvb[k * CHUNK:(k + 1) * CHUNK, :]) for k in range(TB // CHUNK)]

        def p1_out(c):
            r0, qa, u = c["r0"], c.pop("qa"), c.pop("u")
            rope_lanes = lax.broadcasted_iota(I32, (TB, LANES), 1) < QK_ROPE
            for h in range(HEADS):
                qn = qa[:, U_NOPE + h * QK_NOPE:U_NOPE + (h + 1) * QK_NOPE]
                qp = qa[:, U_PE + h * LANES:U_PE + (h + 1) * LANES]
                if not ctx:
                    y = qp * cs_ref[pl.ds(r0, TB), :]
                    qp = y + pltpu.roll(y, QK_ROPE, 1)
                qp = jnp.where(rope_lanes, qp, 0.0)
                qf_s[h, pl.ds(r0, TB), 0:QK_NOPE] = (qn * qk_scale).astype(BF16)
                qf_s[h, pl.ds(r0, TB), QK_NOPE:QK_PAD] = (qp * qk_scale).astype(BF16)
            store_kv(n_past + r0, TB, c.pop("kv"), c.pop("kpe_b"))
            for k, r in enumerate(c.pop("r")):
                mixed = jnp.where(
                    lane_grp == 0, r[0:CHUNK],
                    jnp.where(lane_grp == 1, r[CHUNK:2 * CHUNK],
                              jnp.where(lane_grp == 2, r[2 * CHUNK:3 * CHUNK], r[3 * CHUNK:4 * CHUNK])))
                out = u[k * CHUNK:(k + 1) * CHUNK, :] * (mixed + sb_ref[...])
                mix_s[pl.ds(r0 + k * CHUNK, CHUNK), HEADS * V_HEAD + CONV_W:D] = out.astype(BF16)

        _skewed([p1_norm, p1_proj, p1_mid, p1_up, p1_out], [dict(r0=i * TB) for i in range(nblk)])

        z = z_s[...]
        pos = lax.broadcasted_iota(I32, (ROW, CONV_W), 0) & (seg - 1)
        zm = jnp.where(pos == 0, 0.0, pltpu.roll(z, 1, 0))
        zp = jnp.where(pos == seg - 1, 0.0, pltpu.roll(z, ROW - 1, 0))
        cw = cw_ref[...]
        conv = zm * cw[0:1, :] + z * cw[1:2, :] + zp * cw[2:3, :]
        mix_s[:, HEADS * V_HEAD:HEADS * V_HEAD + CONV_W] = (gb_s[...] * conv).astype(BF16)

        def at_qk(c):
            keys = kf_s[c["h"], pl.ds(c["k0"], seg + n_past), :]
            c["s"] = _dot_nt(qf_s[c["h"], pl.ds(c["r0"], TQ), :], keys)

        def at_max(c):
            c["m"] = jnp.max(c["s"], axis=-1, keepdims=True)

        def at_exp(c):
            e = jnp.exp2(c.pop("s") - c.pop("m"))
            c["l"] = jnp.sum(e, axis=-1, keepdims=True)
            c["p"] = e.astype(BF16)

        def at_pv(c):
            c["o"] = _dot(c.pop("p"), v_s[c["h"], pl.ds(c["k0"], seg + n_past), :])

        def at_out(c):
            o = c.pop("o") / c.pop("l")
            mix_s[pl.ds(c["r0"], TQ), c["h"] * V_HEAD:(c["h"] + 1) * V_HEAD] = o.astype(BF16)

        at_stages = [at_qk, lambda c: (at_max(c), at_exp(c)), lambda c: (at_pv(c), at_out(c))]
        if ctx:
            _skewed(at_stages, [dict(r0=b * TQ, k0=b * TQ, h=h) for b in range(ROW // TQ) for h in range(HEADS)])
        else:
            def at_block(b, carry):
                r0 = pl.multiple_of(b * (AT_BLOCKS * TQ), AT_BLOCKS * TQ)
                _skewed(at_stages, [dict(r0=r0 + k * TQ, k0=0, h=h) for k in range(AT_BLOCKS) for h in range(HEADS)])
                return carry

            lax.fori_loop(0, ROW // (AT_BLOCKS * TQ), at_block, 0)

        wr = wr_ref[...]
        wr_hi = wr.astype(BF16)
        wr_hl = jnp.concatenate([wr_hi, (wr - wr_hi.astype(F32)).astype(BF16)], axis=0)

        def p2_proj(c):
            c["mo"] = _dot(mix_s[pl.ds(c["r0"], TB), :], wout_ref[...])

        def p2_norm(c):
            r0 = c["r0"]
            xm = x_ref[pl.ds(r0, TB), :] + g1 * c.pop("mo")
            xmid_ref[pl.ds(r0, TB), :] = xm
            h2 = _rms(xm, g2n_ref[...]) * (1.0 + sc2) + sh2
            h2_hi = h2.astype(BF16)
            h2_ref[pl.ds(r0, TB), :] = h2_hi
            c["hi"], c["lo"] = h2_hi, (h2 - h2_hi.astype(F32)).astype(BF16)

        def p2_route(c):
            la = _dot_nt(wr_hl, c["hi"])
            c["lg"] = la[0:N_EXP] + la[N_EXP:2 * N_EXP] + _dot_nt(wr_hi, c.pop("lo"))
            c.pop("hi")

        def p2_aff(c):
            lg = c.pop("lg")
            e = jnp.exp(lg - jnp.max(lg, axis=0, keepdims=True))
            aff_ref[:, pl.ds(c["r0"], TB)] = e / jnp.sum(e, axis=0, keepdims=True)

        _skewed([p2_proj, p2_norm, p2_route, p2_aff], [dict(r0=i * TB) for i in range(nblk)])

    is_ctx = pl.program_id(0) < P_ROWS

    @pl.when(is_ctx)
    def _():
        run(True)

    @pl.when(jnp.logical_not(is_ctx))
    def _():
        run(False)


def _layer_spec(shape):
    nd = len(shape)
    return pl.BlockSpec((None,) + shape, lambda r, l: (l[0],) + (0,) * nd, pipeline_mode=pl.Buffered(1))


def _layer_arg(l):
    return jnp.full((1,), l, I32)


def _mixer_call(l, x, mods, wts, rope_tab, caches):
    lk = PAST_LEN + ROW
    lat = lambda r: jnp.maximum(r - P_ROWS, 0)
    ctx = lambda r: jnp.minimum(r, P_ROWS - 1)
    split_x = isinstance(x, tuple)
    if split_x:
        x_specs = [pl.BlockSpec((None, ROW, D), lambda r, l: (ctx(r), 0, 0)),
                   pl.BlockSpec((None, ROW, D), lambda r, l: (lat(r), 0, 0))]
    else:
        x_specs, x = [pl.BlockSpec((None, ROW, D), lambda r, l: (r, 0, 0))], (x,)
    in_specs = x_specs + [
        _layer_spec((MOD_ROWS, 6 * D)),
        _layer_spec((1, D)), _layer_spec((1, D)),
        _layer_spec((D, C_END)),
        _layer_spec((1, Q_RANK)), _layer_spec((1, KV_RANK)),
        _layer_spec((Q_RANK, U_END)),
        _layer_spec((KV_RANK, HEADS * (QK_NOPE + V_HEAD))),
        _layer_spec((3, CONV_W)),
        _layer_spec((CMLP_G * CHUNK, CHUNK)),
        _layer_spec((CHUNK, CMLP_W)),
        _layer_spec((D, D)),
        _layer_spec((N_EXP, D)),
        pl.BlockSpec((ROW, LANES), lambda r, l: (0, 0), pipeline_mode=pl.Buffered(1)),
        pl.BlockSpec((None, None, PAST_LEN, KV_RANK), lambda r, l: (lat(r), l[0], 0, 0)),
        pl.BlockSpec((None, None, QK_ROPE, PAST_LEN), lambda r, l: (lat(r), l[0], 0, 0)),
    ]
    args = list(x) + [mods, wts["norm1_g"], wts["norm2_g"], wts["w_in"], wts["q_norm_g"], wts["kv_norm_g"],
            wts["w_uq"], wts["w_ukv"], wts["conv_w"], wts["spatial_w"], wts["spatial_b"],
            wts["w_out"], wts["w_router_t"], rope_tab, caches[0], caches[1]]
    out_shape = [jax.ShapeDtypeStruct((N_ROWS, ROW, D), F32),
                 jax.ShapeDtypeStruct((N_ROWS, ROW, D), BF16),
                 jax.ShapeDtypeStruct((N_ROWS, N_EXP, ROW), F32),
                 jax.ShapeDtypeStruct((P_ROWS, ROW, KV_RANK), F32),
                 jax.ShapeDtypeStruct((BATCH, QK_ROPE, SEQ), F32)]
    out_specs = [pl.BlockSpec((None, ROW, D), lambda r, l: (r, 0, 0)),
                 pl.BlockSpec((None, ROW, D), lambda r, l: (r, 0, 0)),
                 pl.BlockSpec((None, N_EXP, ROW), lambda r, l: (r, 0, 0)),
                 pl.BlockSpec((None, ROW, KV_RANK), lambda r, l: (ctx(r), 0, 0)),
                 pl.BlockSpec((P_PER_ROW, QK_ROPE, SEQ), lambda r, l: (ctx(r), 0, 0))]
    return pl.pallas_call(
        _make_mixer_kernel(split_x),
        out_shape=out_shape,
        grid_spec=pltpu.PrefetchScalarGridSpec(
            num_scalar_prefetch=1,
            grid=(N_ROWS,),
            in_specs=in_specs,
            out_specs=out_specs,
            scratch_shapes=[
                pltpu.VMEM((HEADS, ROW, QK_PAD), BF16),
                pltpu.VMEM((HEADS, lk, QK_PAD), BF16),
                pltpu.VMEM((HEADS, lk, V_HEAD), BF16),
                pltpu.VMEM((ROW, D), BF16),
                pltpu.VMEM((ROW, CONV_W), F32),
                pltpu.VMEM((ROW, CONV_W), F32),
            ]),
        compiler_params=pltpu.CompilerParams(
            dimension_semantics=("arbitrary",), vmem_limit_bytes=VMEM_LIMIT),
        name="mixer",
    )(_layer_arg(l), *args)


def _cap_thresholds(affs, caps):
    def bit_step(i, ts):
        bit = jnp.left_shift(jnp.int32(1), 30 - i)
        out = []
        for aff, cap, t in zip(affs, caps, ts):
            cand = t | bit
            cnt = jnp.sum(jnp.where(aff >= pltpu.bitcast(cand, F32), 1.0, 0.0), axis=1, keepdims=True)
            out.append(jnp.where(cnt >= float(cap), cand, t))
        return tuple(out)

    ts = lax.fori_loop(0, 31, bit_step, tuple(jnp.zeros((a.shape[0], 1), I32) for a in affs))
    return [pltpu.bitcast(t, F32) for t in ts]


def _select_slots(aff, thr, cap, tri):
    capf = float(cap)
    gt = aff > thr
    eq = aff == thr
    n_gt = jnp.sum(jnp.where(gt, 1.0, 0.0), axis=1, keepdims=True)
    eq_rank = _dot(jnp.where(eq, 1.0, 0.0).astype(BF16), tri)
    sel = jnp.where(gt, 1.0, jnp.where(eq, jnp.where(eq_rank < capf - n_gt, 1.0, 0.0), 0.0))
    pos = _dot(sel.astype(BF16), tri)
    return jnp.where(sel > 0.5, pos.astype(I32), -1)


def _route_kernel(aff_ref, slot_ref, tri_s):
    r = lax.broadcasted_iota(I32, (ROW, ROW), 0)
    c = lax.broadcasted_iota(I32, (ROW, ROW), 1)
    tri_s[...] = jnp.where(r < c, 1.0, 0.0).astype(BF16)
    np_ = P_ROWS * N_EXP
    affs = [aff_ref[0:np_, s * SEQ:(s + 1) * SEQ] for s in range(P_PER_ROW)] + [aff_ref[np_:, :]]
    caps = [CAP_P] * P_PER_ROW + [CAP]
    thrs = _cap_thresholds(affs, caps)
    for s in range(P_PER_ROW):
        sl = _select_slots(affs[s], thrs[s], CAP_P, tri_s[0:SEQ, 0:SEQ])
        slot_ref[0:np_, s * SEQ:(s + 1) * SEQ] = jnp.where(sl >= 0, sl + s * CAP_P, -1)
    slot_ref[np_:, :] = _select_slots(affs[-1], thrs[-1], CAP, tri_s[...])


def _route_call(aff):
    rows = N_ROWS * N_EXP
    return pl.pallas_call(
        _route_kernel,
        out_shape=jax.ShapeDtypeStruct((rows, ROW), I32),
        scratch_shapes=[pltpu.VMEM((ROW, ROW), BF16)],
        compiler_params=pltpu.CompilerParams(vmem_limit_bytes=VMEM_LIMIT),
        name="route",
    )(aff)


GATHER_NC = 256


def _gather_kernel(slot_ref, h2_ref, xs_ref, hot_s):
    @pl.when(pl.program_id(0) >= P_ROWS)
    def _():
        j = lax.broadcasted_iota(I32, (CAP, ROW), 0)
        for e in range(N_EXP):
            hot_s[e * CAP:(e + 1) * CAP, :] = jnp.where(slot_ref[e:e + 1, :] == j, 1.0, 0.0).astype(BF16)
        for c in range(D // GATHER_NC):
            xs = _dot(hot_s[...], h2_ref[:, c * GATHER_NC:(c + 1) * GATHER_NC]).astype(BF16)
            for e in range(N_EXP):
                xs_ref[e, :, c * GATHER_NC:(c + 1) * GATHER_NC] = xs[e * CAP:(e + 1) * CAP, :]

    @pl.when(pl.program_id(0) < P_ROWS)
    def _():
        j = lax.broadcasted_iota(I32, (CAP_P, SEQ), 0)
        for s in range(P_PER_ROW):
            toks = slice(s * SEQ, (s + 1) * SEQ)
            for e in range(N_EXP):
                hot_s[e * CAP_P:(e + 1) * CAP_P, 0:SEQ] = jnp.where(
                    slot_ref[e:e + 1, toks] == j + s * CAP_P, 1.0, 0.0).astype(BF16)
            xs = _dot(hot_s[0:N_EXP * CAP_P, 0:SEQ], h2_ref[toks, :]).astype(BF16)
            for e in range(N_EXP):
                xs_ref[e, s * CAP_P:(s + 1) * CAP_P, :] = xs[e * CAP_P:(e + 1) * CAP_P, :]


def _gather_call(slot, h2):
    return pl.pallas_call(
        _gather_kernel,
        out_shape=jax.ShapeDtypeStruct((N_EXP, N_ROWS, CAP, D), BF16),
        grid=(N_ROWS,),
        in_specs=[
            pl.BlockSpec((N_EXP, ROW), lambda r: (r, 0)),
            pl.BlockSpec((None, ROW, D), lambda r: (r, 0, 0)),
        ],
        out_specs=pl.BlockSpec((N_EXP, None, CAP, D), lambda r: (0, r, 0, 0)),
        scratch_shapes=[pltpu.VMEM((N_EXP * CAP, ROW), BF16)],
        compiler_params=pltpu.CompilerParams(
            dimension_semantics=("arbitrary",), vmem_limit_bytes=VMEM_LIMIT),
        name="gather",
    )(slot, h2)


EXP_ROWS = 4


def _expert_kernel(l_ref, slot_ref, aff_ref, xs_ref, wg_ref, wu_ref, wd_ref, ye_ref, wg_s, wu_s, wd_s):
    e = pl.program_id(0)
    wg_s[...] = wg_ref[...].astype(BF16)
    wu_s[...] = wu_ref[...].astype(BF16)
    wd_s[...] = wd_ref[...].astype(BF16)
    j = lax.broadcasted_iota(I32, (CAP, ROW), 0)

    def gates(c):
        c["gate"] = []
        for k in range(EXP_ROWS):
            row = (c["r0"] + k) * N_EXP + e
            hit = slot_ref[pl.ds(row, 1), :] == j
            c["gate"].append(jnp.sum(jnp.where(hit, aff_ref[pl.ds(row, 1), :], 0.0), axis=1, keepdims=True))

    def up(c):
        xs = xs_ref[c["r0"]:c["r0"] + EXP_ROWS].reshape(EXP_ROWS * CAP, D)
        c["a"], c["u"] = _dot(xs, wg_s[...]), _dot(xs, wu_s[...])

    def act(c):
        c["hid"] = (_silu(c.pop("a")) * c.pop("u")).astype(BF16)

    def down(c):
        c["ye"] = _dot(c.pop("hid"), wd_s[...])

    def out(c):
        ye, gate = c.pop("ye"), c.pop("gate")
        for k in range(EXP_ROWS):
            ye_ref[c["r0"] + k] = (ye[k * CAP:(k + 1) * CAP, :] * gate[k]).astype(BF16)

    _skewed([gates, up, act, down, out], [dict(r0=g * EXP_ROWS) for g in range(N_ROWS // EXP_ROWS)])


def _expert_call(l, slot, aff, xs, w_gate, w_up, w_down):
    rows = N_ROWS * N_EXP
    return pl.pallas_call(
        _expert_kernel,
        out_shape=jax.ShapeDtypeStruct((N_EXP, N_ROWS, CAP, D), BF16),
        grid_spec=pltpu.PrefetchScalarGridSpec(
            num_scalar_prefetch=1,
            grid=(N_EXP,),
            in_specs=[
                pl.BlockSpec((rows, ROW), lambda e, l: (0, 0), pipeline_mode=pl.Buffered(1)),
                pl.BlockSpec((rows, ROW), lambda e, l: (0, 0), pipeline_mode=pl.Buffered(1)),
                pl.BlockSpec((None, N_ROWS, CAP, D), lambda e, l: (e, 0, 0, 0)),
                pl.BlockSpec((None, None, D, FF), lambda e, l: (l[0], e, 0, 0)),
                pl.BlockSpec((None, None, D, FF), lambda e, l: (l[0], e, 0, 0)),
                pl.BlockSpec((None, None, FF, D), lambda e, l: (l[0], e, 0, 0)),
            ],
            out_specs=pl.BlockSpec((None, N_ROWS, CAP, D), lambda e, l: (e, 0, 0, 0)),
            scratch_shapes=[pltpu.VMEM((D, FF), BF16), pltpu.VMEM((D, FF), BF16), pltpu.VMEM((FF, D), BF16)]),
        compiler_params=pltpu.CompilerParams(
            dimension_semantics=("arbitrary",), vmem_limit_bytes=VMEM_LIMIT),
        name="experts",
    )(_layer_arg(l), slot, aff, xs, w_gate, w_up, w_down)


COMB_TC = 256


def _make_combine_kernel(final, first_row):
    def kern(*refs):
        if final:
            _, slot_ref, ye_ref, xmid_ref, g2_ref, fg_ref, o_ref = refs
        else:
            _, slot_ref, ye_ref, xmid_ref, g2_ref, o_ref = refs
        def run(ctx):
            mod_row = CTX_MOD_ROW if ctx else first_row + pl.program_id(0) - P_ROWS
            g2 = g2_ref[pl.ds(mod_row, 1), :]
            nslot = CAP_P if ctx else CAP
            j = lax.broadcasted_iota(I32, (nslot, COMB_TC), 0)
            if not ctx:
                ye = ye_ref[...].reshape(N_EXP * CAP, D)
            for t in range(ROW // COMB_TC):
                cols = slice(t * COMB_TC, (t + 1) * COMB_TC)
                s0 = t * CAP_P if ctx else 0
                onehot = jnp.concatenate(
                    [jnp.where(slot_ref[e:e + 1, cols] == j + s0, 1.0, 0.0).astype(BF16) for e in range(N_EXP)],
                    axis=0)
                if ctx:
                    ye = jnp.concatenate([ye_ref[e, s0:s0 + CAP_P, :] for e in range(N_EXP)], axis=0)
                moe = lax.dot_general(onehot, ye, TN, preferred_element_type=F32)
                xn = xmid_ref[cols, :] + g2 * moe
                if final:
                    xn = _rms(xn, fg_ref[...])
                o_ref[cols, :] = xn

        is_ctx = first_row + pl.program_id(0) < P_ROWS

        @pl.when(is_ctx)
        def _():
            run(True)

        @pl.when(jnp.logical_not(is_ctx))
        def _():
            run(False)

    return kern


def _combine_call(l, slot, ye, xmid, mods, final_g, row0, n_rows):
    final = final_g is not None
    in_specs = [
        pl.BlockSpec((N_EXP, ROW), lambda r, l: (row0 + r, 0)),
        pl.BlockSpec((N_EXP, None, CAP, D), lambda r, l: (0, row0 + r, 0, 0)),
        pl.BlockSpec((None, ROW, D), lambda r, l: (row0 + r, 0, 0)),
        pl.BlockSpec((None, MOD_ROWS, D), lambda r, l: (l[0], 0, 5), pipeline_mode=pl.Buffered(1)),
    ]
    args = [slot, ye, xmid, mods]
    if final:
        in_specs.append(pl.BlockSpec((1, D), lambda r, l: (0, 0)))
        args.append(final_g)
    return pl.pallas_call(
        _make_combine_kernel(final, row0),
        out_shape=jax.ShapeDtypeStruct((n_rows, ROW, D), F32),
        grid_spec=pltpu.PrefetchScalarGridSpec(
            num_scalar_prefetch=1,
            grid=(n_rows,),
            in_specs=in_specs,
            out_specs=pl.BlockSpec((None, ROW, D), lambda r, l: (r, 0, 0))),
        compiler_params=pltpu.CompilerParams(
            dimension_semantics=("arbitrary",), vmem_limit_bytes=VMEM_LIMIT),
        name="combine",
    )(_layer_arg(l), *args)


def _rope_rot(w, axis):
    q = QK_ROPE // 4
    part = lambda a, b: lax.slice_in_dim(w, a * q, b * q, axis=axis)
    return jnp.concatenate([-part(1, 2), part(0, 1), -part(3, 4), part(2, 3)], axis=axis)


def _repack_kernel(win_ref, wuq_ref, wout_ref, win_o, wuq_o, wout_o):
    o_conv = Q_RANK + KV_RANK + QK_ROPE
    wt = win_ref[...]
    win_o[...] = jnp.concatenate(
        [wt[:o_conv], _rope_rot(wt[o_conv - QK_ROPE:o_conv], 0), wt[o_conv:]], axis=0).T.astype(BF16)
    u = wuq_ref[...]
    hd = QK_NOPE + QK_ROPE
    pe = [u[:, h * hd + QK_NOPE:(h + 1) * hd] for h in range(HEADS)]
    wuq_o[...] = jnp.concatenate(
        [u[:, h * hd:h * hd + QK_NOPE] for h in range(HEADS)]
        + [piece for h in range(HEADS) for piece in (pe[h], _rope_rot(pe[h], 1))], axis=1).astype(BF16)
    wout_o[...] = wout_ref[...].astype(BF16)


def _repack_call(w_in, w_uq, w_out):
    in_w, uq_w = w_in.shape[-1], w_uq.shape[-1]
    w_in = jnp.swapaxes(w_in, 1, 2)
    return pl.pallas_call(
        _repack_kernel,
        out_shape=[jax.ShapeDtypeStruct((DEPTH, D, C_END), BF16),
                   jax.ShapeDtypeStruct((DEPTH, Q_RANK, U_END), BF16),
                   jax.ShapeDtypeStruct((DEPTH, D, D), BF16)],
        grid=(DEPTH,),
        in_specs=[pl.BlockSpec((None, in_w, D), lambda l: (l, 0, 0)),
                  pl.BlockSpec((None, Q_RANK, uq_w), lambda l: (l, 0, 0)),
                  pl.BlockSpec((None, D, D), lambda l: (l, 0, 0))],
        out_specs=[pl.BlockSpec((None, D, C_END), lambda l: (l, 0, 0)),
                   pl.BlockSpec((None, Q_RANK, U_END), lambda l: (l, 0, 0)),
                   pl.BlockSpec((None, D, D), lambda l: (l, 0, 0))],
        compiler_params=pltpu.CompilerParams(
            dimension_semantics=("arbitrary",), vmem_limit_bytes=VMEM_LIMIT),
        name="repack",
    )(w_in, w_uq, w_out)


def _prep_weights(norm1_g, norm2_g, w_in, q_norm_g, kv_norm_g, w_uq, w_ukv, conv_w, spatial_w,
                  spatial_b, w_out, w_router):
    w_in_p, w_uq_p, w_out_p = _repack_call(w_in, w_uq, w_out)
    sb = jnp.repeat(jnp.swapaxes(spatial_b, 1, 2), CMLP_W // CMLP_G, axis=-1)
    return dict(
        norm1_g=norm1_g.reshape(DEPTH, 1, D), norm2_g=norm2_g.reshape(DEPTH, 1, D),
        w_in=w_in_p, q_norm_g=q_norm_g.reshape(DEPTH, 1, Q_RANK),
        kv_norm_g=kv_norm_g.reshape(DEPTH, 1, KV_RANK), w_uq=w_uq_p, w_ukv=w_ukv.astype(BF16),
        conv_w=conv_w, spatial_w=spatial_w.reshape(DEPTH, CMLP_G * CHUNK, CHUNK).astype(BF16),
        spatial_b=sb, w_out=w_out_p, w_router_t=jnp.swapaxes(w_router, 1, 2))


def _rope_table(n):
    rows = n // GRID_W
    row = jnp.repeat(jnp.arange(rows), GRID_W).astype(F32)
    col = jnp.tile(jnp.arange(GRID_W), rows).astype(F32)
    n_freq = QK_ROPE // 4
    inv = ROPE_BASE ** (-jnp.arange(n_freq, dtype=F32) / n_freq)
    ang_r, ang_c = row[:, None] * inv, col[:, None] * inv
    ang = jnp.concatenate([ang_r, ang_r, ang_c, ang_c], axis=-1)
    return jnp.concatenate([jnp.cos(ang), jnp.sin(ang)], axis=-1)


def kernel(x_prompt, x_sample, cache_ckv, cache_krope, c, c_ctx, w_ada, b_ada, norm1_g, norm2_g, w_in,
           q_norm_g, kv_norm_g, w_uq, w_ukv, conv_w, spatial_w, spatial_b, w_out, w_router, w_gate,
           w_up, w_down, final_norm_g):
    wts = _prep_weights(norm1_g, norm2_g, w_in, q_norm_g, kv_norm_g, w_uq, w_ukv, conv_w, spatial_w,
                        spatial_b, w_out, w_router)
    rope_tab = _rope_table(DEC_SEQ)
    krope_t = jnp.swapaxes(cache_krope, 2, 3)
    cond = jnp.concatenate(
        [c, c_ctx[None, :], jnp.zeros((MOD_ROWS - DEC_BATCH - 1, D), F32)], axis=0)
    mods = _ada_call(cond, w_ada, b_ada)
    final_g = final_norm_g.reshape(1, D)

    x = (x_prompt.reshape(P_ROWS, ROW, D), x_sample)
    ckv_list, kpe_list = [], []
    for l in range(DEPTH):
        xmid, h2, aff, ckv, kpe = _mixer_call(l, x, mods, wts, rope_tab, (cache_ckv, krope_t))
        ckv_list.append(ckv.reshape(BATCH, SEQ, KV_RANK))
        kpe_list.append(kpe)
        aff = aff.reshape(N_ROWS * N_EXP, ROW)
        slot = _route_call(aff)
        ye = _expert_call(l, slot, aff, _gather_call(slot, h2), w_gate, w_up, w_down)
        if l < DEPTH - 1:
            x = _combine_call(l, slot, ye, xmid, mods, None, 0, N_ROWS)
        else:
            y_prompt = _combine_call(l, slot, ye, xmid, mods, final_g, 0, P_ROWS)
            y_sample = _combine_call(l, slot, ye, xmid, mods, final_g, P_ROWS, DEC_BATCH)
    return (y_prompt.reshape(BATCH, SEQ, D), y_sample,
            jnp.stack(ckv_list, axis=1), jnp.swapaxes(jnp.stack(kpe_list, axis=1), 2, 3))
```

```python
import jax
import jax.numpy as jnp
from jax import lax
from jax.experimental import pallas as pl
from jax.experimental.pallas import tpu as pltpu

F32 = jnp.float32
BF16 = jnp.bfloat16
I32 = jnp.int32

D = 1024
BATCH = 16
SEQ = 256
DEPTH = 4
DEC_BATCH = 8
DEC_SEQ = 1024
PAST_LEN = 512
GRID_W = 64
HEADS = 4
QK_NOPE = 128
QK_ROPE = 64
V_HEAD = 128
Q_RANK = 256
KV_RANK = 128
CONV_W = 256
CMLP_W = 256
CMLP_G = 4
CHUNK = 128
N_EXP = 16
EC_FACTOR = 2
FF = D // 2
ROPE_BASE = 10000.0
EPS = 1e-6
LOG2_E = 1.4426950408889634

LANES = 128
ROW = 1024
N_ROWS = (BATCH * SEQ + DEC_BATCH * DEC_SEQ) // ROW
P_ROWS = BATCH * SEQ // ROW
P_PER_ROW = ROW // SEQ
CAP = EC_FACTOR * ROW // N_EXP
CAP_P = EC_FACTOR * SEQ // N_EXP
CTX_MOD_ROW = DEC_BATCH
MOD_ROWS = 16
QK_PAD = QK_NOPE + LANES

C_Q, C_KV, C_KPE, C_CONV, C_CMLP, C_END = 0, 256, 384, 512, 1280, 1792
U_NOPE, U_PE, U_END = 0, HEADS * QK_NOPE, HEADS * (QK_NOPE + LANES)
TB = 256
assert 2 * QK_ROPE == LANES and TB == SEQ
TQ = 256
VMEM_LIMIT = 56 * 1024 * 1024

NT = (((1,), (1,)), ((), ()))
TN = (((0,), (0,)), ((), ()))


def _rms(x, g):
    return x * lax.rsqrt(jnp.mean(x * x, axis=-1, keepdims=True) + EPS) * g


def _silu(x):
    return x / (1.0 + jnp.exp(-x))


def _dot(a, b):
    return jnp.dot(a, b, preferred_element_type=F32)


def _dot_nt(a, b):
    return lax.dot_general(a, b, NT, preferred_element_type=F32)


def _skewed(stages, items):
    n = len(stages)
    for t in range(len(items) + n - 1):
        for s in range(n - 1, -1, -1):
            if 0 <= t - s < len(items):
                stages[s](items[t - s])


def _ada_kernel(c_ref, w_ref, b_ref, o_ref):
    s = _silu(c_ref[...]).astype(BF16)
    o_ref[...] = _dot(s, w_ref[...].astype(BF16)) + b_ref[...]


def _ada_call(cond, w_ada, b_ada):
    nc, wc = 2, 3 * D
    return pl.pallas_call(
        _ada_kernel,
        out_shape=jax.ShapeDtypeStruct((DEPTH, MOD_ROWS, 6 * D), F32),
        grid=(DEPTH, nc),
        in_specs=[
            pl.BlockSpec((MOD_ROWS, D), lambda l, j: (0, 0)),
            pl.BlockSpec((None, D, wc), lambda l, j: (l, 0, j)),
            pl.BlockSpec((None, 1, wc), lambda l, j: (l, 0, j)),
        ],
        out_specs=pl.BlockSpec((None, MOD_ROWS, wc), lambda l, j: (l, 0, j)),
        compiler_params=pltpu.CompilerParams(
            dimension_semantics=("arbitrary", "arbitrary"), vmem_limit_bytes=VMEM_LIMIT),
        name="ada_mod",
    )(cond, w_ada, b_ada.reshape(DEPTH, 1, 6 * D))


def _make_mixer_kernel(split_x):
    def kern(l_ref, *refs):
        if split_x:
            _mixer_body(*refs)
        else:
            _mixer_body(refs[0], *refs)
    return kern


def _mixer_body(xp_ref, xs_ref, mod_ref, g1n_ref, g2n_ref, win_ref, qg_ref, kvg_ref, wuq_ref, wukv_ref,
                cw_ref, sw_ref, sb_ref, wout_ref, wr_ref, cs_ref, cckv_ref, ckr_ref,
                xmid_ref, h2_ref, aff_ref, ckv_out, kpe_out,
                qf_s, kf_s, v_s, mix_s, gb_s, z_s):
    qk_scale = float(QK_NOPE + QK_ROPE) ** -0.5 * LOG2_E
    nblk = ROW // TB

    def run(ctx):
        x_ref = xp_ref if ctx else xs_ref
        n_past = 0 if ctx else PAST_LEN
        seg = SEQ if ctx else ROW
        mod_row = CTX_MOD_ROW if ctx else pl.program_id(0) - P_ROWS
        mod = mod_ref[pl.ds(mod_row, 1), :]
        sh1, sc1, g1 = mod[:, 0:D], mod[:, D:2 * D], mod[:, 2 * D:3 * D]
        sh2, sc2, g2 = mod[:, 3 * D:4 * D], mod[:, 4 * D:5 * D], mod[:, 5 * D:6 * D]

        def store_kv(row0, rows, kv, kpb):
            for h in range(HEADS):
                c0 = h * (QK_NOPE + V_HEAD)
                kf_s[h, pl.ds(row0, rows), 0:QK_NOPE] = kv[:, c0:c0 + QK_NOPE].astype(BF16)
                kf_s[h, pl.ds(row0, rows), QK_NOPE:QK_PAD] = kpb
                v_s[h, pl.ds(row0, rows), :] = kv[:, c0 + QK_NOPE:c0 + QK_NOPE + V_HEAD].astype(BF16)

        if not ctx:
            kr_t = ckr_ref[...]
            store_kv(0, PAST_LEN, _dot(cckv_ref[...].astype(BF16), wukv_ref[...]),
                     jnp.concatenate([kr_t, jnp.zeros_like(kr_t)], axis=0).T.astype(BF16))

        grp_shift = (CMLP_W // CMLP_G).bit_length() - 1
        lane_grp = lax.shift_right_logical(lax.broadcasted_iota(I32, (CHUNK, CMLP_W), 1), grp_shift)

        def p1_norm(c):
            x = x_ref[pl.ds(c["r0"], TB), :]
            c["hb"] = (_rms(x, g1n_ref[...]) * (1.0 + sc1) + sh1).astype(BF16)

        def p1_proj(c):
            c["pa"] = _dot(c["hb"], win_ref[:, C_Q:C_CONV])
            c["cv"] = _dot(c["hb"], win_ref[:, C_CONV:C_CMLP])
            c["cm"] = _dot(c.pop("hb"), win_ref[:, C_CMLP:C_END])

        def p1_mid(c):
            r0, pa, cv, cm = c["r0"], c.pop("pa"), c.pop("cv"), c.pop("cm")
            kp = pa[:, C_KPE:C_CONV]
            c["cqn"] = _rms(pa[:, C_Q:C_KV], qg_ref[...]).astype(BF16)
            ckv_n = _rms(pa[:, C_KV:C_KPE], kvg_ref[...])
            if ctx:
                ckv_out[pl.ds(r0, TB), :] = ckv_n
                kpe_out[r0 // SEQ] = kp.T[0:QK_ROPE, :]
                kpe = kp
            else:
                y = kp * cs_ref[pl.ds(r0, TB), :]
                kpe = y + pltpu.roll(y, QK_ROPE, 1)
            c["ckv_b"], c["kpe_b"] = ckv_n.astype(BF16), kpe.astype(BF16)
            gb_s[pl.ds(r0, TB), :] = cv[:, 0:CONV_W]
            z_s[pl.ds(r0, TB), :] = cv[:, CONV_W:2 * CONV_W] * cv[:, 2 * CONV_W:3 * CONV_W]
            c["u"], c["vvb"] = cm[:, 0:CMLP_W], cm[:, CMLP_W:2 * CMLP_W].astype(BF16)

        def p1_up(c):
            c["qa"] = _dot(c.pop("cqn"), wuq_ref[...])
            c["kv"] = _dot(c.pop("ckv_b"), wukv_ref[...])
            vvb = c.pop("vvb")
            c["r"] = [_dot(sw_ref[...], vvb[k * CHUNK:(k + 1) * CHUNK, :]) for k in range(TB // CHUNK)]

        def p1_out(c):
            r0, qa, u = c["r0"], c.pop("qa"), c.pop("u")
            rope_lanes = lax.broadcasted_iota(I32, (TB, LANES), 1) < QK_ROPE
            for h in range(HEADS):
                qn = qa[:, U_NOPE + h * QK_NOPE:U_NOPE + (h + 1) * QK_NOPE]
                qp = qa[:, U_PE + h * LANES:U_PE + (h + 1) * LANES]
                if not ctx:
                    y = qp * cs_ref[pl.ds(r0, TB), :]
                    qp = y + pltpu.roll(y, QK_ROPE, 1)
                qp = jnp.where(rope_lanes, qp, 0.0)
                qf_s[h, pl.ds(r0, TB), 0:QK_NOPE] = (qn * qk_scale).astype(BF16)
                qf_s[h, pl.ds(r0, TB), QK_NOPE:QK_PAD] = (qp * qk_scale).astype(BF16)
            store_kv(n_past + r0, TB, c.pop("kv"), c.pop("kpe_b"))
            for k, r in enumerate(c.pop("r")):
                mixed = jnp.where(
                    lane_grp == 0, r[0:CHUNK],
                    jnp.where(lane_grp == 1, r[CHUNK:2 * CHUNK],
                              jnp.where(lane_grp == 2, r[2 * CHUNK:3 * CHUNK], r[3 * CHUNK:4 * CHUNK])))
                out = u[k * CHUNK:(k + 1) * CHUNK, :] * (mixed + sb_ref[...])
                mix_s[pl.ds(r0 + k * CHUNK, CHUNK), HEADS * V_HEAD + CONV_W:D] = out.astype(BF16)

        _skewed([p1_norm, p1_proj, p1_mid, p1_up, p1_out], [dict(r0=i * TB) for i in range(nblk)])

        z = z_s[...]
        pos = lax.broadcasted_iota(I32, (ROW, CONV_W), 0) & (seg - 1)
        zm = jnp.where(pos == 0, 0.0, pltpu.roll(z, 1, 0))
        zp = jnp.where(pos == seg - 1, 0.0, pltpu.roll(z, ROW - 1, 0))
        cw = cw_ref[...]
        conv = zm * cw[0:1, :] + z * cw[1:2, :] + zp * cw[2:3, :]
        mix_s[:, HEADS * V_HEAD:HEADS * V_HEAD + CONV_W] = (gb_s[...] * conv).astype(BF16)

        def at_qk(c):
            keys = kf_s[c["h"], pl.ds(c["k0"], seg + n_past), :]
            c["s"] = _dot_nt(qf_s[c["h"], pl.ds(c["r0"], TQ), :], keys)

        def at_max(c):
            c["m"] = jnp.max(c["s"], axis=-1, keepdims=True)

        def at_exp(c):
            e = jnp.exp2(c.pop("s") - c.pop("m"))
            c["l"] = jnp.sum(e, axis=-1, keepdims=True)
            c["p"] = e.astype(BF16)

        def at_pv(c):
            c["o"] = _dot(c.pop("p"), v_s[c["h"], pl.ds(c["k0"], seg + n_past), :])

        def at_out(c):
            o = c.pop("o") / c.pop("l")
            mix_s[pl.ds(c["r0"], TQ), c["h"] * V_HEAD:(c["h"] + 1) * V_HEAD] = o.astype(BF16)

        at_stages = [at_qk, lambda c: (at_max(c), at_exp(c)), lambda c: (at_pv(c), at_out(c))]
        if ctx:
            _skewed(at_stages, [dict(r0=b * TQ, k0=b * TQ, h=h) for b in range(ROW // TQ) for h in range(HEADS)])
        else:
            def at_block(b, carry):
                r0 = pl.multiple_of(b * TQ, TQ)
                _skewed(at_stages, [dict(r0=r0, k0=0, h=h) for h in range(HEADS)])
                return carry

            lax.fori_loop(0, ROW // TQ, at_block, 0)

        wr = wr_ref[...]
        wr_hi = wr.astype(BF16)
        wr_hl = jnp.concatenate([wr_hi, (wr - wr_hi.astype(F32)).astype(BF16)], axis=0)

        def p2_proj(c):
            c["mo"] = _dot(mix_s[pl.ds(c["r0"], TB), :], wout_ref[...])

        def p2_norm(c):
            r0 = c["r0"]
            xm = x_ref[pl.ds(r0, TB), :] + g1 * c.pop("mo")
            xmid_ref[pl.ds(r0, TB), :] = xm
            h2 = _rms(xm, g2n_ref[...]) * (1.0 + sc2) + sh2
            h2_hi = h2.astype(BF16)
            h2_ref[pl.ds(r0, TB), :] = h2_hi
            c["hi"], c["lo"] = h2_hi, (h2 - h2_hi.astype(F32)).astype(BF16)

        def p2_route(c):
            la = _dot_nt(wr_hl, c["hi"])
            c["lg"] = la[0:N_EXP] + la[N_EXP:2 * N_EXP] + _dot_nt(wr_hi, c.pop("lo"))
            c.pop("hi")

        def p2_aff(c):
            lg = c.pop("lg")
            e = jnp.exp(lg - jnp.max(lg, axis=0, keepdims=True))
            aff_ref[:, pl.ds(c["r0"], TB)] = e / jnp.sum(e, axis=0, keepdims=True)

        _skewed([p2_proj, p2_norm, p2_route, p2_aff], [dict(r0=i * TB) for i in range(nblk)])

    is_ctx = pl.program_id(0) < P_ROWS

    @pl.when(is_ctx)
    def _():
        run(True)

    @pl.when(jnp.logical_not(is_ctx))
    def _():
        run(False)


def _layer_spec(shape):
    nd = len(shape)
    return pl.BlockSpec((None,) + shape, lambda r, l: (l[0],) + (0,) * nd, pipeline_mode=pl.Buffered(1))


def _layer_arg(l):
    return jnp.full((1,), l, I32)


def _mixer_call(l, x, mods, wts, rope_tab, caches):
    lk = PAST_LEN + ROW
    lat = lambda r: jnp.maximum(r - P_ROWS, 0)
    ctx = lambda r: jnp.minimum(r, P_ROWS - 1)
    split_x = isinstance(x, tuple)
    if split_x:
        x_specs = [pl.BlockSpec((None, ROW, D), lambda r, l: (ctx(r), 0, 0)),
                   pl.BlockSpec((None, ROW, D), lambda r, l: (lat(r), 0, 0))]
    else:
        x_specs, x = [pl.BlockSpec((None, ROW, D), lambda r, l: (r, 0, 0))], (x,)
    in_specs = x_specs + [
        _layer_spec((MOD_ROWS, 6 * D)),
        _layer_spec((1, D)), _layer_spec((1, D)),
        _layer_spec((D, C_END)),
        _layer_spec((1, Q_RANK)), _layer_spec((1, KV_RANK)),
        _layer_spec((Q_RANK, U_END)),
        _layer_spec((KV_RANK, HEADS * (QK_NOPE + V_HEAD))),
        _layer_spec((3, CONV_W)),
        _layer_spec((CMLP_G * CHUNK, CHUNK)),
        _layer_spec((CHUNK, CMLP_W)),
        _layer_spec((D, D)),
        _layer_spec((N_EXP, D)),
        pl.BlockSpec((ROW, LANES), lambda r, l: (0, 0), pipeline_mode=pl.Buffered(1)),
        pl.BlockSpec((None, None, PAST_LEN, KV_RANK), lambda r, l: (lat(r), l[0], 0, 0)),
        pl.BlockSpec((None, None, QK_ROPE, PAST_LEN), lambda r, l: (lat(r), l[0], 0, 0)),
    ]
    args = list(x) + [mods, wts["norm1_g"], wts["norm2_g"], wts["w_in"], wts["q_norm_g"], wts["kv_norm_g"],
            wts["w_uq"], wts["w_ukv"], wts["conv_w"], wts["spatial_w"], wts["spatial_b"],
            wts["w_out"], wts["w_router_t"], rope_tab, caches[0], caches[1]]
    out_shape = [jax.ShapeDtypeStruct((N_ROWS, ROW, D), F32),
                 jax.ShapeDtypeStruct((N_ROWS, ROW, D), BF16),
                 jax.ShapeDtypeStruct((N_ROWS, N_EXP, ROW), F32),
                 jax.ShapeDtypeStruct((P_ROWS, ROW, KV_RANK), F32),
                 jax.ShapeDtypeStruct((BATCH, QK_ROPE, SEQ), F32)]
    out_specs = [pl.BlockSpec((None, ROW, D), lambda r, l: (r, 0, 0)),
                 pl.BlockSpec((None, ROW, D), lambda r, l: (r, 0, 0)),
                 pl.BlockSpec((None, N_EXP, ROW), lambda r, l: (r, 0, 0)),
                 pl.BlockSpec((None, ROW, KV_RANK), lambda r, l: (ctx(r), 0, 0)),
                 pl.BlockSpec((P_PER_ROW, QK_ROPE, SEQ), lambda r, l: (ctx(r), 0, 0))]
    return pl.pallas_call(
        _make_mixer_kernel(split_x),
        out_shape=out_shape,
        grid_spec=pltpu.PrefetchScalarGridSpec(
            num_scalar_prefetch=1,
            grid=(N_ROWS,),
            in_specs=in_specs,
            out_specs=out_specs,
            scratch_shapes=[
                pltpu.VMEM((HEADS, ROW, QK_PAD), BF16),
                pltpu.VMEM((HEADS, lk, QK_PAD), BF16),
                pltpu.VMEM((HEADS, lk, V_HEAD), BF16),
                pltpu.VMEM((ROW, D), BF16),
                pltpu.VMEM((ROW, CONV_W), F32),
                pltpu.VMEM((ROW, CONV_W), F32),
            ]),
        compiler_params=pltpu.CompilerParams(
            dimension_semantics=("arbitrary",), vmem_limit_bytes=VMEM_LIMIT),
        name="mixer",
    )(_layer_arg(l), *args)


def _cap_thresholds(affs, caps):
    def bit_step(i, ts):
        bit = jnp.left_shift(jnp.int32(1), 30 - i)
        out = []
        for aff, cap, t in zip(affs, caps, ts):
            cand = t | bit
            cnt = jnp.sum(jnp.where(aff >= pltpu.bitcast(cand, F32), 1.0, 0.0), axis=1, keepdims=True)
            out.append(jnp.where(cnt >= float(cap), cand, t))
        return tuple(out)

    ts = lax.fori_loop(0, 31, bit_step, tuple(jnp.zeros((a.shape[0], 1), I32) for a in affs))
    return [pltpu.bitcast(t, F32) for t in ts]


def _select_slots(aff, thr, cap, tri):
    capf = float(cap)
    gt = aff > thr
    eq = aff == thr
    n_gt = jnp.sum(jnp.where(gt, 1.0, 0.0), axis=1, keepdims=True)
    eq_rank = _dot(jnp.where(eq, 1.0, 0.0).astype(BF16), tri)
    sel = jnp.where(gt, 1.0, jnp.where(eq, jnp.where(eq_rank < capf - n_gt, 1.0, 0.0), 0.0))
    pos = _dot(sel.astype(BF16), tri)
    return jnp.where(sel > 0.5, pos.astype(I32), -1)


def _route_kernel(aff_ref, slot_ref, tri_s):
    r = lax.broadcasted_iota(I32, (ROW, ROW), 0)
    c = lax.broadcasted_iota(I32, (ROW, ROW), 1)
    tri_s[...] = jnp.where(r < c, 1.0, 0.0).astype(BF16)
    np_ = P_ROWS * N_EXP
    affs = [aff_ref[0:np_, s * SEQ:(s + 1) * SEQ] for s in range(P_PER_ROW)] + [aff_ref[np_:, :]]
    caps = [CAP_P] * P_PER_ROW + [CAP]
    thrs = _cap_thresholds(affs, caps)
    for s in range(P_PER_ROW):
        sl = _select_slots(affs[s], thrs[s], CAP_P, tri_s[0:SEQ, 0:SEQ])
        slot_ref[0:np_, s * SEQ:(s + 1) * SEQ] = jnp.where(sl >= 0, sl + s * CAP_P, -1)
    slot_ref[np_:, :] = _select_slots(affs[-1], thrs[-1], CAP, tri_s[...])


def _route_call(aff):
    rows = N_ROWS * N_EXP
    return pl.pallas_call(
        _route_kernel,
        out_shape=jax.ShapeDtypeStruct((rows, ROW), I32),
        scratch_shapes=[pltpu.VMEM((ROW, ROW), BF16)],
        compiler_params=pltpu.CompilerParams(vmem_limit_bytes=VMEM_LIMIT),
        name="route",
    )(aff)


GATHER_NC = 256


def _gather_kernel(slot_ref, h2_ref, xs_ref, hot_s):
    @pl.when(pl.program_id(0) >= P_ROWS)
    def _():
        j = lax.broadcasted_iota(I32, (CAP, ROW), 0)
        for e in range(N_EXP):
            hot_s[e * CAP:(e + 1) * CAP, :] = jnp.where(slot_ref[e:e + 1, :] == j, 1.0, 0.0).astype(BF16)
        for c in range(D // GATHER_NC):
            xs = _dot(hot_s[...], h2_ref[:, c * GATHER_NC:(c + 1) * GATHER_NC]).astype(BF16)
            for e in range(N_EXP):
                xs_ref[e, :, c * GATHER_NC:(c + 1) * GATHER_NC] = xs[e * CAP:(e + 1) * CAP, :]

    @pl.when(pl.program_id(0) < P_ROWS)
    def _():
        j = lax.broadcasted_iota(I32, (CAP_P, SEQ), 0)
        for s in range(P_PER_ROW):
            toks = slice(s * SEQ, (s + 1) * SEQ)
            for e in range(N_EXP):
                hot_s[e * CAP_P:(e + 1) * CAP_P, 0:SEQ] = jnp.where(
                    slot_ref[e:e + 1, toks] == j + s * CAP_P, 1.0, 0.0).astype(BF16)
            xs = _dot(hot_s[0:N_EXP * CAP_P, 0:SEQ], h2_ref[toks, :]).astype(BF16)
            for e in range(N_EXP):
                xs_ref[e, s * CAP_P:(s + 1) * CAP_P, :] = xs[e * CAP_P:(e + 1) * CAP_P, :]


def _gather_call(slot, h2):
    return pl.pallas_call(
        _gather_kernel,
        out_shape=jax.ShapeDtypeStruct((N_EXP, N_ROWS, CAP, D), BF16),
        grid=(N_ROWS,),
        in_specs=[
            pl.BlockSpec((N_EXP, ROW), lambda r: (r, 0)),
            pl.BlockSpec((None, ROW, D), lambda r: (r, 0, 0)),
        ],
        out_specs=pl.BlockSpec((N_EXP, None, CAP, D), lambda r: (0, r, 0, 0)),
        scratch_shapes=[pltpu.VMEM((N_EXP * CAP, ROW), BF16)],
        compiler_params=pltpu.CompilerParams(
            dimension_semantics=("arbitrary",), vmem_limit_bytes=VMEM_LIMIT),
        name="gather",
    )(slot, h2)


EXP_ROWS = 4


def _expert_kernel(l_ref, slot_ref, aff_ref, xs_ref, wg_ref, wu_ref, wd_ref, ye_ref, wg_s, wu_s, wd_s):
    e = pl.program_id(0)
    wg_s[...] = wg_ref[...].astype(BF16)
    wu_s[...] = wu_ref[...].astype(BF16)
    wd_s[...] = wd_ref[...].astype(BF16)
    j = lax.broadcasted_iota(I32, (CAP, ROW), 0)

    def gates(c):
        c["gate"] = []
        for k in range(EXP_ROWS):
            row = (c["r0"] + k) * N_EXP + e
            hit = slot_ref[pl.ds(row, 1), :] == j
            c["gate"].append(jnp.sum(jnp.where(hit, aff_ref[pl.ds(row, 1), :], 0.0), axis=1, keepdims=True))

    def up(c):
        xs = xs_ref[c["r0"]:c["r0"] + EXP_ROWS].reshape(EXP_ROWS * CAP, D)
        c["a"], c["u"] = _dot(xs, wg_s[...]), _dot(xs, wu_s[...])

    def act(c):
        c["hid"] = (_silu(c.pop("a")) * c.pop("u")).astype(BF16)

    def down(c):
        c["ye"] = _dot(c.pop("hid"), wd_s[...])

    def out(c):
        ye, gate = c.pop("ye"), c.pop("gate")
        for k in range(EXP_ROWS):
            ye_ref[c["r0"] + k] = (ye[k * CAP:(k + 1) * CAP, :] * gate[k]).astype(BF16)

    _skewed([gates, up, act, down, out], [dict(r0=g * EXP_ROWS) for g in range(N_ROWS // EXP_ROWS)])


def _expert_call(l, slot, aff, xs, w_gate, w_up, w_down):
    rows = N_ROWS * N_EXP
    return pl.pallas_call(
        _expert_kernel,
        out_shape=jax.ShapeDtypeStruct((N_EXP, N_ROWS, CAP, D), BF16),
        grid_spec=pltpu.PrefetchScalarGridSpec(
            num_scalar_prefetch=1,
            grid=(N_EXP,),
            in_specs=[
                pl.BlockSpec((rows, ROW), lambda e, l: (0, 0), pipeline_mode=pl.Buffered(1)),
                pl.BlockSpec((rows, ROW), lambda e, l: (0, 0), pipeline_mode=pl.Buffered(1)),
                pl.BlockSpec((None, N_ROWS, CAP, D), lambda e, l: (e, 0, 0, 0)),
                pl.BlockSpec((None, None, D, FF), lambda e, l: (l[0], e, 0, 0)),
                pl.BlockSpec((None, None, D, FF), lambda e, l: (l[0], e, 0, 0)),
                pl.BlockSpec((None, None, FF, D), lambda e, l: (l[0], e, 0, 0)),
            ],
            out_specs=pl.BlockSpec((None, N_ROWS, CAP, D), lambda e, l: (e, 0, 0, 0)),
            scratch_shapes=[pltpu.VMEM((D, FF), BF16), pltpu.VMEM((D, FF), BF16), pltpu.VMEM((FF, D), BF16)]),
        compiler_params=pltpu.CompilerParams(
            dimension_semantics=("arbitrary",), vmem_limit_bytes=VMEM_LIMIT),
        name="experts",
    )(_layer_arg(l), slot, aff, xs, w_gate, w_up, w_down)


COMB_TC = 256


def _make_combine_kernel(final, first_row):
    def kern(*refs):
        if final:
            _, slot_ref, ye_ref, xmid_ref, g2_ref, fg_ref, o_ref = refs
        else:
            _, slot_ref, ye_ref, xmid_ref, g2_ref, o_ref = refs
        def run(ctx):
            mod_row = CTX_MOD_ROW if ctx else first_row + pl.program_id(0) - P_ROWS
            g2 = g2_ref[pl.ds(mod_row, 1), :]
            nslot = CAP_P if ctx else CAP
            j = lax.broadcasted_iota(I32, (nslot, COMB_TC), 0)
            if not ctx:
                ye = ye_ref[...].reshape(N_EXP * CAP, D)
            for t in range(ROW // COMB_TC):
                cols = slice(t * COMB_TC, (t + 1) * COMB_TC)
                s0 = t * CAP_P if ctx else 0
                onehot = jnp.concatenate(
                    [jnp.where(slot_ref[e:e + 1, cols] == j + s0, 1.0, 0.0).astype(BF16) for e in range(N_EXP)],
                    axis=0)
                if ctx:
                    ye = jnp.concatenate([ye_ref[e, s0:s0 + CAP_P, :] for e in range(N_EXP)], axis=0)
                moe = lax.dot_general(onehot, ye, TN, preferred_element_type=F32)
                xn = xmid_ref[cols, :] + g2 * moe
                if final:
                    xn = _rms(xn, fg_ref[...])
                o_ref[cols, :] = xn

        is_ctx = first_row + pl.program_id(0) < P_ROWS

        @pl.when(is_ctx)
        def _():
            run(True)

        @pl.when(jnp.logical_not(is_ctx))
        def _():
            run(False)

    return kern


def _combine_call(l, slot, ye, xmid, mods, final_g, row0, n_rows):
    final = final_g is not None
    in_specs = [
        pl.BlockSpec((N_EXP, ROW), lambda r, l: (row0 + r, 0)),
        pl.BlockSpec((N_EXP, None, CAP, D), lambda r, l: (0, row0 + r, 0, 0)),
        pl.BlockSpec((None, ROW, D), lambda r, l: (row0 + r, 0, 0)),
        pl.BlockSpec((None, MOD_ROWS, D), lambda r, l: (l[0], 0, 5), pipeline_mode=pl.Buffered(1)),
    ]
    args = [slot, ye, xmid, mods]
    if final:
        in_specs.append(pl.BlockSpec((1, D), lambda r, l: (0, 0)))
        args.append(final_g)
    return pl.pallas_call(
        _make_combine_kernel(final, row0),
        out_shape=jax.ShapeDtypeStruct((n_rows, ROW, D), F32),
        grid_spec=pltpu.PrefetchScalarGridSpec(
            num_scalar_prefetch=1,
            grid=(n_rows,),
            in_specs=in_specs,
            out_specs=pl.BlockSpec((None, ROW, D), lambda r, l: (r, 0, 0))),
        compiler_params=pltpu.CompilerParams(
            dimension_semantics=("arbitrary",), vmem_limit_bytes=VMEM_LIMIT),
        name="combine",
    )(_layer_arg(l), *args)


def _rope_rot(w, axis):
    q = QK_ROPE // 4
    part = lambda a, b: lax.slice_in_dim(w, a * q, b * q, axis=axis)
    return jnp.concatenate([-part(1, 2), part(0, 1), -part(3, 4), part(2, 3)], axis=axis)


def _repack_kernel(win_ref, wuq_ref, wout_ref, win_o, wuq_o, wout_o):
    o_conv = Q_RANK + KV_RANK + QK_ROPE
    wt = win_ref[...]
    win_o[...] = jnp.concatenate(
        [wt[:o_conv], _rope_rot(wt[o_conv - QK_ROPE:o_conv], 0), wt[o_conv:]], axis=0).T.astype(BF16)
    u = wuq_ref[...]
    hd = QK_NOPE + QK_ROPE
    pe = [u[:, h * hd + QK_NOPE:(h + 1) * hd] for h in range(HEADS)]
    wuq_o[...] = jnp.concatenate(
        [u[:, h * hd:h * hd + QK_NOPE] for h in range(HEADS)]
        + [piece for h in range(HEADS) for piece in (pe[h], _rope_rot(pe[h], 1))], axis=1).astype(BF16)
    wout_o[...] = wout_ref[...].astype(BF16)


def _repack_call(w_in, w_uq, w_out):
    in_w, uq_w = w_in.shape[-1], w_uq.shape[-1]
    w_in = jnp.swapaxes(w_in, 1, 2)
    return pl.pallas_call(
        _repack_kernel,
        out_shape=[jax.ShapeDtypeStruct((DEPTH, D, C_END), BF16),
                   jax.ShapeDtypeStruct((DEPTH, Q_RANK, U_END), BF16),
                   jax.ShapeDtypeStruct((DEPTH, D, D), BF16)],
        grid=(DEPTH,),
        in_specs=[pl.BlockSpec((None, in_w, D), lambda l: (l, 0, 0)),
                  pl.BlockSpec((None, Q_RANK, uq_w), lambda l: (l, 0, 0)),
                  pl.BlockSpec((None, D, D), lambda l: (l, 0, 0))],
        out_specs=[pl.BlockSpec((None, D, C_END), lambda l: (l, 0, 0)),
                   pl.BlockSpec((None, Q_RANK, U_END), lambda l: (l, 0, 0)),
                   pl.BlockSpec((None, D, D), lambda l: (l, 0, 0))],
        compiler_params=pltpu.CompilerParams(
            dimension_semantics=("arbitrary",), vmem_limit_bytes=VMEM_LIMIT),
        name="repack",
    )(w_in, w_uq, w_out)


def _prep_weights(norm1_g, norm2_g, w_in, q_norm_g, kv_norm_g, w_uq, w_ukv, conv_w, spatial_w,
                  spatial_b, w_out, w_router):
    w_in_p, w_uq_p, w_out_p = _repack_call(w_in, w_uq, w_out)
    sb = jnp.repeat(jnp.swapaxes(spatial_b, 1, 2), CMLP_W // CMLP_G, axis=-1)
    return dict(
        norm1_g=norm1_g.reshape(DEPTH, 1, D), norm2_g=norm2_g.reshape(DEPTH, 1, D),
        w_in=w_in_p, q_norm_g=q_norm_g.reshape(DEPTH, 1, Q_RANK),
        kv_norm_g=kv_norm_g.reshape(DEPTH, 1, KV_RANK), w_uq=w_uq_p, w_ukv=w_ukv.astype(BF16),
        conv_w=conv_w, spatial_w=spatial_w.reshape(DEPTH, CMLP_G * CHUNK, CHUNK).astype(BF16),
        spatial_b=sb, w_out=w_out_p, w_router_t=jnp.swapaxes(w_router, 1, 2))


def _rope_table(n):
    rows = n // GRID_W
    row = jnp.repeat(jnp.arange(rows), GRID_W).astype(F32)
    col = jnp.tile(jnp.arange(GRID_W), rows).astype(F32)
    n_freq = QK_ROPE // 4
    inv = ROPE_BASE ** (-jnp.arange(n_freq, dtype=F32) / n_freq)
    ang_r, ang_c = row[:, None] * inv, col[:, None] * inv
    ang = jnp.concatenate([ang_r, ang_r, ang_c, ang_c], axis=-1)
    return jnp.concatenate([jnp.cos(ang), jnp.sin(ang)], axis=-1)


def kernel(x_prompt, x_sample, cache_ckv, cache_krope, c, c_ctx, w_ada, b_ada, norm1_g, norm2_g, w_in,
           q_norm_g, kv_norm_g, w_uq, w_ukv, conv_w, spatial_w, spatial_b, w_out, w_router, w_gate,
           w_up, w_down, final_norm_g):
    wts = _prep_weights(norm1_g, norm2_g, w_in, q_norm_g, kv_norm_g, w_uq, w_ukv, conv_w, spatial_w,
                        spatial_b, w_out, w_router)
    rope_tab = _rope_table(DEC_SEQ)
    krope_t = jnp.swapaxes(cache_krope, 2, 3)
    cond = jnp.concatenate(
        [c, c_ctx[None, :], jnp.zeros((MOD_ROWS - DEC_BATCH - 1, D), F32)], axis=0)
    mods = _ada_call(cond, w_ada, b_ada)
    final_g = final_norm_g.reshape(1, D)

    x = (x_prompt.reshape(P_ROWS, ROW, D), x_sample)
    ckv_list, kpe_list = [], []
    for l in range(DEPTH):
        xmid, h2, aff, ckv, kpe = _mixer_call(l, x, mods, wts, rope_tab, (cache_ckv, krope_t))
        ckv_list.append(ckv.reshape(BATCH, SEQ, KV_RANK))
        kpe_list.append(kpe)
        aff = aff.reshape(N_ROWS * N_EXP, ROW)
        slot = _route_call(aff)
        ye = _expert_call(l, slot, aff, _gather_call(slot, h2), w_gate, w_up, w_down)
        if l < DEPTH - 1:
            x = _combine_call(l, slot, ye, xmid, mods, None, 0, N_ROWS)
        else:
            y_prompt = _combine_call(l, slot, ye, xmid, mods, final_g, 0, P_ROWS)
            y_sample = _combine_call(l, slot, ye, xmid, mods, final_g, P_ROWS, DEC_BATCH)
    return (y_prompt.reshape(BATCH, SEQ, D), y_sample,
            jnp.stack(ckv_list, axis=1), jnp.swapaxes(jnp.stack(kpe_list, axis=1), 2, 3))
```

```python
import jax
import jax.numpy as jnp
from jax import lax
from jax.experimental import pallas as pl
from jax.experimental.pallas import tpu as pltpu

F32 = jnp.float32
BF16 = jnp.bfloat16
I32 = jnp.int32

D = 1024
BATCH = 16
SEQ = 256
DEPTH = 4
DEC_BATCH = 8
DEC_SEQ = 1024
PAST_LEN = 512
GRID_W = 64
HEADS = 4
QK_NOPE = 128
QK_ROPE = 64
V_HEAD = 128
Q_RANK = 256
KV_RANK = 128
CONV_W = 256
CMLP_W = 256
CMLP_G = 4
CHUNK = 128
N_EXP = 16
EC_FACTOR = 2
FF = D // 2
ROPE_BASE = 10000.0
EPS = 1e-6
LOG2_E = 1.4426950408889634

LANES = 128
ROW = 1024
N_ROWS = (BATCH * SEQ + DEC_BATCH * DEC_SEQ) // ROW
P_ROWS = BATCH * SEQ // ROW
P_PER_ROW = ROW // SEQ
CAP = EC_FACTOR * ROW // N_EXP
CAP_P = EC_FACTOR * SEQ // N_EXP
CTX_MOD_ROW = DEC_BATCH
MOD_ROWS = 16
QK_PAD = QK_NOPE + LANES

C_Q, C_KV, C_KPE, C_CONV, C_CMLP, C_END = 0, 256, 384, 512, 1280, 1792
U_NOPE, U_PE, U_END = 0, HEADS * QK_NOPE, HEADS * (QK_NOPE + LANES)
TB = 512
assert 2 * QK_ROPE == LANES and TB % SEQ == 0
TQ = 256
VMEM_LIMIT = 56 * 1024 * 1024

NT = (((1,), (1,)), ((), ()))
TN = (((0,), (0,)), ((), ()))


def _rms(x, g):
    return x * lax.rsqrt(jnp.mean(x * x, axis=-1, keepdims=True) + EPS) * g


def _silu(x):
    return x / (1.0 + jnp.exp(-x))


def _dot(a, b):
    return jnp.dot(a, b, preferred_element_type=F32)


def _dot_nt(a, b):
    return lax.dot_general(a, b, NT, preferred_element_type=F32)


def _skewed(stages, items):
    n = len(stages)
    for t in range(len(items) + n - 1):
        for s in range(n - 1, -1, -1):
            if 0 <= t - s < len(items):
                stages[s](items[t - s])


def _ada_kernel(c_ref, w_ref, b_ref, o_ref):
    s = _silu(c_ref[...]).astype(BF16)
    o_ref[...] = _dot(s, w_ref[...].astype(BF16)) + b_ref[...]


def _ada_call(cond, w_ada, b_ada):
    nc, wc = 2, 3 * D
    return pl.pallas_call(
        _ada_kernel,
        out_shape=jax.ShapeDtypeStruct((DEPTH, MOD_ROWS, 6 * D), F32),
        grid=(DEPTH, nc),
        in_specs=[
            pl.BlockSpec((MOD_ROWS, D), lambda l, j: (0, 0)),
            pl.BlockSpec((None, D, wc), lambda l, j: (l, 0, j)),
            pl.BlockSpec((None, 1, wc), lambda l, j: (l, 0, j)),
        ],
        out_specs=pl.BlockSpec((None, MOD_ROWS, wc), lambda l, j: (l, 0, j)),
        compiler_params=pltpu.CompilerParams(
            dimension_semantics=("arbitrary", "arbitrary"), vmem_limit_bytes=VMEM_LIMIT),
        name="ada_mod",
    )(cond, w_ada, b_ada.reshape(DEPTH, 1, 6 * D))


def _make_mixer_kernel(split_x):
    def kern(l_ref, *refs):
        if split_x:
            _mixer_body(*refs)
        else:
            _mixer_body(refs[0], *refs)
    return kern


def _mixer_body(xp_ref, xs_ref, mod_ref, g1n_ref, g2n_ref, win_ref, qg_ref, kvg_ref, wuq_ref, wukv_ref,
                cw_ref, sw_ref, sb_ref, wout_ref, wr_ref, cs_ref, cckv_ref, ckr_ref,
                xmid_ref, h2_ref, aff_ref, ckv_out, kpe_out,
                qf_s, kf_s, v_s, mix_s, gb_s, z_s):
    qk_scale = float(QK_NOPE + QK_ROPE) ** -0.5 * LOG2_E
    nblk = ROW // TB

    def run(ctx):
        x_ref = xp_ref if ctx else xs_ref
        n_past = 0 if ctx else PAST_LEN
        seg = SEQ if ctx else ROW
        mod_row = CTX_MOD_ROW if ctx else pl.program_id(0) - P_ROWS
        mod = mod_ref[pl.ds(mod_row, 1), :]
        sh1, sc1, g1 = mod[:, 0:D], mod[:, D:2 * D], mod[:, 2 * D:3 * D]
        sh2, sc2, g2 = mod[:, 3 * D:4 * D], mod[:, 4 * D:5 * D], mod[:, 5 * D:6 * D]

        def store_kv(row0, rows, kv, kpb):
            for h in range(HEADS):
                c0 = h * (QK_NOPE + V_HEAD)
                kf_s[h, pl.ds(row0, rows), 0:QK_NOPE] = kv[:, c0:c0 + QK_NOPE].astype(BF16)
                kf_s[h, pl.ds(row0, rows), QK_NOPE:QK_PAD] = kpb
                v_s[h, pl.ds(row0, rows), :] = kv[:, c0 + QK_NOPE:c0 + QK_NOPE + V_HEAD].astype(BF16)

        if not ctx:
            kr_t = ckr_ref[...]
            store_kv(0, PAST_LEN, _dot(cckv_ref[...].astype(BF16), wukv_ref[...]),
                     jnp.concatenate([kr_t, jnp.zeros_like(kr_t)], axis=0).T.astype(BF16))

        grp_shift = (CMLP_W // CMLP_G).bit_length() - 1
        lane_grp = lax.shift_right_logical(lax.broadcasted_iota(I32, (CHUNK, CMLP_W), 1), grp_shift)

        def p1_norm(c):
            x = x_ref[pl.ds(c["r0"], TB), :]
            c["hb"] = (_rms(x, g1n_ref[...]) * (1.0 + sc1) + sh1).astype(BF16)

        def p1_proj(c):
            c["pa"] = _dot(c["hb"], win_ref[:, C_Q:C_CONV])
            c["cv"] = _dot(c["hb"], win_ref[:, C_CONV:C_CMLP])
            c["cm"] = _dot(c.pop("hb"), win_ref[:, C_CMLP:C_END])

        def p1_mid(c):
            r0, pa, cv, cm = c["r0"], c.pop("pa"), c.pop("cv"), c.pop("cm")
            kp = pa[:, C_KPE:C_CONV]
            c["cqn"] = _rms(pa[:, C_Q:C_KV], qg_ref[...]).astype(BF16)
            ckv_n = _rms(pa[:, C_KV:C_KPE], kvg_ref[...])
            if ctx:
                ckv_out[pl.ds(r0, TB), :] = ckv_n
                kp_t = kp.T
                for i in range(TB // SEQ):
                    kpe_out[r0 // SEQ + i] = kp_t[0:QK_ROPE, i * SEQ:(i + 1) * SEQ]
                kpe = kp
            else:
                y = kp * cs_ref[pl.ds(r0, TB), :]
                kpe = y + pltpu.roll(y, QK_ROPE, 1)
            c["ckv_b"], c["kpe_b"] = ckv_n.astype(BF16), kpe.astype(BF16)
            gb_s[pl.ds(r0, TB), :] = cv[:, 0:CONV_W]
            z_s[pl.ds(r0, TB), :] = cv[:, CONV_W:2 * CONV_W] * cv[:, 2 * CONV_W:3 * CONV_W]
            c["u"], c["vvb"] = cm[:, 0:CMLP_W], cm[:, CMLP_W:2 * CMLP_W].astype(BF16)

        def p1_up(c):
            c["qa"] = _dot(c.pop("cqn"), wuq_ref[...])
            c["kv"] = _dot(c.pop("ckv_b"), wukv_ref[...])
            vvb = c.pop("vvb")
            c["r"] = [_dot(sw_ref[...], vvb[k * CHUNK:(k + 1) * CHUNK, :]) for k in range(TB // CHUNK)]

        def p1_out(c):
            r0, qa, u = c["r0"], c.pop("qa"), c.pop("u")
            rope_lanes = lax.broadcasted_iota(I32, (TB, LANES), 1) < QK_ROPE
            for h in range(HEADS):
                qn = qa[:, U_NOPE + h * QK_NOPE:U_NOPE + (h + 1) * QK_NOPE]
                qp = qa[:, U_PE + h * LANES:U_PE + (h + 1) * LANES]
                if not ctx:
                    y = qp * cs_ref[pl.ds(r0, TB), :]
                    qp = y + pltpu.roll(y, QK_ROPE, 1)
                qp = jnp.where(rope_lanes, qp, 0.0)
                qf_s[h, pl.ds(r0, TB), 0:QK_NOPE] = (qn * qk_scale).astype(BF16)
                qf_s[h, pl.ds(r0, TB), QK_NOPE:QK_PAD] = (qp * qk_scale).astype(BF16)
            store_kv(n_past + r0, TB, c.pop("kv"), c.pop("kpe_b"))
            for k, r in enumerate(c.pop("r")):
                mixed = jnp.where(
                    lane_grp == 0, r[0:CHUNK],
                    jnp.where(lane_grp == 1, r[CHUNK:2 * CHUNK],
                              jnp.where(lane_grp == 2, r[2 * CHUNK:3 * CHUNK], r[3 * CHUNK:4 * CHUNK])))
                out = u[k * CHUNK:(k + 1) * CHUNK, :] * (mixed + sb_ref[...])
                mix_s[pl.ds(r0 + k * CHUNK, CHUNK), HEADS * V_HEAD + CONV_W:D] = out.astype(BF16)

        _skewed([p1_norm, p1_proj, p1_mid, p1_up, p1_out], [dict(r0=i * TB) for i in range(nblk)])

        z = z_s[...]
        pos = lax.broadcasted_iota(I32, (ROW, CONV_W), 0) & (seg - 1)
        zm = jnp.where(pos == 0, 0.0, pltpu.roll(z, 1, 0))
        zp = jnp.where(pos == seg - 1, 0.0, pltpu.roll(z, ROW - 1, 0))
        cw = cw_ref[...]
        conv = zm * cw[0:1, :] + z * cw[1:2, :] + zp * cw[2:3, :]
        mix_s[:, HEADS * V_HEAD:HEADS * V_HEAD + CONV_W] = (gb_s[...] * conv).astype(BF16)

        def at_qk(c):
            keys = kf_s[c["h"], pl.ds(c["k0"], seg + n_past), :]
            c["s"] = _dot_nt(qf_s[c["h"], pl.ds(c["r0"], TQ), :], keys)

        def at_max(c):
            c["m"] = jnp.max(c["s"], axis=-1, keepdims=True)

        def at_exp(c):
            e = jnp.exp2(c.pop("s") - c.pop("m"))
            c["l"] = jnp.sum(e, axis=-1, keepdims=True)
            c["p"] = e.astype(BF16)

        def at_pv(c):
            c["o"] = _dot(c.pop("p"), v_s[c["h"], pl.ds(c["k0"], seg + n_past), :])

        def at_out(c):
            o = c.pop("o") / c.pop("l")
            mix_s[pl.ds(c["r0"], TQ), c["h"] * V_HEAD:(c["h"] + 1) * V_HEAD] = o.astype(BF16)

        at_stages = [at_qk, lambda c: (at_max(c), at_exp(c)), lambda c: (at_pv(c), at_out(c))]
        if ctx:
            _skewed(at_stages, [dict(r0=b * TQ, k0=b * TQ, h=h) for b in range(ROW // TQ) for h in range(HEADS)])
        else:
            def at_block(b, carry):
                r0 = pl.multiple_of(b * TQ, TQ)
                _skewed(at_stages, [dict(r0=r0, k0=0, h=h) for h in range(HEADS)])
                return carry

            lax.fori_loop(0, ROW // TQ, at_block, 0)

        wr = wr_ref[...]
        wr_hi = wr.astype(BF16)
        wr_hl = jnp.concatenate([wr_hi, (wr - wr_hi.astype(F32)).astype(BF16)], axis=0)

        def p2_proj(c):
            c["mo"] = _dot(mix_s[pl.ds(c["r0"], TB), :], wout_ref[...])

        def p2_norm(c):
            r0 = c["r0"]
            xm = x_ref[pl.ds(r0, TB), :] + g1 * c.pop("mo")
            xmid_ref[pl.ds(r0, TB), :] = xm
            h2 = _rms(xm, g2n_ref[...]) * (1.0 + sc2) + sh2
            h2_hi = h2.astype(BF16)
            h2_ref[pl.ds(r0, TB), :] = h2_hi
            c["hi"], c["lo"] = h2_hi, (h2 - h2_hi.astype(F32)).astype(BF16)

        def p2_route(c):
            la = _dot_nt(wr_hl, c["hi"])
            c["lg"] = la[0:N_EXP] + la[N_EXP:2 * N_EXP] + _dot_nt(wr_hi, c.pop("lo"))
            c.pop("hi")

        def p2_aff(c):
            lg = c.pop("lg")
            e = jnp.exp(lg - jnp.max(lg, axis=0, keepdims=True))
            aff_ref[:, pl.ds(c["r0"], TB)] = e / jnp.sum(e, axis=0, keepdims=True)

        _skewed([p2_proj, p2_norm, p2_route, p2_aff], [dict(r0=i * TB) for i in range(nblk)])

    is_ctx = pl.program_id(0) < P_ROWS

    @pl.when(is_ctx)
    def _():
        run(True)

    @pl.when(jnp.logical_not(is_ctx))
    def _():
        run(False)


def _layer_spec(shape):
    nd = len(shape)
    return pl.BlockSpec((None,) + shape, lambda r, l: (l[0],) + (0,) * nd, pipeline_mode=pl.Buffered(1))


def _layer_arg(l):
    return jnp.full((1,), l, I32)


def _mixer_call(l, x, mods, wts, rope_tab, caches):
    lk = PAST_LEN + ROW
    lat = lambda r: jnp.maximum(r - P_ROWS, 0)
    ctx = lambda r: jnp.minimum(r, P_ROWS - 1)
    split_x = isinstance(x, tuple)
    if split_x:
        x_specs = [pl.BlockSpec((None, ROW, D), lambda r, l: (ctx(r), 0, 0)),
                   pl.BlockSpec((None, ROW, D), lambda r, l: (lat(r), 0, 0))]
    else:
        x_specs, x = [pl.BlockSpec((None, ROW, D), lambda r, l: (r, 0, 0))], (x,)
    in_specs = x_specs + [
        _layer_spec((MOD_ROWS, 6 * D)),
        _layer_spec((1, D)), _layer_spec((1, D)),
        _layer_spec((D, C_END)),
        _layer_spec((1, Q_RANK)), _layer_spec((1, KV_RANK)),
        _layer_spec((Q_RANK, U_END)),
        _layer_spec((KV_RANK, HEADS * (QK_NOPE + V_HEAD))),
        _layer_spec((3, CONV_W)),
        _layer_spec((CMLP_G * CHUNK, CHUNK)),
        _layer_spec((CHUNK, CMLP_W)),
        _layer_spec((D, D)),
        _layer_spec((N_EXP, D)),
        pl.BlockSpec((ROW, LANES), lambda r, l: (0, 0), pipeline_mode=pl.Buffered(1)),
        pl.BlockSpec((None, None, PAST_LEN, KV_RANK), lambda r, l: (lat(r), l[0], 0, 0)),
        pl.BlockSpec((None, None, QK_ROPE, PAST_LEN), lambda r, l: (lat(r), l[0], 0, 0)),
    ]
    args = list(x) + [mods, wts["norm1_g"], wts["norm2_g"], wts["w_in"], wts["q_norm_g"], wts["kv_norm_g"],
            wts["w_uq"], wts["w_ukv"], wts["conv_w"], wts["spatial_w"], wts["spatial_b"],
            wts["w_out"], wts["w_router_t"], rope_tab, caches[0], caches[1]]
    out_shape = [jax.ShapeDtypeStruct((N_ROWS, ROW, D), F32),
                 jax.ShapeDtypeStruct((N_ROWS, ROW, D), BF16),
                 jax.ShapeDtypeStruct((N_ROWS, N_EXP, ROW), F32),
                 jax.ShapeDtypeStruct((P_ROWS, ROW, KV_RANK), F32),
                 jax.ShapeDtypeStruct((BATCH, QK_ROPE, SEQ), F32)]
    out_specs = [pl.BlockSpec((None, ROW, D), lambda r, l: (r, 0, 0)),
                 pl.BlockSpec((None, ROW, D), lambda r, l: (r, 0, 0)),
                 pl.BlockSpec((None, N_EXP, ROW), lambda r, l: (r, 0, 0)),
                 pl.BlockSpec((None, ROW, KV_RANK), lambda r, l: (ctx(r), 0, 0)),
                 pl.BlockSpec((P_PER_ROW, QK_ROPE, SEQ), lambda r, l: (ctx(r), 0, 0))]
    return pl.pallas_call(
        _make_mixer_kernel(split_x),
        out_shape=out_shape,
        grid_spec=pltpu.PrefetchScalarGridSpec(
            num_scalar_prefetch=1,
            grid=(N_ROWS,),
            in_specs=in_specs,
            out_specs=out_specs,
            scratch_shapes=[
                pltpu.VMEM((HEADS, ROW, QK_PAD), BF16),
                pltpu.VMEM((HEADS, lk, QK_PAD), BF16),
                pltpu.VMEM((HEADS, lk, V_HEAD), BF16),
                pltpu.VMEM((ROW, D), BF16),
                pltpu.VMEM((ROW, CONV_W), F32),
                pltpu.VMEM((ROW, CONV_W), F32),
            ]),
        compiler_params=pltpu.CompilerParams(
            dimension_semantics=("arbitrary",), vmem_limit_bytes=VMEM_LIMIT),
        name="mixer",
    )(_layer_arg(l), *args)


def _cap_thresholds(affs, caps):
    def bit_step(i, ts):
        bit = jnp.left_shift(jnp.int32(1), 30 - i)
        out = []
        for aff, cap, t in zip(affs, caps, ts):
            cand = t | bit
            cnt = jnp.sum(jnp.where(aff >= pltpu.bitcast(cand, F32), 1.0, 0.0), axis=1, keepdims=True)
            out.append(jnp.where(cnt >= float(cap), cand, t))
        return tuple(out)

    ts = lax.fori_loop(0, 31, bit_step, tuple(jnp.zeros((a.shape[0], 1), I32) for a in affs))
    return [pltpu.bitcast(t, F32) for t in ts]


def _select_slots(aff, thr, cap, tri):
    capf = float(cap)
    gt = aff > thr
    eq = aff == thr
    n_gt = jnp.sum(jnp.where(gt, 1.0, 0.0), axis=1, keepdims=True)
    eq_rank = _dot(jnp.where(eq, 1.0, 0.0).astype(BF16), tri)
    sel = jnp.where(gt, 1.0, jnp.where(eq, jnp.where(eq_rank < capf - n_gt, 1.0, 0.0), 0.0))
    pos = _dot(sel.astype(BF16), tri)
    return jnp.where(sel > 0.5, pos.astype(I32), -1)


def _route_kernel(aff_ref, slot_ref, tri_s):
    r = lax.broadcasted_iota(I32, (ROW, ROW), 0)
    c = lax.broadcasted_iota(I32, (ROW, ROW), 1)
    tri_s[...] = jnp.where(r < c, 1.0, 0.0).astype(BF16)
    np_ = P_ROWS * N_EXP
    affs = [aff_ref[0:np_, s * SEQ:(s + 1) * SEQ] for s in range(P_PER_ROW)] + [aff_ref[np_:, :]]
    caps = [CAP_P] * P_PER_ROW + [CAP]
    thrs = _cap_thresholds(affs, caps)
    for s in range(P_PER_ROW):
        sl = _select_slots(affs[s], thrs[s], CAP_P, tri_s[0:SEQ, 0:SEQ])
        slot_ref[0:np_, s * SEQ:(s + 1) * SEQ] = jnp.where(sl >= 0, sl + s * CAP_P, -1)
    slot_ref[np_:, :] = _select_slots(affs[-1], thrs[-1], CAP, tri_s[...])


def _route_call(aff):
    rows = N_ROWS * N_EXP
    return pl.pallas_call(
        _route_kernel,
        out_shape=jax.ShapeDtypeStruct((rows, ROW), I32),
        scratch_shapes=[pltpu.VMEM((ROW, ROW), BF16)],
        compiler_params=pltpu.CompilerParams(vmem_limit_bytes=VMEM_LIMIT),
        name="route",
    )(aff)


GATHER_NC = 256


def _gather_kernel(slot_ref, h2_ref, xs_ref, hot_s):
    @pl.when(pl.program_id(0) >= P_ROWS)
    def _():
        j = lax.broadcasted_iota(I32, (CAP, ROW), 0)
        for e in range(N_EXP):
            hot_s[e * CAP:(e + 1) * CAP, :] = jnp.where(slot_ref[e:e + 1, :] == j, 1.0, 0.0).astype(BF16)
        for c in range(D // GATHER_NC):
            xs = _dot(hot_s[...], h2_ref[:, c * GATHER_NC:(c + 1) * GATHER_NC]).astype(BF16)
            for e in range(N_EXP):
                xs_ref[e, :, c * GATHER_NC:(c + 1) * GATHER_NC] = xs[e * CAP:(e + 1) * CAP, :]

    @pl.when(pl.program_id(0) < P_ROWS)
    def _():
        j = lax.broadcasted_iota(I32, (CAP_P, SEQ), 0)
        for s in range(P_PER_ROW):
            toks = slice(s * SEQ, (s + 1) * SEQ)
            for e in range(N_EXP):
                hot_s[e * CAP_P:(e + 1) * CAP_P, 0:SEQ] = jnp.where(
                    slot_ref[e:e + 1, toks] == j + s * CAP_P, 1.0, 0.0).astype(BF16)
            xs = _dot(hot_s[0:N_EXP * CAP_P, 0:SEQ], h2_ref[toks, :]).astype(BF16)
            for e in range(N_EXP):
                xs_ref[e, s * CAP_P:(s + 1) * CAP_P, :] = xs[e * CAP_P:(e + 1) * CAP_P, :]


def _gather_call(slot, h2):
    return pl.pallas_call(
        _gather_kernel,
        out_shape=jax.ShapeDtypeStruct((N_EXP, N_ROWS, CAP, D), BF16),
        grid=(N_ROWS,),
        in_specs=[
            pl.BlockSpec((N_EXP, ROW), lambda r: (r, 0)),
            pl.BlockSpec((None, ROW, D), lambda r: (r, 0, 0)),
        ],
        out_specs=pl.BlockSpec((N_EXP, None, CAP, D), lambda r: (0, r, 0, 0)),
        scratch_shapes=[pltpu.VMEM((N_EXP * CAP, ROW), BF16)],
        compiler_params=pltpu.CompilerParams(
            dimension_semantics=("arbitrary",), vmem_limit_bytes=VMEM_LIMIT),
        name="gather",
    )(slot, h2)


EXP_ROWS = 4


def _expert_kernel(l_ref, slot_ref, aff_ref, xs_ref, wg_ref, wu_ref, wd_ref, ye_ref, wg_s, wu_s, wd_s):
    e = pl.program_id(0)
    wg_s[...] = wg_ref[...].astype(BF16)
    wu_s[...] = wu_ref[...].astype(BF16)
    wd_s[...] = wd_ref[...].astype(BF16)
    j = lax.broadcasted_iota(I32, (CAP, ROW), 0)

    def gates(c):
        c["gate"] = []
        for k in range(EXP_ROWS):
            row = (c["r0"] + k) * N_EXP + e
            hit = slot_ref[pl.ds(row, 1), :] == j
            c["gate"].append(jnp.sum(jnp.where(hit, aff_ref[pl.ds(row, 1), :], 0.0), axis=1, keepdims=True))

    def up(c):
        xs = xs_ref[c["r0"]:c["r0"] + EXP_ROWS].reshape(EXP_ROWS * CAP, D)
        c["a"], c["u"] = _dot(xs, wg_s[...]), _dot(xs, wu_s[...])

    def act(c):
        c["hid"] = (_silu(c.pop("a")) * c.pop("u")).astype(BF16)

    def down(c):
        c["ye"] = _dot(c.pop("hid"), wd_s[...])

    def out(c):
        ye, gate = c.pop("ye"), c.pop("gate")
        for k in range(EXP_ROWS):
            ye_ref[c["r0"] + k] = (ye[k * CAP:(k + 1) * CAP, :] * gate[k]).astype(BF16)

    _skewed([gates, up, act, down, out], [dict(r0=g * EXP_ROWS) for g in range(N_ROWS // EXP_ROWS)])


def _expert_call(l, slot, aff, xs, w_gate, w_up, w_down):
    rows = N_ROWS * N_EXP
    return pl.pallas_call(
        _expert_kernel,
        out_shape=jax.ShapeDtypeStruct((N_EXP, N_ROWS, CAP, D), BF16),
        grid_spec=pltpu.PrefetchScalarGridSpec(
            num_scalar_prefetch=1,
            grid=(N_EXP,),
            in_specs=[
                pl.BlockSpec((rows, ROW), lambda e, l: (0, 0), pipeline_mode=pl.Buffered(1)),
                pl.BlockSpec((rows, ROW), lambda e, l: (0, 0), pipeline_mode=pl.Buffered(1)),
                pl.BlockSpec((None, N_ROWS, CAP, D), lambda e, l: (e, 0, 0, 0)),
                pl.BlockSpec((None, None, D, FF), lambda e, l: (l[0], e, 0, 0)),
                pl.BlockSpec((None, None, D, FF), lambda e, l: (l[0], e, 0, 0)),
                pl.BlockSpec((None, None, FF, D), lambda e, l: (l[0], e, 0, 0)),
            ],
            out_specs=pl.BlockSpec((None, N_ROWS, CAP, D), lambda e, l: (e, 0, 0, 0)),
            scratch_shapes=[pltpu.VMEM((D, FF), BF16), pltpu.VMEM((D, FF), BF16), pltpu.VMEM((FF, D), BF16)]),
        compiler_params=pltpu.CompilerParams(
            dimension_semantics=("arbitrary",), vmem_limit_bytes=VMEM_LIMIT),
        name="experts",
    )(_layer_arg(l), slot, aff, xs, w_gate, w_up, w_down)


COMB_TC = 256


def _make_combine_kernel(final, first_row):
    def kern(*refs):
        if final:
            _, slot_ref, ye_ref, xmid_ref, g2_ref, fg_ref, o_ref = refs
        else:
            _, slot_ref, ye_ref, xmid_ref, g2_ref, o_ref = refs
        def run(ctx):
            mod_row = CTX_MOD_ROW if ctx else first_row + pl.program_id(0) - P_ROWS
            g2 = g2_ref[pl.ds(mod_row, 1), :]
            nslot = CAP_P if ctx else CAP
            j = lax.broadcasted_iota(I32, (nslot, COMB_TC), 0)
            if not ctx:
                ye = ye_ref[...].reshape(N_EXP * CAP, D)
            for t in range(ROW // COMB_TC):
                cols = slice(t * COMB_TC, (t + 1) * COMB_TC)
                s0 = t * CAP_P if ctx else 0
                onehot = jnp.concatenate(
                    [jnp.where(slot_ref[e:e + 1, cols] == j + s0, 1.0, 0.0).astype(BF16) for e in range(N_EXP)],
                    axis=0)
                if ctx:
                    ye = jnp.concatenate([ye_ref[e, s0:s0 + CAP_P, :] for e in range(N_EXP)], axis=0)
                moe = lax.dot_general(onehot, ye, TN, preferred_element_type=F32)
                xn = xmid_ref[cols, :] + g2 * moe
                if final:
                    xn = _rms(xn, fg_ref[...])
                o_ref[cols, :] = xn

        is_ctx = first_row + pl.program_id(0) < P_ROWS

        @pl.when(is_ctx)
        def _():
            run(True)

        @pl.when(jnp.logical_not(is_ctx))
        def _():
            run(False)

    return kern


def _combine_call(l, slot, ye, xmid, mods, final_g, row0, n_rows):
    final = final_g is not None
    in_specs = [
        pl.BlockSpec((N_EXP, ROW), lambda r, l: (row0 + r, 0)),
        pl.BlockSpec((N_EXP, None, CAP, D), lambda r, l: (0, row0 + r, 0, 0)),
        pl.BlockSpec((None, ROW, D), lambda r, l: (row0 + r, 0, 0)),
        pl.BlockSpec((None, MOD_ROWS, D), lambda r, l: (l[0], 0, 5), pipeline_mode=pl.Buffered(1)),
    ]
    args = [slot, ye, xmid, mods]
    if final:
        in_specs.append(pl.BlockSpec((1, D), lambda r, l: (0, 0)))
        args.append(final_g)
    return pl.pallas_call(
        _make_combine_kernel(final, row0),
        out_shape=jax.ShapeDtypeStruct((n_rows, ROW, D), F32),
        grid_spec=pltpu.PrefetchScalarGridSpec(
            num_scalar_prefetch=1,
            grid=(n_rows,),
            in_specs=in_specs,
            out_specs=pl.BlockSpec((None, ROW, D), lambda r, l: (r, 0, 0))),
        compiler_params=pltpu.CompilerParams(
            dimension_semantics=("arbitrary",), vmem_limit_bytes=VMEM_LIMIT),
        name="combine",
    )(_layer_arg(l), *args)


def _rope_rot(w, axis):
    q = QK_ROPE // 4
    part = lambda a, b: lax.slice_in_dim(w, a * q, b * q, axis=axis)
    return jnp.concatenate([-part(1, 2), part(0, 1), -part(3, 4), part(2, 3)], axis=axis)


def _repack_kernel(win_ref, wuq_ref, wout_ref, win_o, wuq_o, wout_o):
    o_conv = Q_RANK + KV_RANK + QK_ROPE
    wt = win_ref[...]
    win_o[...] = jnp.concatenate(
        [wt[:o_conv], _rope_rot(wt[o_conv - QK_ROPE:o_conv], 0), wt[o_conv:]], axis=0).T.astype(BF16)
    u = wuq_ref[...]
    hd = QK_NOPE + QK_ROPE
    pe = [u[:, h * hd + QK_NOPE:(h + 1) * hd] for h in range(HEADS)]
    wuq_o[...] = jnp.concatenate(
        [u[:, h * hd:h * hd + QK_NOPE] for h in range(HEADS)]
        + [piece for h in range(HEADS) for piece in (pe[h], _rope_rot(pe[h], 1))], axis=1).astype(BF16)
    wout_o[...] = wout_ref[...].astype(BF16)


def _repack_call(w_in, w_uq, w_out):
    in_w, uq_w = w_in.shape[-1], w_uq.shape[-1]
    w_in = jnp.swapaxes(w_in, 1, 2)
    return pl.pallas_call(
        _repack_kernel,
        out_shape=[jax.ShapeDtypeStruct((DEPTH, D, C_END), BF16),
                   jax.ShapeDtypeStruct((DEPTH, Q_RANK, U_END), BF16),
                   jax.ShapeDtypeStruct((DEPTH, D, D), BF16)],
        grid=(DEPTH,),
        in_specs=[pl.BlockSpec((None, in_w, D), lambda l: (l, 0, 0)),
                  pl.BlockSpec((None, Q_RANK, uq_w), lambda l: (l, 0, 0)),
                  pl.BlockSpec((None, D, D), lambda l: (l, 0, 0))],
        out_specs=[pl.BlockSpec((None, D, C_END), lambda l: (l, 0, 0)),
                   pl.BlockSpec((None, Q_RANK, U_END), lambda l: (l, 0, 0)),
                   pl.BlockSpec((None, D, D), lambda l: (l, 0, 0))],
        compiler_params=pltpu.CompilerParams(
            dimension_semantics=("arbitrary",), vmem_limit_bytes=VMEM_LIMIT),
        name="repack",
    )(w_in, w_uq, w_out)


def _prep_weights(norm1_g, norm2_g, w_in, q_norm_g, kv_norm_g, w_uq, w_ukv, conv_w, spatial_w,
                  spatial_b, w_out, w_router):
    w_in_p, w_uq_p, w_out_p = _repack_call(w_in, w_uq, w_out)
    sb = jnp.repeat(jnp.swapaxes(spatial_b, 1, 2), CMLP_W // CMLP_G, axis=-1)
    return dict(
        norm1_g=norm1_g.reshape(DEPTH, 1, D), norm2_g=norm2_g.reshape(DEPTH, 1, D),
        w_in=w_in_p, q_norm_g=q_norm_g.reshape(DEPTH, 1, Q_RANK),
        kv_norm_g=kv_norm_g.reshape(DEPTH, 1, KV_RANK), w_uq=w_uq_p, w_ukv=w_ukv.astype(BF16),
        conv_w=conv_w, spatial_w=spatial_w.reshape(DEPTH, CMLP_G * CHUNK, CHUNK).astype(BF16),
        spatial_b=sb, w_out=w_out_p, w_router_t=jnp.swapaxes(w_router, 1, 2))


def _rope_table(n):
    rows = n // GRID_W
    row = jnp.repeat(jnp.arange(rows), GRID_W).astype(F32)
    col = jnp.tile(jnp.arange(GRID_W), rows).astype(F32)
    n_freq = QK_ROPE // 4
    inv = ROPE_BASE ** (-jnp.arange(n_freq, dtype=F32) / n_freq)
    ang_r, ang_c = row[:, None] * inv, col[:, None] * inv
    ang = jnp.concatenate([ang_r, ang_r, ang_c, ang_c], axis=-1)
    return jnp.concatenate([jnp.cos(ang), jnp.sin(ang)], axis=-1)


def kernel(x_prompt, x_sample, cache_ckv, cache_krope, c, c_ctx, w_ada, b_ada, norm1_g, norm2_g, w_in,
           q_norm_g, kv_norm_g, w_uq, w_ukv, conv_w, spatial_w, spatial_b, w_out, w_router, w_gate,
           w_up, w_down, final_norm_g):
    wts = _prep_weights(norm1_g, norm2_g, w_in, q_norm_g, kv_norm_g, w_uq, w_ukv, conv_w, spatial_w,
                        spatial_b, w_out, w_router)
    rope_tab = _rope_table(DEC_SEQ)
    krope_t = jnp.swapaxes(cache_krope, 2, 3)
    cond = jnp.concatenate(
        [c, c_ctx[None, :], jnp.zeros((MOD_ROWS - DEC_BATCH - 1, D), F32)], axis=0)
    mods = _ada_call(cond, w_ada, b_ada)
    final_g = final_norm_g.reshape(1, D)

    x = (x_prompt.reshape(P_ROWS, ROW, D), x_sample)
    ckv_list, kpe_list = [], []
    for l in range(DEPTH):
        xmid, h2, aff, ckv, kpe = _mixer_call(l, x, mods, wts, rope_tab, (cache_ckv, krope_t))
        ckv_list.append(ckv.reshape(BATCH, SEQ, KV_RANK))
        kpe_list.append(kpe)
        aff = aff.reshape(N_ROWS * N_EXP, ROW)
        slot = _route_call(aff)
        ye = _expert_call(l, slot, aff, _gather_call(slot, h2), w_gate, w_up, w_down)
        if l < DEPTH - 1:
            x = _combine_call(l, slot, ye, xmid, mods, None, 0, N_ROWS)
        else:
            y_prompt = _combine_call(l, slot, ye, xmid, mods, final_g, 0, P_ROWS)
            y_sample = _combine_call(l, slot, ye, xmid, mods, final_g, P_ROWS, DEC_BATCH)
    return (y_prompt.reshape(BATCH, SEQ, D), y_sample,
            jnp.stack(ckv_list, axis=1), jnp.swapaxes(jnp.stack(kpe_list, axis=1), 2, 3))
```

```python
import jax
import jax.numpy as jnp
from jax import lax
from jax.experimental import pallas as pl
from jax.experimental.pallas import tpu as pltpu

F32 = jnp.float32
BF16 = jnp.bfloat16
I32 = jnp.int32

D = 1024
BATCH = 16
SEQ = 256
DEPTH = 4
DEC_BATCH = 8
DEC_SEQ = 1024
PAST_LEN = 512
GRID_W = 64
HEADS = 4
QK_NOPE = 128
QK_ROPE = 64
V_HEAD = 128
Q_RANK = 256
KV_RANK = 128
CONV_W = 256
CMLP_W = 256
CMLP_G = 4
CHUNK = 128
N_EXP = 16
EC_FACTOR = 2
FF = D // 2
ROPE_BASE = 10000.0
EPS = 1e-6
LOG2_E = 1.4426950408889634

LANES = 128
ROW = 1024
N_ROWS = (BATCH * SEQ + DEC_BATCH * DEC_SEQ) // ROW
P_ROWS = BATCH * SEQ // ROW
P_PER_ROW = ROW // SEQ
CAP = EC_FACTOR * ROW // N_EXP
CAP_P = EC_FACTOR * SEQ // N_EXP
CTX_MOD_ROW = DEC_BATCH
MOD_ROWS = 16
QK_PAD = QK_NOPE + LANES

C_Q, C_KV, C_KPE, C_CONV, C_CMLP, C_END = 0, 256, 384, 512, 1280, 1792
U_NOPE, U_PE, U_END = 0, HEADS * QK_NOPE, HEADS * (QK_NOPE + LANES)
TB = 512
assert 2 * QK_ROPE == LANES and TB % SEQ == 0
TQ = 256
Z_HALO = 8
VMEM_LIMIT = 56 * 1024 * 1024

NT = (((1,), (1,)), ((), ()))
TN = (((0,), (0,)), ((), ()))


def _rms(x, g):
    return x * lax.rsqrt(jnp.mean(x * x, axis=-1, keepdims=True) + EPS) * g


def _silu(x):
    return x / (1.0 + jnp.exp(-x))


def _dot(a, b):
    return jnp.dot(a, b, preferred_element_type=F32)


def _dot_nt(a, b):
    return lax.dot_general(a, b, NT, preferred_element_type=F32)


def _skewed(stages, items):
    n = len(stages)
    for t in range(len(items) + n - 1):
        for s in range(n - 1, -1, -1):
            if 0 <= t - s < len(items):
                stages[s](items[t - s])


def _ada_kernel(c_ref, w_ref, b_ref, o_ref):
    s = _silu(c_ref[...]).astype(BF16)
    o_ref[...] = _dot(s, w_ref[...].astype(BF16)) + b_ref[...]


def _ada_call(cond, w_ada, b_ada):
    nc, wc = 2, 3 * D
    return pl.pallas_call(
        _ada_kernel,
        out_shape=jax.ShapeDtypeStruct((DEPTH, MOD_ROWS, 6 * D), F32),
        grid=(DEPTH, nc),
        in_specs=[
            pl.BlockSpec((MOD_ROWS, D), lambda l, j: (0, 0)),
            pl.BlockSpec((None, D, wc), lambda l, j: (l, 0, j)),
            pl.BlockSpec((None, 1, wc), lambda l, j: (l, 0, j)),
        ],
        out_specs=pl.BlockSpec((None, MOD_ROWS, wc), lambda l, j: (l, 0, j)),
        compiler_params=pltpu.CompilerParams(
            dimension_semantics=("arbitrary", "arbitrary"), vmem_limit_bytes=VMEM_LIMIT),
        name="ada_mod",
    )(cond, w_ada, b_ada.reshape(DEPTH, 1, 6 * D))


def _make_mixer_kernel(split_x):
    def kern(l_ref, *refs):
        if split_x:
            _mixer_body(*refs)
        else:
            _mixer_body(refs[0], *refs)
    return kern


def _mixer_body(xp_ref, xs_ref, mod_ref, g1n_ref, g2n_ref, win_ref, qg_ref, kvg_ref, wuq_ref, wukv_ref,
                cw_ref, sw_ref, sb_ref, wout_ref, wr_ref, cs_ref, cckv_ref, ckr_ref,
                xmid_ref, h2_ref, aff_ref, ckv_out, kpe_out,
                qf_s, kf_s, v_s, mix_s, gb_s, z_s):
    qk_scale = float(QK_NOPE + QK_ROPE) ** -0.5 * LOG2_E
    nblk = ROW // TB

    def run(ctx):
        x_ref = xp_ref if ctx else xs_ref
        n_past = 0 if ctx else PAST_LEN
        seg = SEQ if ctx else ROW
        z_s[0:Z_HALO, :] = jnp.zeros((Z_HALO, CONV_W), F32)
        z_s[ROW + Z_HALO:ROW + 2 * Z_HALO, :] = jnp.zeros((Z_HALO, CONV_W), F32)
        mod_row = CTX_MOD_ROW if ctx else pl.program_id(0) - P_ROWS
        mod = mod_ref[pl.ds(mod_row, 1), :]
        sh1, sc1, g1 = mod[:, 0:D], mod[:, D:2 * D], mod[:, 2 * D:3 * D]
        sh2, sc2, g2 = mod[:, 3 * D:4 * D], mod[:, 4 * D:5 * D], mod[:, 5 * D:6 * D]

        def store_kv(row0, rows, kv, kpb):
            for h in range(HEADS):
                c0 = h * (QK_NOPE + V_HEAD)
                kf_s[h, pl.ds(row0, rows), 0:QK_NOPE] = kv[:, c0:c0 + QK_NOPE].astype(BF16)
                kf_s[h, pl.ds(row0, rows), QK_NOPE:QK_PAD] = kpb
                v_s[h, pl.ds(row0, rows), :] = kv[:, c0 + QK_NOPE:c0 + QK_NOPE + V_HEAD].astype(BF16)

        if not ctx:
            kr_t = ckr_ref[...]
            store_kv(0, PAST_LEN, _dot(cckv_ref[...].astype(BF16), wukv_ref[...]),
                     jnp.concatenate([kr_t, jnp.zeros_like(kr_t)], axis=0).T.astype(BF16))

        grp_shift = (CMLP_W // CMLP_G).bit_length() - 1
        lane_grp = lax.shift_right_logical(lax.broadcasted_iota(I32, (CHUNK, CMLP_W), 1), grp_shift)

        def p1_norm(c):
            x = x_ref[pl.ds(c["r0"], TB), :]
            c["hb"] = (_rms(x, g1n_ref[...]) * (1.0 + sc1) + sh1).astype(BF16)

        def p1_proj(c):
            c["pa"] = _dot(c["hb"], win_ref[:, C_Q:C_CONV])
            c["cv"] = _dot(c["hb"], win_ref[:, C_CONV:C_CMLP])
            c["cm"] = _dot(c.pop("hb"), win_ref[:, C_CMLP:C_END])

        def p1_mid(c):
            r0, pa, cv, cm = c["r0"], c.pop("pa"), c.pop("cv"), c.pop("cm")
            kp = pa[:, C_KPE:C_CONV]
            c["cqn"] = _rms(pa[:, C_Q:C_KV], qg_ref[...]).astype(BF16)
            ckv_n = _rms(pa[:, C_KV:C_KPE], kvg_ref[...])
            if ctx:
                ckv_out[pl.ds(r0, TB), :] = ckv_n
                kp_t = kp.T
                for i in range(TB // SEQ):
                    kpe_out[r0 // SEQ + i] = kp_t[0:QK_ROPE, i * SEQ:(i + 1) * SEQ]
                kpe = kp
            else:
                y = kp * cs_ref[pl.ds(r0, TB), :]
                kpe = y + pltpu.roll(y, QK_ROPE, 1)
            c["ckv_b"], c["kpe_b"] = ckv_n.astype(BF16), kpe.astype(BF16)
            gb_s[pl.ds(r0, TB), :] = cv[:, 0:CONV_W]
            z_s[pl.ds(Z_HALO + r0, TB), :] = cv[:, CONV_W:2 * CONV_W] * cv[:, 2 * CONV_W:3 * CONV_W]
            c["u"], c["vvb"] = cm[:, 0:CMLP_W], cm[:, CMLP_W:2 * CMLP_W].astype(BF16)

        def p1_up(c):
            c["qa"] = _dot(c.pop("cqn"), wuq_ref[...])
            c["kv"] = _dot(c.pop("ckv_b"), wukv_ref[...])
            vvb = c.pop("vvb")
            c["r"] = [_dot(sw_ref[...], vvb[k * CHUNK:(k + 1) * CHUNK, :]) for k in range(TB // CHUNK)]

        def p1_out(c):
            r0, qa, u = c["r0"], c.pop("qa"), c.pop("u")
            rope_lanes = lax.broadcasted_iota(I32, (TB, LANES), 1) < QK_ROPE
            for h in range(HEADS):
                qn = qa[:, U_NOPE + h * QK_NOPE:U_NOPE + (h + 1) * QK_NOPE]
                qp = qa[:, U_PE + h * LANES:U_PE + (h + 1) * LANES]
                if not ctx:
                    y = qp * cs_ref[pl.ds(r0, TB), :]
                    qp = y + pltpu.roll(y, QK_ROPE, 1)
                qp = jnp.where(rope_lanes, qp, 0.0)
                qf_s[h, pl.ds(r0, TB), 0:QK_NOPE] = (qn * qk_scale).astype(BF16)
                qf_s[h, pl.ds(r0, TB), QK_NOPE:QK_PAD] = (qp * qk_scale).astype(BF16)
            store_kv(n_past + r0, TB, c.pop("kv"), c.pop("kpe_b"))
            for k, r in enumerate(c.pop("r")):
                mixed = jnp.where(
                    lane_grp == 0, r[0:CHUNK],
                    jnp.where(lane_grp == 1, r[CHUNK:2 * CHUNK],
                              jnp.where(lane_grp == 2, r[2 * CHUNK:3 * CHUNK], r[3 * CHUNK:4 * CHUNK])))
                out = u[k * CHUNK:(k + 1) * CHUNK, :] * (mixed + sb_ref[...])
                mix_s[pl.ds(r0 + k * CHUNK, CHUNK), HEADS * V_HEAD + CONV_W:D] = out.astype(BF16)

        _skewed([p1_norm, p1_proj, p1_mid, p1_up, p1_out], [dict(r0=i * TB) for i in range(nblk)])

        def at_qk(c):
            keys = kf_s[c["h"], pl.ds(c["k0"], seg + n_past), :]
            c["s"] = _dot_nt(qf_s[c["h"], pl.ds(c["r0"], TQ), :], keys)

        def at_max(c):
            c["m"] = jnp.max(c["s"], axis=-1, keepdims=True)

        def at_exp(c):
            e = jnp.exp2(c.pop("s") - c.pop("m"))
            c["l"] = jnp.sum(e, axis=-1, keepdims=True)
            c["p"] = e.astype(BF16)

        def at_pv(c):
            c["o"] = _dot(c.pop("p"), v_s[c["h"], pl.ds(c["k0"], seg + n_past), :])

        def at_out(c):
            o = c.pop("o") / c.pop("l")
            mix_s[pl.ds(c["r0"], TQ), c["h"] * V_HEAD:(c["h"] + 1) * V_HEAD] = o.astype(BF16)

        at_stages = [at_qk, lambda c: (at_max(c), at_exp(c)), lambda c: (at_pv(c), at_out(c))]
        if ctx:
            _skewed(at_stages, [dict(r0=b * TQ, k0=b * TQ, h=h) for b in range(ROW // TQ) for h in range(HEADS)])
        else:
            def at_block(b, carry):
                r0 = pl.multiple_of(b * TQ, TQ)
                _skewed(at_stages, [dict(r0=r0, k0=0, h=h) for h in range(HEADS)])
                return carry

            lax.fori_loop(0, ROW // TQ, at_block, 0)

        wr = wr_ref[...]
        wr_hi = wr.astype(BF16)
        wr_hl = jnp.concatenate([wr_hi, (wr - wr_hi.astype(F32)).astype(BF16)], axis=0)

        def p2_conv(c):
            r0 = c["r0"]
            z = z_s[Z_HALO + r0:Z_HALO + r0 + TB, :]
            pos = (lax.broadcasted_iota(I32, (TB, CONV_W), 0) + r0) & (seg - 1)
            zm = jnp.where(pos == 0, 0.0, z_s[Z_HALO + r0 - 1:Z_HALO + r0 - 1 + TB, :])
            zp = jnp.where(pos == seg - 1, 0.0, z_s[Z_HALO + r0 + 1:Z_HALO + r0 + 1 + TB, :])
            cw = cw_ref[...]
            conv = zm * cw[0:1, :] + z * cw[1:2, :] + zp * cw[2:3, :]
            mix_s[pl.ds(r0, TB), HEADS * V_HEAD:HEADS * V_HEAD + CONV_W] = (
                gb_s[pl.ds(r0, TB), :] * conv).astype(BF16)

        def p2_proj(c):
            c["mo"] = _dot(mix_s[pl.ds(c["r0"], TB), :], wout_ref[...])

        def p2_norm(c):
            r0 = c["r0"]
            xm = x_ref[pl.ds(r0, TB), :] + g1 * c.pop("mo")
            xmid_ref[pl.ds(r0, TB), :] = xm
            h2 = _rms(xm, g2n_ref[...]) * (1.0 + sc2) + sh2
            h2_hi = h2.astype(BF16)
            h2_ref[pl.ds(r0, TB), :] = h2_hi
            c["hi"], c["lo"] = h2_hi, (h2 - h2_hi.astype(F32)).astype(BF16)

        def p2_route(c):
            la = _dot_nt(wr_hl, c["hi"])
            c["lg"] = la[0:N_EXP] + la[N_EXP:2 * N_EXP] + _dot_nt(wr_hi, c.pop("lo"))
            c.pop("hi")

        def p2_aff(c):
            lg = c.pop("lg")
            e = jnp.exp(lg - jnp.max(lg, axis=0, keepdims=True))
            aff_ref[:, pl.ds(c["r0"], TB)] = e / jnp.sum(e, axis=0, keepdims=True)

        _skewed([p2_conv, p2_proj, p2_norm, p2_route, p2_aff], [dict(r0=i * TB) for i in range(nblk)])

    is_ctx = pl.program_id(0) < P_ROWS

    @pl.when(is_ctx)
    def _():
        run(True)

    @pl.when(jnp.logical_not(is_ctx))
    def _():
        run(False)


def _layer_spec(shape):
    nd = len(shape)
    return pl.BlockSpec((None,) + shape, lambda r, l: (l[0],) + (0,) * nd, pipeline_mode=pl.Buffered(1))


def _layer_arg(l):
    return jnp.full((1,), l, I32)


def _mixer_call(l, x, mods, wts, rope_tab, caches):
    lk = PAST_LEN + ROW
    lat = lambda r: jnp.maximum(r - P_ROWS, 0)
    ctx = lambda r: jnp.minimum(r, P_ROWS - 1)
    split_x = isinstance(x, tuple)
    if split_x:
        x_specs = [pl.BlockSpec((None, ROW, D), lambda r, l: (ctx(r), 0, 0)),
                   pl.BlockSpec((None, ROW, D), lambda r, l: (lat(r), 0, 0))]
    else:
        x_specs, x = [pl.BlockSpec((None, ROW, D), lambda r, l: (r, 0, 0))], (x,)
    in_specs = x_specs + [
        _layer_spec((MOD_ROWS, 6 * D)),
        _layer_spec((1, D)), _layer_spec((1, D)),
        _layer_spec((D, C_END)),
        _layer_spec((1, Q_RANK)), _layer_spec((1, KV_RANK)),
        _layer_spec((Q_RANK, U_END)),
        _layer_spec((KV_RANK, HEADS * (QK_NOPE + V_HEAD))),
        _layer_spec((3, CONV_W)),
        _layer_spec((CMLP_G * CHUNK, CHUNK)),
        _layer_spec((CHUNK, CMLP_W)),
        _layer_spec((D, D)),
        _layer_spec((N_EXP, D)),
        pl.BlockSpec((ROW, LANES), lambda r, l: (0, 0), pipeline_mode=pl.Buffered(1)),
        pl.BlockSpec((None, None, PAST_LEN, KV_RANK), lambda r, l: (lat(r), l[0], 0, 0)),
        pl.BlockSpec((None, None, QK_ROPE, PAST_LEN), lambda r, l: (lat(r), l[0], 0, 0)),
    ]
    args = list(x) + [mods, wts["norm1_g"], wts["norm2_g"], wts["w_in"], wts["q_norm_g"], wts["kv_norm_g"],
            wts["w_uq"], wts["w_ukv"], wts["conv_w"], wts["spatial_w"], wts["spatial_b"],
            wts["w_out"], wts["w_router_t"], rope_tab, caches[0], caches[1]]
    out_shape = [jax.ShapeDtypeStruct((N_ROWS, ROW, D), F32),
                 jax.ShapeDtypeStruct((N_ROWS, ROW, D), BF16),
                 jax.ShapeDtypeStruct((N_ROWS, N_EXP, ROW), F32),
                 jax.ShapeDtypeStruct((P_ROWS, ROW, KV_RANK), F32),
                 jax.ShapeDtypeStruct((BATCH, QK_ROPE, SEQ), F32)]
    out_specs = [pl.BlockSpec((None, ROW, D), lambda r, l: (r, 0, 0)),
                 pl.BlockSpec((None, ROW, D), lambda r, l: (r, 0, 0)),
                 pl.BlockSpec((None, N_EXP, ROW), lambda r, l: (r, 0, 0)),
                 pl.BlockSpec((None, ROW, KV_RANK), lambda r, l: (ctx(r), 0, 0)),
                 pl.BlockSpec((P_PER_ROW, QK_ROPE, SEQ), lambda r, l: (ctx(r), 0, 0))]
    return pl.pallas_call(
        _make_mixer_kernel(split_x),
        out_shape=out_shape,
        grid_spec=pltpu.PrefetchScalarGridSpec(
            num_scalar_prefetch=1,
            grid=(N_ROWS,),
            in_specs=in_specs,
            out_specs=out_specs,
            scratch_shapes=[
                pltpu.VMEM((HEADS, ROW, QK_PAD), BF16),
                pltpu.VMEM((HEADS, lk, QK_PAD), BF16),
                pltpu.VMEM((HEADS, lk, V_HEAD), BF16),
                pltpu.VMEM((ROW, D), BF16),
                pltpu.VMEM((ROW, CONV_W), F32),
                pltpu.VMEM((ROW + 2 * Z_HALO, CONV_W), F32),
            ]),
        compiler_params=pltpu.CompilerParams(
            dimension_semantics=("arbitrary",), vmem_limit_bytes=VMEM_LIMIT),
        name="mixer",
    )(_layer_arg(l), *args)


def _cap_thresholds(affs, caps):
    def bit_step(i, ts):
        bit = jnp.left_shift(jnp.int32(1), 30 - i)
        out = []
        for aff, cap, t in zip(affs, caps, ts):
            cand = t | bit
            cnt = jnp.sum(jnp.where(aff >= pltpu.bitcast(cand, F32), 1.0, 0.0), axis=1, keepdims=True)
            out.append(jnp.where(cnt >= float(cap), cand, t))
        return tuple(out)

    ts = lax.fori_loop(0, 31, bit_step, tuple(jnp.zeros((a.shape[0], 1), I32) for a in affs))
    return [pltpu.bitcast(t, F32) for t in ts]


def _select_slots(aff, thr, cap, tri):
    capf = float(cap)
    gt = aff > thr
    eq = aff == thr
    n_gt = jnp.sum(jnp.where(gt, 1.0, 0.0), axis=1, keepdims=True)
    eq_rank = _dot(jnp.where(eq, 1.0, 0.0).astype(BF16), tri)
    sel = jnp.where(gt, 1.0, jnp.where(eq, jnp.where(eq_rank < capf - n_gt, 1.0, 0.0), 0.0))
    pos = _dot(sel.astype(BF16), tri)
    return jnp.where(sel > 0.5, pos.astype(I32), -1)


def _route_kernel(aff_ref, slot_ref, tri_s):
    r = lax.broadcasted_iota(I32, (ROW, ROW), 0)
    c = lax.broadcasted_iota(I32, (ROW, ROW), 1)
    tri_s[...] = jnp.where(r < c, 1.0, 0.0).astype(BF16)
    np_ = P_ROWS * N_EXP
    affs = [aff_ref[0:np_, s * SEQ:(s + 1) * SEQ] for s in range(P_PER_ROW)] + [aff_ref[np_:, :]]
    caps = [CAP_P] * P_PER_ROW + [CAP]
    thrs = _cap_thresholds(affs, caps)
    for s in range(P_PER_ROW):
        sl = _select_slots(affs[s], thrs[s], CAP_P, tri_s[0:SEQ, 0:SEQ])
        slot_ref[0:np_, s * SEQ:(s + 1) * SEQ] = jnp.where(sl >= 0, sl + s * CAP_P, -1)
    slot_ref[np_:, :] = _select_slots(affs[-1], thrs[-1], CAP, tri_s[...])


def _route_call(aff):
    rows = N_ROWS * N_EXP
    return pl.pallas_call(
        _route_kernel,
        out_shape=jax.ShapeDtypeStruct((rows, ROW), I32),
        scratch_shapes=[pltpu.VMEM((ROW, ROW), BF16)],
        compiler_params=pltpu.CompilerParams(vmem_limit_bytes=VMEM_LIMIT),
        name="route",
    )(aff)


GATHER_NC = 256


def _gather_kernel(slot_ref, h2_ref, xs_ref, hot_s):
    @pl.when(pl.program_id(0) >= P_ROWS)
    def _():
        j = lax.broadcasted_iota(I32, (CAP, ROW), 0)
        for e in range(N_EXP):
            hot_s[e * CAP:(e + 1) * CAP, :] = jnp.where(slot_ref[e:e + 1, :] == j, 1.0, 0.0).astype(BF16)
        for c in range(D // GATHER_NC):
            xs = _dot(hot_s[...], h2_ref[:, c * GATHER_NC:(c + 1) * GATHER_NC]).astype(BF16)
            for e in range(N_EXP):
                xs_ref[e, :, c * GATHER_NC:(c + 1) * GATHER_NC] = xs[e * CAP:(e + 1) * CAP, :]

    @pl.when(pl.program_id(0) < P_ROWS)
    def _():
        j = lax.broadcasted_iota(I32, (CAP_P, SEQ), 0)
        for s in range(P_PER_ROW):
            toks = slice(s * SEQ, (s + 1) * SEQ)
            for e in range(N_EXP):
                hot_s[e * CAP_P:(e + 1) * CAP_P, 0:SEQ] = jnp.where(
                    slot_ref[e:e + 1, toks] == j + s * CAP_P, 1.0, 0.0).astype(BF16)
            xs = _dot(hot_s[0:N_EXP * CAP_P, 0:SEQ], h2_ref[toks, :]).astype(BF16)
            for e in range(N_EXP):
                xs_ref[e, s * CAP_P:(s + 1) * CAP_P, :] = xs[e * CAP_P:(e + 1) * CAP_P, :]


def _gather_call(slot, h2):
    return pl.pallas_call(
        _gather_kernel,
        out_shape=jax.ShapeDtypeStruct((N_EXP, N_ROWS, CAP, D), BF16),
        grid=(N_ROWS,),
        in_specs=[
            pl.BlockSpec((N_EXP, ROW), lambda r: (r, 0)),
            pl.BlockSpec((None, ROW, D), lambda r: (r, 0, 0)),
        ],
        out_specs=pl.BlockSpec((N_EXP, None, CAP, D), lambda r: (0, r, 0, 0)),
        scratch_shapes=[pltpu.VMEM((N_EXP * CAP, ROW), BF16)],
        compiler_params=pltpu.CompilerParams(
            dimension_semantics=("arbitrary",), vmem_limit_bytes=VMEM_LIMIT),
        name="gather",
    )(slot, h2)


EXP_ROWS = 4


def _expert_kernel(l_ref, slot_ref, aff_ref, xs_ref, wg_ref, wu_ref, wd_ref, ye_ref, wg_s, wu_s, wd_s):
    e = pl.program_id(0)
    wg_s[...] = wg_ref[...].astype(BF16)
    wu_s[...] = wu_ref[...].astype(BF16)
    wd_s[...] = wd_ref[...].astype(BF16)
    j = lax.broadcasted_iota(I32, (CAP, ROW), 0)

    def gates(c):
        c["gate"] = []
        for k in range(EXP_ROWS):
            row = (c["r0"] + k) * N_EXP + e
            hit = slot_ref[pl.ds(row, 1), :] == j
            c["gate"].append(jnp.sum(jnp.where(hit, aff_ref[pl.ds(row, 1), :], 0.0), axis=1, keepdims=True))

    def up(c):
        xs = xs_ref[c["r0"]:c["r0"] + EXP_ROWS].reshape(EXP_ROWS * CAP, D)
        c["a"], c["u"] = _dot(xs, wg_s[...]), _dot(xs, wu_s[...])

    def act(c):
        c["hid"] = (_silu(c.pop("a")) * c.pop("u")).astype(BF16)

    def down(c):
        c["ye"] = _dot(c.pop("hid"), wd_s[...])

    def out(c):
        ye, gate = c.pop("ye"), c.pop("gate")
        for k in range(EXP_ROWS):
            ye_ref[c["r0"] + k] = (ye[k * CAP:(k + 1) * CAP, :] * gate[k]).astype(BF16)

    _skewed([gates, up, act, down, out], [dict(r0=g * EXP_ROWS) for g in range(N_ROWS // EXP_ROWS)])


def _expert_call(l, slot, aff, xs, w_gate, w_up, w_down):
    rows = N_ROWS * N_EXP
    return pl.pallas_call(
        _expert_kernel,
        out_shape=jax.ShapeDtypeStruct((N_EXP, N_ROWS, CAP, D), BF16),
        grid_spec=pltpu.PrefetchScalarGridSpec(
            num_scalar_prefetch=1,
            grid=(N_EXP,),
            in_specs=[
                pl.BlockSpec((rows, ROW), lambda e, l: (0, 0), pipeline_mode=pl.Buffered(1)),
                pl.BlockSpec((rows, ROW), lambda e, l: (0, 0), pipeline_mode=pl.Buffered(1)),
                pl.BlockSpec((None, N_ROWS, CAP, D), lambda e, l: (e, 0, 0, 0)),
                pl.BlockSpec((None, None, D, FF), lambda e, l: (l[0], e, 0, 0)),
                pl.BlockSpec((None, None, D, FF), lambda e, l: (l[0], e, 0, 0)),
                pl.BlockSpec((None, None, FF, D), lambda e, l: (l[0], e, 0, 0)),
            ],
            out_specs=pl.BlockSpec((None, N_ROWS, CAP, D), lambda e, l: (e, 0, 0, 0)),
            scratch_shapes=[pltpu.VMEM((D, FF), BF16), pltpu.VMEM((D, FF), BF16), pltpu.VMEM((FF, D), BF16)]),
        compiler_params=pltpu.CompilerParams(
            dimension_semantics=("arbitrary",), vmem_limit_bytes=VMEM_LIMIT),
        name="experts",
    )(_layer_arg(l), slot, aff, xs, w_gate, w_up, w_down)


COMB_TC = 256


def _make_combine_kernel(final, first_row):
    def kern(*refs):
        if final:
            _, slot_ref, ye_ref, xmid_ref, g2_ref, fg_ref, o_ref = refs
        else:
            _, slot_ref, ye_ref, xmid_ref, g2_ref, o_ref = refs
        def run(ctx):
            mod_row = CTX_MOD_ROW if ctx else first_row + pl.program_id(0) - P_ROWS
            g2 = g2_ref[pl.ds(mod_row, 1), :]
            nslot = CAP_P if ctx else CAP
            j = lax.broadcasted_iota(I32, (nslot, COMB_TC), 0)
            if not ctx:
                ye = ye_ref[...].reshape(N_EXP * CAP, D)
            for t in range(ROW // COMB_TC):
                cols = slice(t * COMB_TC, (t + 1) * COMB_TC)
                s0 = t * CAP_P if ctx else 0
                onehot = jnp.concatenate(
                    [jnp.where(slot_ref[e:e + 1, cols] == j + s0, 1.0, 0.0).astype(BF16) for e in range(N_EXP)],
                    axis=0)
                if ctx:
                    ye = jnp.concatenate([ye_ref[e, s0:s0 + CAP_P, :] for e in range(N_EXP)], axis=0)
                moe = lax.dot_general(onehot, ye, TN, preferred_element_type=F32)
                xn = xmid_ref[cols, :] + g2 * moe
                if final:
                    xn = _rms(xn, fg_ref[...])
                o_ref[cols, :] = xn

        is_ctx = first_row + pl.program_id(0) < P_ROWS

        @pl.when(is_ctx)
        def _():
            run(True)

        @pl.when(jnp.logical_not(is_ctx))
        def _():
            run(False)

    return kern


def _combine_call(l, slot, ye, xmid, mods, final_g, row0, n_rows):
    final = final_g is not None
    in_specs = [
        pl.BlockSpec((N_EXP, ROW), lambda r, l: (row0 + r, 0)),
        pl.BlockSpec((N_EXP, None, CAP, D), lambda r, l: (0, row0 + r, 0, 0)),
        pl.BlockSpec((None, ROW, D), lambda r, l: (row0 + r, 0, 0)),
        pl.BlockSpec((None, MOD_ROWS, D), lambda r, l: (l[0], 0, 5), pipeline_mode=pl.Buffered(1)),
    ]
    args = [slot, ye, xmid, mods]
    if final:
        in_specs.append(pl.BlockSpec((1, D), lambda r, l: (0, 0)))
        args.append(final_g)
    return pl.pallas_call(
        _make_combine_kernel(final, row0),
        out_shape=jax.ShapeDtypeStruct((n_rows, ROW, D), F32),
        grid_spec=pltpu.PrefetchScalarGridSpec(
            num_scalar_prefetch=1,
            grid=(n_rows,),
            in_specs=in_specs,
            out_specs=pl.BlockSpec((None, ROW, D), lambda r, l: (r, 0, 0))),
        compiler_params=pltpu.CompilerParams(
            dimension_semantics=("arbitrary",), vmem_limit_bytes=VMEM_LIMIT),
        name="combine",
    )(_layer_arg(l), *args)


def _rope_rot(w, axis):
    q = QK_ROPE // 4
    part = lambda a, b: lax.slice_in_dim(w, a * q, b * q, axis=axis)
    return jnp.concatenate([-part(1, 2), part(0, 1), -part(3, 4), part(2, 3)], axis=axis)


def _repack_kernel(win_ref, wuq_ref, wout_ref, win_o, wuq_o, wout_o):
    o_conv = Q_RANK + KV_RANK + QK_ROPE
    wt = win_ref[...]
    win_o[...] = jnp.concatenate(
        [wt[:o_conv], _rope_rot(wt[o_conv - QK_ROPE:o_conv], 0), wt[o_conv:]], axis=0).T.astype(BF16)
    u = wuq_ref[...]
    hd = QK_NOPE + QK_ROPE
    pe = [u[:, h * hd + QK_NOPE:(h + 1) * hd] for h in range(HEADS)]
    wuq_o[...] = jnp.concatenate(
        [u[:, h * hd:h * hd + QK_NOPE] for h in range(HEADS)]
        + [piece for h in range(HEADS) for piece in (pe[h], _rope_rot(pe[h], 1))], axis=1).astype(BF16)
    wout_o[...] = wout_ref[...].astype(BF16)


def _repack_call(w_in, w_uq, w_out):
    in_w, uq_w = w_in.shape[-1], w_uq.shape[-1]
    w_in = jnp.swapaxes(w_in, 1, 2)
    return pl.pallas_call(
        _repack_kernel,
        out_shape=[jax.ShapeDtypeStruct((DEPTH, D, C_END), BF16),
                   jax.ShapeDtypeStruct((DEPTH, Q_RANK, U_END), BF16),
                   jax.ShapeDtypeStruct((DEPTH, D, D), BF16)],
        grid=(DEPTH,),
        in_specs=[pl.BlockSpec((None, in_w, D), lambda l: (l, 0, 0)),
                  pl.BlockSpec((None, Q_RANK, uq_w), lambda l: (l, 0, 0)),
                  pl.BlockSpec((None, D, D), lambda l: (l, 0, 0))],
        out_specs=[pl.BlockSpec((None, D, C_END), lambda l: (l, 0, 0)),
                   pl.BlockSpec((None, Q_RANK, U_END), lambda l: (l, 0, 0)),
                   pl.BlockSpec((None, D, D), lambda l: (l, 0, 0))],
        compiler_params=pltpu.CompilerParams(
            dimension_semantics=("arbitrary",), vmem_limit_bytes=VMEM_LIMIT),
        name="repack",
    )(w_in, w_uq, w_out)


def _prep_weights(norm1_g, norm2_g, w_in, q_norm_g, kv_norm_g, w_uq, w_ukv, conv_w, spatial_w,
                  spatial_b, w_out, w_router):
    w_in_p, w_uq_p, w_out_p = _repack_call(w_in, w_uq, w_out)
    sb = jnp.repeat(jnp.swapaxes(spatial_b, 1, 2), CMLP_W // CMLP_G, axis=-1)
    return dict(
        norm1_g=norm1_g.reshape(DEPTH, 1, D), norm2_g=norm2_g.reshape(DEPTH, 1, D),
        w_in=w_in_p, q_norm_g=q_norm_g.reshape(DEPTH, 1, Q_RANK),
        kv_norm_g=kv_norm_g.reshape(DEPTH, 1, KV_RANK), w_uq=w_uq_p, w_ukv=w_ukv.astype(BF16),
        conv_w=conv_w, spatial_w=spatial_w.reshape(DEPTH, CMLP_G * CHUNK, CHUNK).astype(BF16),
        spatial_b=sb, w_out=w_out_p, w_router_t=jnp.swapaxes(w_router, 1, 2))


def _rope_table(n):
    rows = n // GRID_W
    row = jnp.repeat(jnp.arange(rows), GRID_W).astype(F32)
    col = jnp.tile(jnp.arange(GRID_W), rows).astype(F32)
    n_freq = QK_ROPE // 4
    inv = ROPE_BASE ** (-jnp.arange(n_freq, dtype=F32) / n_freq)
    ang_r, ang_c = row[:, None] * inv, col[:, None] * inv
    ang = jnp.concatenate([ang_r, ang_r, ang_c, ang_c], axis=-1)
    return jnp.concatenate([jnp.cos(ang), jnp.sin(ang)], axis=-1)


def kernel(x_prompt, x_sample, cache_ckv, cache_krope, c, c_ctx, w_ada, b_ada, norm1_g, norm2_g, w_in,
           q_norm_g, kv_norm_g, w_uq, w_ukv, conv_w, spatial_w, spatial_b, w_out, w_router, w_gate,
           w_up, w_down, final_norm_g):
    wts = _prep_weights(norm1_g, norm2_g, w_in, q_norm_g, kv_norm_g, w_uq, w_ukv, conv_w, spatial_w,
                        spatial_b, w_out, w_router)
    rope_tab = _rope_table(DEC_SEQ)
    krope_t = jnp.swapaxes(cache_krope, 2, 3)
    cond = jnp.concatenate(
        [c, c_ctx[None, :], jnp.zeros((MOD_ROWS - DEC_BATCH - 1, D), F32)], axis=0)
    mods = _ada_call(cond, w_ada, b_ada)
    final_g = final_norm_g.reshape(1, D)

    x = (x_prompt.reshape(P_ROWS, ROW, D), x_sample)
    ckv_list, kpe_list = [], []
    for l in range(DEPTH):
        xmid, h2, aff, ckv, kpe = _mixer_call(l, x, mods, wts, rope_tab, (cache_ckv, krope_t))
        ckv_list.append(ckv.reshape(BATCH, SEQ, KV_RANK))
        kpe_list.append(kpe)
        aff = aff.reshape(N_ROWS * N_EXP, ROW)
        slot = _route_call(aff)
        ye = _expert_call(l, slot, aff, _gather_call(slot, h2), w_gate, w_up, w_down)
        if l < DEPTH - 1:
            x = _combine_call(l, slot, ye, xmid, mods, None, 0, N_ROWS)
        else:
            y_prompt = _combine_call(l, slot, ye, xmid, mods, final_g, 0, P_ROWS)
            y_sample = _combine_call(l, slot, ye, xmid, mods, final_g, P_ROWS, DEC_BATCH)
    return (y_prompt.reshape(BATCH, SEQ, D), y_sample,
            jnp.stack(ckv_list, axis=1), jnp.swapaxes(jnp.stack(kpe_list, axis=1), 2, 3))
```
